```python
import math, functools
import jax, jax.numpy as jnp
from jax import lax
import numpy as np

D_MODEL = 1024
BATCH = 16
SEQ = 2048
DEPTH = 2
DEC_BATCH = 128
DEC_SEQ = 8
PAST_LEN = 16384
PAGE_SIZE = 128

MLA_HEADS = 8
MLA_NOPE = 64
MLA_ROPE = 32
MLA_V = 64
MLA_QK = MLA_NOPE + MLA_ROPE
MLA_Q_RANK = 256
MLA_KV_RANK = 128
ROPE_THETA = 10000.0
Q_BLOCK = 128
SSD_HEADS = 4
SSD_HEADDIM = 64
SSD_D = SSD_HEADS * SSD_HEADDIM
SSD_GROUPS = 2
SSD_STATE = 128
SSD_CONV = 4
SSD_CONV_DIM = SSD_D + 2 * SSD_GROUPS * SSD_STATE
SSD_CHUNK = 128
GLA_HEADS = 4
GLA_DK = 32
GLA_DV = 64
GLA_KD = GLA_HEADS * GLA_DK
GLA_VD = GLA_HEADS * GLA_DV
GLA_GATE_RANK = 16
GLA_GATE_TAU = 16.0
GLA_CHUNK = 64
MIX_D = MLA_HEADS * MLA_V + SSD_D + GLA_VD
W_IN_SIZES = (MLA_Q_RANK, MLA_KV_RANK, MLA_ROPE,
              SSD_D, SSD_CONV_DIM, SSD_HEADS,
              GLA_KD, GLA_KD, GLA_VD, GLA_GATE_RANK, GLA_VD)
D_IN = (MLA_Q_RANK + MLA_KV_RANK + MLA_ROPE + SSD_D + SSD_CONV_DIM + SSD_HEADS
        + GLA_KD + GLA_KD + GLA_VD + GLA_GATE_RANK + GLA_VD)
N_MEM = 256
MEM_HEADS = 4
MEM_HEAD_DIM = 64
MEM_D = MEM_HEADS * MEM_HEAD_DIM
D_FF = 2816
FFN_CONV = 3
EPS = 1e-6

kernel_name = "hybrid_mla_ssd_gla_step"


def rms_norm(x, g):
    xf = x.astype(jnp.float32)
    y = xf * lax.rsqrt(jnp.mean(xf * xf, axis=-1, keepdims=True) + EPS)
    return (y * g.astype(jnp.float32)).astype(x.dtype)


def rope(x, pos):
    half = x.shape[-1] // 2
    inv = ROPE_THETA ** (-jnp.arange(half, dtype=jnp.float32) / half)
    ang = pos.astype(jnp.float32)[:, None] * inv[None, :]
    shape = (ang.shape[0],) + (1,) * (x.ndim - 3) + (half,)
    cos, sin = jnp.cos(ang).reshape(shape), jnp.sin(ang).reshape(shape)
    x1 = x[..., :half].astype(jnp.float32)
    x2 = x[..., half:].astype(jnp.float32)
    return jnp.concatenate([x1 * cos - x2 * sin, x2 * cos + x1 * sin], axis=-1).astype(x.dtype)


def causal_dwconv(x_full, w, b):
    K = w.shape[0]
    L = x_full.shape[1] - K + 1
    out = x_full[:, 0:L] * w[0]
    for i in range(1, K):
        out = out + x_full[:, i:i + L] * w[i]
    return out + b


def split_cols(h, sizes):
    out, o = [], 0
    for s in sizes:
        out.append(h[..., o:o + s])
        o += s
    return out


def to_chunks(a, c):
    b, L = a.shape[:2]
    return a.reshape((b, L // c, c) + a.shape[2:]).swapaxes(0, 1)


def from_chunks(a):
    nc, b, c = a.shape[:3]
    return a.swapaxes(0, 1).reshape((b, nc * c) + a.shape[3:])


def mla_project(cq, ckv, krr, pos, lp):
    q = jnp.einsum('blr,rhd->blhd', rms_norm(cq, lp['mla_q_norm']), lp['mla_w_uq'])
    q_nope = rms_norm(q[..., :MLA_NOPE], lp['mla_qn_norm'])
    q_rope = rope(rms_norm(q[..., MLA_NOPE:], lp['mla_qr_norm']), pos)
    c_kv = rms_norm(ckv, lp['mla_kv_norm'])
    kr = rope(rms_norm(krr, lp['mla_kr_norm']), pos)
    return q_nope, q_rope, c_kv, kr


def mla_keys(c, w_uk, kn_g):
    return rms_norm(jnp.einsum('blc,chd->blhd', c, w_uk), kn_g)


def mla_scores(q_nope, q_rope, k_nope, kr):
    s = (jnp.einsum('bqhd,bkhd->bhqk', q_nope, k_nope, preferred_element_type=jnp.float32)
         + jnp.einsum('bqhr,bkr->bhqk', q_rope, kr, preferred_element_type=jnp.float32))
    return s * (MLA_QK ** -0.5)


def mla_attend_prompt(q_nope, q_rope, c_kv, kr, w_uk, w_uv, kn_g):
    b, L = q_nope.shape[:2]
    k_nope = mla_keys(c_kv, w_uk, kn_g)
    c_f = c_kv.astype(jnp.float32)
    nb = L // Q_BLOCK
    qn = to_chunks(q_nope, Q_BLOCK)
    qr = to_chunks(q_rope, Q_BLOCK)
    kpos = jnp.arange(L)

    def block(args):
        qn_i, qr_i, i = args
        qpos = i * Q_BLOCK + jnp.arange(Q_BLOCK)
        s = mla_scores(qn_i, qr_i, k_nope, kr)
        s = jnp.where((kpos[None, :] <= qpos[:, None])[None, None], s, -jnp.inf)
        p = jax.nn.softmax(s, axis=-1)
        return jnp.einsum('bhqk,bkc->bqhc', p, c_f)

    o_lat = from_chunks(lax.map(block, (qn, qr, jnp.arange(nb))))
    o = jnp.einsum('blhc,chv->blhv', o_lat, w_uv)
    return o.reshape(b, L, MLA_HEADS * MLA_V).astype(q_nope.dtype)


def mla_attend_sample(q_nope, q_rope, c_kv, kr, w_uk, w_uv, kn_g, lat_cache, kr_cache, layer, page_table):
    b, L = q_nope.shape[:2]

    def page_step(carry, pages):
        m, l, acc = carry
        c = lat_cache[layer, pages]
        kr_p = kr_cache[layer, pages]
        s = mla_scores(q_nope, q_rope, mla_keys(c, w_uk, kn_g), kr_p)
        m_new = jnp.maximum(m, jnp.max(s, axis=-1))
        p = jnp.exp(s - m_new[..., None])
        corr = jnp.exp(m - m_new)
        l = l * corr + jnp.sum(p, axis=-1)
        acc = acc * corr[..., None] + jnp.einsum('bhqk,bkc->bhqc', p, c.astype(jnp.float32))
        return (m_new, l, acc), None

    init = (jnp.full((b, MLA_HEADS, L), -jnp.inf, jnp.float32),
            jnp.zeros((b, MLA_HEADS, L), jnp.float32),
            jnp.zeros((b, MLA_HEADS, L, MLA_KV_RANK), jnp.float32))
    (m, l, acc), _ = lax.scan(page_step, init, page_table.T)
    s = mla_scores(q_nope, q_rope, mla_keys(c_kv, w_uk, kn_g), kr)
    s = jnp.where(jnp.tril(jnp.ones((L, L), bool))[None, None], s, -jnp.inf)
    m_new = jnp.maximum(m, jnp.max(s, axis=-1))
    p = jnp.exp(s - m_new[..., None])
    corr = jnp.exp(m - m_new)
    l = l * corr + jnp.sum(p, axis=-1)
    acc = acc * corr[..., None] + jnp.einsum('bhqk,bkc->bhqc', p, c_kv.astype(jnp.float32))
    o = jnp.einsum('bhqc,chv->bqhv', acc / l[..., None], w_uv)
    return o.reshape(b, L, MLA_HEADS * MLA_V).astype(q_nope.dtype)


def ssd_chunk_scan(x, dt, A, Bm, Cm, h0, c):
    mask = jnp.tril(jnp.ones((c, c), bool))

    def body(h, inp):
        xc, dtc, Bc, Cc = inp
        cs = jnp.cumsum(dtc * A, axis=1)
        seg = cs[:, :, None, :] - cs[:, None, :, :]
        decay = jnp.exp(jnp.where(mask[None, :, :, None], seg, -jnp.inf))
        w = jnp.einsum('bthn,bshn->btsh', Cc, Bc) * decay * dtc[:, None, :, :]
        y = (jnp.einsum('btsh,bshp->bthp', w, xc)
             + jnp.einsum('bthn,bhpn->bthp', Cc, h) * jnp.exp(cs)[..., None])
        wend = jnp.exp(cs[:, -1:, :] - cs) * dtc
        h = (h * jnp.exp(cs[:, -1, :])[:, :, None, None]
             + jnp.einsum('bsh,bshp,bshn->bhpn', wend, xc, Bc))
        return h, y

    h, y = lax.scan(body, h0, (to_chunks(x, c), to_chunks(dt, c), to_chunks(Bm, c), to_chunks(Cm, c)))
    return from_chunks(y), h


def ssd_mixer(z, xbc, dt_raw, conv_prev, h0, lp):
    b, L = z.shape[:2]
    xbc_full = jnp.concatenate([conv_prev.astype(xbc.dtype), xbc], axis=1)
    u = jax.nn.silu(causal_dwconv(xbc_full, lp['ssd_conv_w'], lp['ssd_conv_b']).astype(jnp.float32))
    xs = u[..., :SSD_D].reshape(b, L, SSD_HEADS, SSD_HEADDIM)
    rep = SSD_HEADS // SSD_GROUPS
    Bm = jnp.repeat(u[..., SSD_D:SSD_D + SSD_GROUPS * SSD_STATE].reshape(b, L, SSD_GROUPS, SSD_STATE), rep, axis=2)
    Cm = jnp.repeat(u[..., SSD_D + SSD_GROUPS * SSD_STATE:].reshape(b, L, SSD_GROUPS, SSD_STATE), rep, axis=2)
    dt = jax.nn.softplus(dt_raw.astype(jnp.float32) + lp['ssd_dt_bias'].astype(jnp.float32))
    A = -jnp.exp(lp['ssd_a_log'].astype(jnp.float32))
    y, h = ssd_chunk_scan(xs, dt, A, Bm, Cm, h0.astype(jnp.float32), math.gcd(L, SSD_CHUNK))
    y = y + lp['ssd_d'].astype(jnp.float32)[:, None] * xs
    y = y.reshape(b, L, SSD_D) * jax.nn.silu(z.astype(jnp.float32))
    y = rms_norm(y.reshape(b, L, SSD_GROUPS, SSD_D // SSD_GROUPS),
                 lp['ssd_norm'].reshape(SSD_GROUPS, SSD_D // SSD_GROUPS))
    return y.reshape(b, L, SSD_D).astype(z.dtype), xbc_full[:, -(SSD_CONV - 1):], h.astype(h0.dtype)


def gla_chunk_scan(q, k, v, gk, S0, c):
    mask = jnp.tril(jnp.ones((c, c), bool))

    def body(S, inp):
        qc, kc, vc, gc = inp
        bc = jnp.cumsum(gc, axis=1)
        qt = qc * jnp.exp(bc)
        kt = kc * jnp.exp(-bc)
        a = jnp.where(mask[None, None], jnp.einsum('bthk,bshk->bhts', qt, kt), 0.0)
        o = jnp.einsum('bhts,bshv->bthv', a, vc) + jnp.einsum('bthk,bhkv->bthv', qt, S)
        blast = bc[:, -1]
        S = (S * jnp.exp(blast)[..., None]
             + jnp.einsum('bshk,bshv->bhkv', kc * jnp.exp(blast[:, None] - bc), vc))
        return S, o

    S, o = lax.scan(body, S0, (to_chunks(q, c), to_chunks(k, c), to_chunks(v, c), to_chunks(gk, c)))
    return from_chunks(o), S


def gla_mixer(q, k, v, glr, g, S0, lp):
    b, L = q.shape[:2]
    q = q.astype(jnp.float32).reshape(b, L, GLA_HEADS, GLA_DK) * (GLA_DK ** -0.5)
    k = k.astype(jnp.float32).reshape(b, L, GLA_HEADS, GLA_DK)
    v = v.astype(jnp.float32).reshape(b, L, GLA_HEADS, GLA_DV)
    gk = jax.nn.log_sigmoid((glr @ lp['gla_w_gate'] + lp['gla_b_gate']).astype(jnp.float32))
    gk = gk.reshape(b, L, GLA_HEADS, GLA_DK) / GLA_GATE_TAU
    o, S = gla_chunk_scan(q, k, v, gk, S0.astype(jnp.float32), math.gcd(L, GLA_CHUNK))
    o = rms_norm(o, lp['gla_norm'].reshape(GLA_HEADS, GLA_DV)).reshape(b, L, GLA_VD)
    o = o * jax.nn.silu(g.astype(jnp.float32))
    return o.astype(g.dtype), S.astype(S0.dtype)


def mem_kv(mem, lp):
    b, n = mem.shape[:2]
    k = rms_norm((mem @ lp['mem_wk']).reshape(b, n, MEM_HEADS, MEM_HEAD_DIM), lp['mem_k_norm'])
    v = (mem @ lp['mem_wv']).reshape(b, n, MEM_HEADS, MEM_HEAD_DIM)
    return k, v


def mem_attend(xn, k, v, lp):
    b, L = xn.shape[:2]
    q = rms_norm((xn @ lp['mem_wq']).reshape(b, L, MEM_HEADS, MEM_HEAD_DIM), lp['mem_q_norm'])
    s = jnp.einsum('blhd,bmhd->bhlm', q, k, preferred_element_type=jnp.float32) * (MEM_HEAD_DIM ** -0.5)
    p = jax.nn.softmax(s, axis=-1)
    o = jnp.einsum('bhlm,bmhd->blhd', p, v.astype(jnp.float32)).reshape(b, L, MEM_D)
    return o.astype(xn.dtype) @ lp['mem_wo']


def conv_ffn(xn, conv_prev, lp):
    h = xn @ lp['ffn_w_up']
    u, v = h[..., :D_FF], h[..., D_FF:]
    u_full = jnp.concatenate([conv_prev.astype(u.dtype), u], axis=1)
    u_c = causal_dwconv(u_full, lp['ffn_conv_w'], lp['ffn_conv_b'])
    y = (jax.nn.silu(u_c) * v) @ lp['ffn_w_down']
    return y, u_full[:, -(FFN_CONV - 1):]


def trunk_layer(x, pos, mla_attend, mem_k, mem_v, ssd_conv0, ssd_h0, gla_s0, ffn_conv0, lp):
    xn = rms_norm(x, lp['norm_mix'])
    h = xn @ lp['w_in']
    cq, ckv, krr, z, xbc, dtr, gq, gk, gv, glr, gg = split_cols(h, W_IN_SIZES)
    q_nope, q_rope, c_kv, kr = mla_project(cq, ckv, krr, pos, lp)
    o_mla = mla_attend(q_nope, q_rope, c_kv, kr, lp['mla_w_uk'], lp['mla_w_uv'], lp['mla_kn_norm'])
    o_ssd, ssd_conv, ssd_h = ssd_mixer(z, xbc, dtr, ssd_conv0, ssd_h0, lp)
    o_gla, gla_s = gla_mixer(gq, gk, gv, glr, gg, gla_s0, lp)
    x = x + jnp.concatenate([o_mla, o_ssd, o_gla], axis=-1) @ lp['w_o']
    x = x + mem_attend(rms_norm(x, lp['norm_mem']), mem_k, mem_v, lp)
    y_ffn, ffn_conv = conv_ffn(rms_norm(x, lp['norm_ffn']), ffn_conv0, lp)
    x = x + y_ffn
    return x, c_kv, kr, ssd_conv, ssd_h, gla_s, ffn_conv


def setup_inputs(seed: int = 0) -> dict:
    key = jax.random.key(seed)
    ks = iter(jax.random.split(key, 64))
    f32 = jnp.float32

    def nrm(shape, scale=1.0):
        return jax.random.normal(next(ks), shape, f32) * scale

    def gain(dim):
        return 1.0 + 0.05 * jax.random.normal(next(ks), (DEPTH, dim), f32)

    n_pages = PAST_LEN // PAGE_SIZE
    n_used = DEC_BATCH * n_pages
    n_pool = n_used + n_used // 4
    page_table = jax.random.permutation(next(ks), n_pool)[:n_used].reshape(DEC_BATCH, n_pages).astype(jnp.int32)

    dt0 = jnp.exp(jax.random.uniform(next(ks), (DEPTH, SSD_HEADS), f32) * (math.log(0.1) - math.log(0.001)) + math.log(0.001))
    dt_bias = dt0 + jnp.log(-jnp.expm1(-dt0))
    a_log = jnp.log(jax.random.uniform(next(ks), (DEPTH, SSD_HEADS), f32, 1.0, 16.0))

    return {
        'x_prompt': nrm((BATCH, SEQ, D_MODEL)),
        'x_sample': nrm((DEC_BATCH, DEC_SEQ, D_MODEL)),
        'cache_mla_latent': nrm((DEPTH, n_pool, PAGE_SIZE, MLA_KV_RANK)),
        'cache_mla_krope': nrm((DEPTH, n_pool, PAGE_SIZE, MLA_ROPE)),
        'cache_mem_k': nrm((DEPTH, DEC_BATCH, N_MEM, MEM_HEADS, MEM_HEAD_DIM)),
        'cache_mem_v': nrm((DEPTH, DEC_BATCH, N_MEM, MEM_HEADS, MEM_HEAD_DIM)),
        'state_ssd_conv': nrm((DEPTH, DEC_BATCH, SSD_CONV - 1, SSD_CONV_DIM)),
        'state_ssd': nrm((DEPTH, DEC_BATCH, SSD_HEADS, SSD_HEADDIM, SSD_STATE), 0.5),
        'state_gla': nrm((DEPTH, DEC_BATCH, GLA_HEADS, GLA_DK, GLA_DV), 0.5),
        'state_ffn_conv': nrm((DEPTH, DEC_BATCH, FFN_CONV - 1, D_FF)),
        'page_table': page_table,
        'mem_prompt': nrm((BATCH, N_MEM, D_MODEL)),
        'norm_mix': gain(D_MODEL),
        'w_in': nrm((DEPTH, D_MODEL, D_IN), D_MODEL ** -0.5),
        'mla_q_norm': gain(MLA_Q_RANK),
        'mla_w_uq': nrm((DEPTH, MLA_Q_RANK, MLA_HEADS, MLA_QK), MLA_Q_RANK ** -0.5),
        'mla_kv_norm': gain(MLA_KV_RANK),
        'mla_w_uk': nrm((DEPTH, MLA_KV_RANK, MLA_HEADS, MLA_NOPE), MLA_KV_RANK ** -0.5),
        'mla_w_uv': nrm((DEPTH, MLA_KV_RANK, MLA_HEADS, MLA_V), MLA_KV_RANK ** -0.5),
        'mla_qn_norm': gain(MLA_NOPE),
        'mla_qr_norm': gain(MLA_ROPE),
        'mla_kn_norm': gain(MLA_NOPE),
        'mla_kr_norm': gain(MLA_ROPE),
        'ssd_conv_w': nrm((DEPTH, SSD_CONV, SSD_CONV_DIM), SSD_CONV ** -0.5),
        'ssd_conv_b': nrm((DEPTH, SSD_CONV_DIM), 0.01),
        'ssd_dt_bias': dt_bias,
        'ssd_a_log': a_log,
        'ssd_d': 1.0 + 0.1 * nrm((DEPTH, SSD_HEADS)),
        'ssd_norm': gain(SSD_D),
        'gla_w_gate': nrm((DEPTH, GLA_GATE_RANK, GLA_KD), GLA_GATE_RANK ** -0.5),
        'gla_b_gate': nrm((DEPTH, GLA_KD), 0.1),
        'gla_norm': gain(GLA_VD),
        'w_o': nrm((DEPTH, MIX_D, D_MODEL), MIX_D ** -0.5),
        'norm_mem': gain(D_MODEL),
        'mem_wq': nrm((DEPTH, D_MODEL, MEM_D), D_MODEL ** -0.5),
        'mem_wk': nrm((DEPTH, D_MODEL, MEM_D), D_MODEL ** -0.5),
        'mem_wv': nrm((DEPTH, D_MODEL, MEM_D), D_MODEL ** -0.5),
        'mem_wo': nrm((DEPTH, MEM_D, D_MODEL), MEM_D ** -0.5),
        'mem_q_norm': gain(MEM_HEAD_DIM),
        'mem_k_norm': gain(MEM_HEAD_DIM),
        'norm_ffn': gain(D_MODEL),
        'ffn_w_up': nrm((DEPTH, D_MODEL, 2 * D_FF), D_MODEL ** -0.5),
        'ffn_conv_w': nrm((DEPTH, FFN_CONV, D_FF), FFN_CONV ** -0.5),
        'ffn_conv_b': nrm((DEPTH, D_FF), 0.01),
        'ffn_w_down': nrm((DEPTH, D_FF, D_MODEL), D_FF ** -0.5),
    }


def reference(x_prompt, x_sample, cache_mla_latent, cache_mla_krope, cache_mem_k, cache_mem_v,
              state_ssd_conv, state_ssd, state_gla, state_ffn_conv, page_table, mem_prompt,
              norm_mix, w_in, mla_q_norm, mla_w_uq, mla_kv_norm, mla_w_uk, mla_w_uv,
              mla_qn_norm, mla_qr_norm, mla_kn_norm, mla_kr_norm,
              ssd_conv_w, ssd_conv_b, ssd_dt_bias, ssd_a_log, ssd_d, ssd_norm,
              gla_w_gate, gla_b_gate, gla_norm, w_o,
              norm_mem, mem_wq, mem_wk, mem_wv, mem_wo, mem_q_norm, mem_k_norm,
              norm_ffn, ffn_w_up, ffn_conv_w, ffn_conv_b, ffn_w_down):
    bp, Lp = x_prompt.shape[:2]
    Ls = x_sample.shape[1]
    past_len = page_table.shape[1] * PAGE_SIZE
    pos_p = jnp.arange(Lp)
    pos_s = past_len + jnp.arange(Ls)
    dt = x_prompt.dtype
    zero_ssd_conv = jnp.zeros((bp, SSD_CONV - 1, SSD_CONV_DIM), dt)
    zero_ssd = jnp.zeros((bp, SSD_HEADS, SSD_HEADDIM, SSD_STATE), dt)
    zero_gla = jnp.zeros((bp, GLA_HEADS, GLA_DK, GLA_DV), dt)
    zero_ffn_conv = jnp.zeros((bp, FFN_CONV - 1, D_FF), dt)

    xp, xs = x_prompt, x_sample
    p_lat, p_kr, p_mk, p_mv, p_sc, p_ss, p_gla, p_fc = [], [], [], [], [], [], [], []
    s_lat, s_kr, s_sc, s_ss, s_gla, s_fc = [], [], [], [], [], []
    for l in range(DEPTH):
        lp = dict(norm_mix=norm_mix[l], w_in=w_in[l], mla_q_norm=mla_q_norm[l], mla_w_uq=mla_w_uq[l],
                  mla_kv_norm=mla_kv_norm[l], mla_w_uk=mla_w_uk[l], mla_w_uv=mla_w_uv[l],
                  mla_qn_norm=mla_qn_norm[l], mla_qr_norm=mla_qr_norm[l], mla_kn_norm=mla_kn_norm[l],
                  mla_kr_norm=mla_kr_norm[l], ssd_conv_w=ssd_conv_w[l], ssd_conv_b=ssd_conv_b[l],
                  ssd_dt_bias=ssd_dt_bias[l], ssd_a_log=ssd_a_log[l], ssd_d=ssd_d[l], ssd_norm=ssd_norm[l],
                  gla_w_gate=gla_w_gate[l], gla_b_gate=gla_b_gate[l], gla_norm=gla_norm[l], w_o=w_o[l],
                  norm_mem=norm_mem[l], mem_wq=mem_wq[l], mem_wk=mem_wk[l], mem_wv=mem_wv[l],
                  mem_wo=mem_wo[l], mem_q_norm=mem_q_norm[l], mem_k_norm=mem_k_norm[l],
                  norm_ffn=norm_ffn[l], ffn_w_up=ffn_w_up[l], ffn_conv_w=ffn_conv_w[l],
                  ffn_conv_b=ffn_conv_b[l], ffn_w_down=ffn_w_down[l])
        mk, mv = mem_kv(mem_prompt, lp)
        xp, c_kv, kr, sc, ss, sg, fc = trunk_layer(xp, pos_p, mla_attend_prompt, mk, mv,
                                                   zero_ssd_conv, zero_ssd, zero_gla, zero_ffn_conv, lp)
        p_lat.append(c_kv); p_kr.append(kr); p_mk.append(mk); p_mv.append(mv)
        p_sc.append(sc); p_ss.append(ss); p_gla.append(sg); p_fc.append(fc)
        attend_s = functools.partial(mla_attend_sample, lat_cache=cache_mla_latent, kr_cache=cache_mla_krope,
                                     layer=l, page_table=page_table)
        xs, c_kv, kr, sc, ss, sg, fc = trunk_layer(xs, pos_s, attend_s, cache_mem_k[l], cache_mem_v[l],
                                                   state_ssd_conv[l], state_ssd[l], state_gla[l],
                                                   state_ffn_conv[l], lp)
        s_lat.append(c_kv); s_kr.append(kr); s_sc.append(sc); s_ss.append(ss); s_gla.append(sg); s_fc.append(fc)

    return (xp, xs,
            jnp.stack(p_lat), jnp.stack(p_kr), jnp.stack(p_mk), jnp.stack(p_mv),
            jnp.stack(p_sc), jnp.stack(p_ss), jnp.stack(p_gla), jnp.stack(p_fc),
            jnp.stack(s_lat), jnp.stack(s_kr), jnp.stack(s_sc), jnp.stack(s_ss), jnp.stack(s_gla), jnp.stack(s_fc))
```

```python
import functools
import math

import jax
import jax.numpy as jnp
from jax import lax
from jax.experimental import pallas as pl
from jax.experimental.pallas import tpu as pltpu

F32 = jnp.float32
BF16 = jnp.bfloat16
EPS = 1e-6
LANES = 128
SUBLANES = 8
VMEM_LIMIT = 56 * 1024 * 1024

D_MODEL = 1024
MLA_HEADS, MLA_NOPE, MLA_ROPE, MLA_V = 8, 64, 32, 64
MLA_QK = MLA_NOPE + MLA_ROPE
MLA_Q_RANK, MLA_KV_RANK = 256, 128
ROPE_THETA = 10000.0
PAGE_SIZE = 128
SSD_HEADS, SSD_HEADDIM, SSD_GROUPS, SSD_STATE, SSD_CONV = 4, 64, 2, 128, 4
SSD_D = SSD_HEADS * SSD_HEADDIM
SSD_CONV_DIM = SSD_D + 2 * SSD_GROUPS * SSD_STATE
SSD_CHUNK = 128
GLA_HEADS, GLA_DK, GLA_DV = 4, 32, 64
GLA_KD, GLA_VD = GLA_HEADS * GLA_DK, GLA_HEADS * GLA_DV
GLA_GATE_RANK, GLA_GATE_TAU, GLA_CHUNK = 16, 16.0, 64
N_MEM, MEM_HEADS, MEM_HEAD_DIM = 256, 4, 64
MEM_D = MEM_HEADS * MEM_HEAD_DIM
D_FF, FFN_CONV = 2816, 3
FFN_CHUNK = 256

HI = lax.Precision.HIGHEST


def _cparams(*sem):
    return pltpu.CompilerParams(dimension_semantics=sem, vmem_limit_bytes=VMEM_LIMIT)


def _dot(a, b, precision=None):
    return jnp.dot(a, b, preferred_element_type=F32, precision=precision)


def _dot_nt(a, b):
    return lax.dot_general(a, b, (((1,), (1,)), ((), ())), preferred_element_type=F32)


def _dot_tn(a, b):
    return lax.dot_general(a, b, (((0,), (0,)), ((), ())), preferred_element_type=F32)


def _rms(x, g):
    return x * lax.rsqrt(jnp.mean(x * x, axis=-1, keepdims=True) + EPS) * g


def _sigmoid(x):
    return 1.0 / (1.0 + jnp.exp(-x))


def _silu(x):
    return x * _sigmoid(x)


def _softplus(x):
    return jnp.maximum(x, 0.0) + jnp.log1p(jnp.exp(-jnp.abs(x)))


def _log_sigmoid(x):
    return jnp.minimum(x, 0.0) - jnp.log1p(jnp.exp(-jnp.abs(x)))


def _full(shape):
    return pl.BlockSpec(shape, lambda *_: (0,) * len(shape))


def _norm_mm_body(x_ref, g_ref, *refs, n_out, has_norm):
    w_refs, o_refs = refs[:n_out], refs[n_out:]
    x = x_ref[...]
    if has_norm:
        x = _rms(x, g_ref[...])
    xb = x.astype(BF16)
    for w_ref, o_ref in zip(w_refs, o_refs):
        o_ref[...] = _dot(xb, w_ref[...]).astype(o_ref.dtype)


def norm_matmul(x, g, ws, out_dtypes, tm, has_norm=True):
    rows, k = x.shape
    assert rows % tm == 0
    in_specs = [pl.BlockSpec((tm, k), lambda i: (i, 0)), _full((1, k))]
    in_specs += [_full(w.shape) for w in ws]
    out_specs = [pl.BlockSpec((tm, w.shape[1]), lambda i: (i, 0)) for w in ws]
    out_shape = [jax.ShapeDtypeStruct((rows, w.shape[1]), dt) for w, dt in zip(ws, out_dtypes)]
    return pl.pallas_call(
        functools.partial(_norm_mm_body, n_out=len(ws), has_norm=has_norm),
        grid=(rows // tm,), in_specs=in_specs, out_specs=out_specs, out_shape=out_shape,
        compiler_params=_cparams("parallel"), name="norm_matmul",
    )(x, g.reshape(1, k), *ws)


def _mla_prep_body(h_ref, cos_ref, sin_ref, qng_ref, wq_ref, gq_ref, kvg_ref, wk_ref, gk_ref, krg_ref, mh_ref,
                   q_out, k_out, ckv_out, cbf_out, kr_out):
    tm = h_ref.shape[0]
    h = h_ref[...]
    cq, ckv, krr = h[:, 0:256], h[:, 256:384], h[:, 384:512]
    cos, sin = cos_ref[...], sin_ref[...]
    lane = lax.broadcasted_iota(jnp.int32, (tm, LANES), 1)
    mh = mh_ref[...]

    def rope(xv):
        rot = jnp.where(lane < MLA_NOPE + MLA_ROPE // 2,
                        pltpu.roll(xv, LANES - MLA_ROPE // 2, 1), pltpu.roll(xv, MLA_ROPE // 2, 1))
        return xv * cos + rot * sin

    def seg_norm(xv, g):
        msq = _dot((xv * xv).astype(BF16), mh)
        return xv * lax.rsqrt(msq + EPS) * g

    q = _dot(_rms(cq, qng_ref[...]).astype(BF16), wq_ref[...])
    c = _rms(ckv, kvg_ref[...])
    cb = c.astype(BF16)
    ckv_out[...] = c
    cbf_out[...] = cb
    kr = rope(seg_norm(krr, krg_ref[...]))
    kr_out[...] = kr
    k = _dot(cb, wk_ref[...])
    gq, gk = gq_ref[...], gk_ref[...]
    for hd in range(MLA_HEADS):
        sl = slice(hd * LANES, (hd + 1) * LANES)
        q_out[:, sl] = rope(seg_norm(q[:, sl], gq)).astype(BF16)
        k_out[:, sl] = (seg_norm(k[:, sl], gk) + kr).astype(BF16)


def mla_prep(h_mla, cos_t, sin_t, p, tm):
    rows = h_mla.shape[0]
    nt = cos_t.shape[0] // tm
    row_spec = lambda n: pl.BlockSpec((tm, n), lambda i: (i, 0))
    tab_spec = pl.BlockSpec((tm, LANES), lambda i: (i % nt, 0))
    hw = MLA_HEADS * LANES
    return pl.pallas_call(
        _mla_prep_body, grid=(rows // tm,),
        in_specs=[row_spec(512), tab_spec, tab_spec, _full((1, 256)), _full((256, hw)), _full((1, LANES)),
                  _full((1, LANES)), _full((LANES, hw)), _full((1, LANES)), _full((1, LANES)), _full((LANES, LANES))],
        out_specs=[row_spec(hw), row_spec(hw), row_spec(LANES), row_spec(LANES), row_spec(LANES)],
        out_shape=[jax.ShapeDtypeStruct((rows, hw), BF16), jax.ShapeDtypeStruct((rows, hw), BF16),
                   jax.ShapeDtypeStruct((rows, LANES), F32), jax.ShapeDtypeStruct((rows, LANES), BF16),
                   jax.ShapeDtypeStruct((rows, LANES), F32)],
        compiler_params=_cparams("parallel"), name="mla_prep",
    )(h_mla, cos_t, sin_t, p["mla_q_norm"], p["wq_pad"], p["gq_vec"], p["mla_kv_norm"], p["wk_pad"], p["gk_vec"],
      p["gkr_vec"], p["mh"])


def _mla_prompt_body(q_ref, k_ref, c_ref, wuv_ref, o_ref, m_ref, l_ref, acc_ref, olat_ref, *, tq):
    qi = pl.program_id(1)
    row = lax.broadcasted_iota(jnp.int32, (tq, tq), 0)
    col = lax.broadcasted_iota(jnp.int32, (tq, tq), 1)
    causal = col <= row

    for hd in range(MLA_HEADS):
        sl = slice(hd * LANES, (hd + 1) * LANES)
        qh = q_ref[0, :, sl]
        m_ref[...] = jnp.full((tq, LANES), -jnp.inf, F32)
        l_ref[...] = jnp.zeros((tq, LANES), F32)
        acc_ref[...] = jnp.zeros((tq, LANES), F32)

        def step(j, masked):
            start = pl.multiple_of(j * tq, tq)
            s = _dot_nt(qh, k_ref[0, pl.ds(start, tq), sl])
            if masked:
                s = jnp.where(causal, s, -jnp.inf)
            m_old = m_ref[...]
            m_new = jnp.maximum(m_old, jnp.max(s, axis=-1, keepdims=True))
            p = jnp.exp(s - m_new[:, 0:1])
            corr = jnp.exp(m_old - m_new)
            l_ref[...] = l_ref[...] * corr + jnp.sum(p, axis=-1, keepdims=True)
            acc_ref[...] = acc_ref[...] * corr + _dot(p.astype(BF16), c_ref[0, pl.ds(start, tq), :])
            m_ref[...] = m_new

        def body(j, carry):
            step(j, False)
            return carry

        lax.fori_loop(0, qi, body, 0)
        step(qi, True)
        olat_ref[:, sl] = (acc_ref[...] / l_ref[...]).astype(BF16)

    o_ref[0] = _dot(olat_ref[...], wuv_ref[...]).astype(o_ref.dtype)


def mla_prompt(q_cat, k_cat, c_bf, wuv_bd, tq=256):
    b, L, hw = q_cat.shape
    return pl.pallas_call(
        functools.partial(_mla_prompt_body, tq=tq), grid=(b, L // tq),
        in_specs=[pl.BlockSpec((1, tq, hw), lambda bi, qi: (bi, qi, 0)),
                  pl.BlockSpec((1, L, hw), lambda bi, qi: (bi, 0, 0)),
                  pl.BlockSpec((1, L, LANES), lambda bi, qi: (bi, 0, 0)),
                  _full(wuv_bd.shape)],
        out_specs=pl.BlockSpec((1, tq, MLA_HEADS * MLA_V), lambda bi, qi: (bi, qi, 0)),
        out_shape=jax.ShapeDtypeStruct((b, L, MLA_HEADS * MLA_V), BF16),
        scratch_shapes=[pltpu.VMEM((tq, LANES), F32), pltpu.VMEM((tq, LANES), F32), pltpu.VMEM((tq, LANES), F32),
                        pltpu.VMEM((tq, hw), BF16)],
        compiler_params=_cparams("parallel", "arbitrary"), name="mla_prompt",
    )(q_cat, k_cat, c_bf, wuv_bd)


def _res_mm_body(x_ref, *refs, n_in):
    a_refs, w_refs, o_ref = refs[:n_in], refs[n_in:2 * n_in], refs[2 * n_in]
    acc = x_ref[...]
    for a_ref, w_ref in zip(a_refs, w_refs):
        acc = acc + _dot(a_ref[...], w_ref[...])
    o_ref[...] = acc


def residual_matmul(x, acts, ws, tm):
    rows, d = x.shape
    in_specs = [pl.BlockSpec((tm, d), lambda i: (i, 0))]
    in_specs += [pl.BlockSpec((tm, a.shape[1]), lambda i: (i, 0)) for a in acts]
    in_specs += [_full(w.shape) for w in ws]
    return pl.pallas_call(
        functools.partial(_res_mm_body, n_in=len(acts)), grid=(rows // tm,),
        in_specs=in_specs, out_specs=pl.BlockSpec((tm, d), lambda i: (i, 0)),
        out_shape=jax.ShapeDtypeStruct((rows, d), F32),
        compiler_params=_cparams("parallel"), name="out_proj",
    )(x, *acts, *ws)


def _mem_kv_body(x_ref, wk_ref, wv_ref, m_ref, g_ref, k_out, v_out):
    xb = x_ref[...].astype(BF16)
    k = _dot(xb, wk_ref[...])
    msq = _dot((k * k).astype(BF16), m_ref[...])
    k_out[...] = k * lax.rsqrt(msq + EPS) * g_ref[...]
    v_out[...] = _dot(xb, wv_ref[...])


def mem_kv(mem2d, p, tm):
    rows, d = mem2d.shape
    row = lambda n: pl.BlockSpec((tm, n), lambda i: (i, 0))
    return pl.pallas_call(
        _mem_kv_body, grid=(rows // tm,),
        in_specs=[row(d), _full((d, MEM_D)), _full((d, MEM_D)), _full((MEM_D, MEM_D)), _full((1, MEM_D))],
        out_specs=[row(MEM_D), row(MEM_D)],
        out_shape=[jax.ShapeDtypeStruct((rows, MEM_D), F32)] * 2,
        compiler_params=_cparams("parallel"), name="mem_kv",
    )(mem2d, p["mem_wk"], p["mem_wv"], p["m64"], p["mem_k_gain"])


def _mem_attend_body(x_ref, g_ref, wq_ref, m_ref, gq_ref, k_ref, v_ref, wo_ref, o_ref, *, nb):
    r = x_ref.shape[0]
    x = x_ref[...]
    q = _dot(_rms(x, g_ref[...]).astype(BF16), wq_ref[...])
    msq = _dot((q * q).astype(BF16), m_ref[...])
    qn = (q * lax.rsqrt(msq + EPS) * gq_ref[...]).astype(BF16)
    kb = k_ref[...].astype(BF16)
    vb = v_ref[...].astype(BF16)
    lane = lax.broadcasted_iota(jnp.int32, (r, MEM_D), 1) // MEM_HEAD_DIM
    if nb > 1:
        rb = lax.broadcasted_iota(jnp.int32, (r, nb * N_MEM), 0) // (r // nb)
        cb = lax.broadcasted_iota(jnp.int32, (r, nb * N_MEM), 1) // N_MEM
        same = rb == cb
    o = jnp.zeros((r, MEM_D), F32)
    for hd in range(MEM_HEADS):
        s = _dot_nt(jnp.where(lane == hd, qn, jnp.zeros_like(qn)), kb)
        if nb > 1:
            s = jnp.where(same, s, -jnp.inf)
        pr = jnp.exp(s - jnp.max(s, axis=-1, keepdims=True))
        pv = _dot(pr.astype(BF16), vb) / jnp.sum(pr, axis=-1, keepdims=True)
        o = jnp.where(lane == hd, pv, o)
    o_ref[...] = x + _dot(o.astype(BF16), wo_ref[...])


def mem_attend(x, k2d, v2d, p, r, nb, rows_per_batch):
    rows, d = x.shape
    steps_per_kv = (nb * rows_per_batch) // r if nb == 1 else 1
    if nb == 1:
        kv_map = lambda i: (i // (rows_per_batch // r), 0)
    else:
        kv_map = lambda i: (i, 0)
    del steps_per_kv
    kv_spec = pl.BlockSpec((nb * N_MEM, MEM_D), kv_map)
    return pl.pallas_call(
        functools.partial(_mem_attend_body, nb=nb), grid=(rows // r,),
        in_specs=[pl.BlockSpec((r, d), lambda i: (i, 0)), _full((1, d)), _full((d, MEM_D)), _full((MEM_D, MEM_D)),
                  _full((1, MEM_D)), kv_spec, kv_spec, _full((MEM_D, d))],
        out_specs=pl.BlockSpec((r, d), lambda i: (i, 0)),
        out_shape=jax.ShapeDtypeStruct((rows, d), F32),
        compiler_params=_cparams("parallel"), name="mem_attend",
    )(x, p["norm_mem"], p["mem_wq"], p["m64"], p["mem_q_gain"], k2d, v2d, p["mem_wo"])


def _ffn_body(x_ref, g_ref, *refs, carry, tiles_per_seq, seg, tail_rows):
    if carry:
        wu_ref, wv_ref, wd_ref, cw_ref, cb_ref, o_ref, tail_ref, xn_ref, carry_ref = refs
        halo_ref = None
    else:
        halo_ref, wu_ref, wv_ref, wd_ref, cw_ref, cb_ref, o_ref, tail_ref, xn_ref = refs
        carry_ref = None
    tm = x_ref.shape[0]
    x = x_ref[...]
    xn_ref[...] = _rms(x, g_ref[...]).astype(BF16)
    row = lax.broadcasted_iota(jnp.int32, (tm, FFN_CHUNK), 0)
    if carry:
        @pl.when(pl.program_id(0) % tiles_per_seq == 0)
        def _():
            carry_ref[...] = jnp.zeros_like(carry_ref)
    else:
        t = row % seg
    acc = x
    for j in range(D_FF // FFN_CHUNK):
        sl = slice(j * FFN_CHUNK, (j + 1) * FFN_CHUNK)
        xn = xn_ref[...]
        u = _dot(xn, wu_ref[:, sl])
        v = _dot(xn, wv_ref[:, sl])
        um1 = pltpu.roll(u, 1, 0)
        um2 = pltpu.roll(u, 2, 0)
        if carry:
            c6 = carry_ref[6:7, sl]
            c7 = carry_ref[7:8, sl]
            um1 = jnp.where(row == 0, c7, um1)
            um2 = jnp.where(row == 0, c6, jnp.where(row == 1, c7, um2))
            carry_ref[:, sl] = u[tm - SUBLANES:, :]
        else:
            hal = halo_ref[:, sl]
            um1 = jnp.where(t >= 1, um1, pltpu.roll(hal, tm - 1, 0))
            um2 = jnp.where(t >= 2, um2, hal)
        tail_ref[:, sl] = u[tm - tail_rows:, :]
        uc = um2 * cw_ref[0:1, sl] + um1 * cw_ref[1:2, sl] + u * cw_ref[2:3, sl] + cb_ref[:, sl]
        a = (_silu(uc) * v).astype(BF16)
        acc = acc + _dot(a, wd_ref[sl, :])
    o_ref[...] = acc


def ffn(x, halo, p, tm, carry, tiles_per_seq, seg, tail_rows):
    rows, d = x.shape
    n_tiles = rows // tm
    row_spec = lambda n: pl.BlockSpec((tm, n), lambda i: (i, 0))
    in_specs = [row_spec(d), _full((1, d))]
    args = [x, p["norm_ffn"]]
    if not carry:
        in_specs.append(row_spec(D_FF))
        args.append(halo)
    in_specs += [_full((d, D_FF)), _full((d, D_FF)), _full((D_FF, d)), _full((SUBLANES, D_FF)), _full((1, D_FF))]
    args += [p["ffn_wu"], p["ffn_wv"], p["ffn_wd"], p["ffn_conv_w"], p["ffn_conv_b"]]
    scratch = [pltpu.VMEM((tm, d), BF16)]
    if carry:
        scratch.append(pltpu.VMEM((SUBLANES, D_FF), F32))
    return pl.pallas_call(
        functools.partial(_ffn_body, carry=carry, tiles_per_seq=tiles_per_seq, seg=seg, tail_rows=tail_rows),
        grid=(n_tiles,), in_specs=in_specs,
        out_specs=[row_spec(d), pl.BlockSpec((tail_rows, D_FF), lambda i: (i, 0))],
        out_shape=[jax.ShapeDtypeStruct((rows, d), F32), jax.ShapeDtypeStruct((n_tiles * tail_rows, D_FF), F32)],
        scratch_shapes=scratch,
        compiler_params=_cparams("arbitrary"), name="ffn",
    )(*args)


def _ssd_body(*refs, nb, seg, carry, has_state):
    z_ref, xbc_ref, dtr_ref = refs[0:3]
    i = 3
    halo_ref = None
    if not carry:
        halo_ref = refs[i]
        i += 1
    cw_ref, cb_ref, dtb_ref, alog_ref, d_ref, ng_ref = refs[i:i + 6]
    i += 6
    h0_ref = None
    if has_state:
        h0_ref = refs[i]
        i += 1
    o_ref, h_ref = refs[i:i + 2]
    tail_ref = refs[i + 2] if carry else None
    r = z_ref.shape[0]
    hp = SSD_HEADS * SSD_HEADDIM
    gp = hp // SSD_GROUPS

    @pl.when(pl.program_id(1) == 0)
    def _():
        h_ref[...] = h0_ref[...] if has_state else jnp.zeros_like(h_ref)
        if carry:
            tail_ref[...] = jnp.zeros_like(tail_ref)

    x = xbc_ref[...]
    row = lax.broadcasted_iota(jnp.int32, (r, SSD_CONV_DIM), 0)
    row8 = lax.broadcasted_iota(jnp.int32, (SUBLANES, SSD_CONV_DIM), 0)

    def prev(k):
        xs = pltpu.roll(x, k, 0)
        if carry:
            tl = pltpu.roll(tail_ref[...], k, 0)
            top = jnp.where(row8 < k, tl, xs[:SUBLANES])
            return jnp.concatenate([top, xs[SUBLANES:]], axis=0)
        sh = SSD_CONV - 1 - k
        hal = halo_ref[...]
        hs_ = hal if sh == 0 else pltpu.roll(hal, r - sh, 0)
        return jnp.where(row % seg >= k, xs, hs_)

    u = (cb_ref[...] + x * cw_ref[3:4, :] + prev(1) * cw_ref[2:3, :] + prev(2) * cw_ref[1:2, :]
         + prev(3) * cw_ref[0:1, :])
    if carry:
        tail_ref[...] = x[r - SUBLANES:, :]
    u = _silu(u)

    ri = lax.broadcasted_iota(jnp.int32, (r, r), 0)
    ci = lax.broadcasted_iota(jnp.int32, (r, r), 1)
    if nb > 1:
        same = (ri // seg) == (ci // seg)
        tril = jnp.logical_and(same, ci <= ri)
        ones = jnp.where(same, 1.0, 0.0).astype(F32)
    else:
        tril = ci <= ri
        ones = jnp.ones((r, r), F32)
    tri = jnp.where(tril, 1.0, 0.0).astype(F32)

    dt = _softplus(dtr_ref[...] + dtb_ref[...])
    dta = dt * (-jnp.exp(alog_ref[...]))
    cs = _dot(tri, dta, HI)
    tot = _dot(ones, dta, HI)
    ecs = jnp.exp(cs)
    wend = jnp.exp(tot - cs) * dt
    etot = jnp.exp(tot)

    lane = lax.broadcasted_iota(jnp.int32, (r, LANES), 1)
    rb = lax.broadcasted_iota(jnp.int32, (r, LANES), 0) // seg
    lo = lane < SSD_HEADDIM

    def pair(a, b):
        return jnp.where(lo, a, b)

    def hs(a, hd):
        return a[:, hd * LANES:(hd + 1) * LANES]

    zz = z_ref[...]
    hall = h_ref[...].astype(BF16)
    for g in range(SSD_GROUPS):
        h0i, h1i = 2 * g, 2 * g + 1
        gs = slice(g * LANES, (g + 1) * LANES)
        xg = u[:, gs]
        bg = u[:, SSD_D + g * SSD_STATE:SSD_D + (g + 1) * SSD_STATE].astype(BF16)
        cg = u[:, SSD_D + SSD_GROUPS * SSD_STATE + g * SSD_STATE:
               SSD_D + SSD_GROUPS * SSD_STATE + (g + 1) * SSD_STATE].astype(BF16)
        cbm = _dot_nt(cg, bg)
        w0 = (cbm * jnp.where(tril, jnp.exp(hs(cs, h0i) - hs(cs, h0i).T), 0.0)).astype(BF16)
        w1 = (cbm * jnp.where(tril, jnp.exp(hs(cs, h1i) - hs(cs, h1i).T), 0.0)).astype(BF16)
        xdt = (xg * pair(hs(dt, h0i), hs(dt, h1i))).astype(BF16)
        y = pair(_dot(w0, xdt), _dot(w1, xdt))
        zst = _dot_nt(cg, hall)
        if nb == 1:
            yst = zst[:, g * gp:(g + 1) * gp]
        else:
            yst = jnp.zeros((r, gp), F32)
            for b in range(nb):
                yst = jnp.where(rb == b, zst[:, b * hp + g * gp:b * hp + (g + 1) * gp], yst)
        y = y + yst * pair(hs(ecs, h0i), hs(ecs, h1i)) + d_ref[:, gs] * xg
        y = y * _silu(zz[:, gs])
        o_ref[:, gs] = _rms(y, ng_ref[:, gs]).astype(o_ref.dtype)

        xw = (xg * pair(hs(wend, h0i), hs(wend, h1i))).astype(BF16)
        if nb > 1:
            xw = jnp.concatenate([jnp.where(rb == b, xw, jnp.zeros_like(xw)) for b in range(nb)], axis=1)
        dh = _dot_tn(xw, bg)
        for b in range(nb):
            r0 = b * seg if nb > 1 else 0
            dec = jnp.concatenate(
                [jnp.broadcast_to(hs(etot, h0i)[r0:r0 + 1, :], (SSD_HEADDIM, LANES)),
                 jnp.broadcast_to(hs(etot, h1i)[r0:r0 + 1, :], (SSD_HEADDIM, LANES))], axis=0)
            sl = slice(b * hp + g * gp, b * hp + (g + 1) * gp)
            h_ref[sl, :] = h_ref[sl, :] * dec + dh[b * gp:(b + 1) * gp, :]


def ssd(z, xbc, dtr, halo, h0, p, n_batch, nb, seg, n_chunks):
    rows = z.shape[0]
    r = SSD_CHUNK
    carry = halo is None
    assert rows == (n_batch // nb) * n_chunks * r
    hp = SSD_HEADS * SSD_HEADDIM
    row_spec = lambda n: pl.BlockSpec((r, n), lambda b, c: (b * n_chunks + c, 0))
    st_spec = pl.BlockSpec((nb * hp, SSD_STATE), lambda b, c: (b, 0))
    in_specs = [row_spec(SSD_D), row_spec(SSD_CONV_DIM), row_spec(SSD_HEADS * LANES)]
    args = [z, xbc, dtr]
    if not carry:
        in_specs.append(row_spec(SSD_CONV_DIM))
        args.append(halo)
    in_specs += [_full((SUBLANES, SSD_CONV_DIM)), _full((1, SSD_CONV_DIM)), _full((1, SSD_HEADS * LANES)),
                 _full((1, SSD_HEADS * LANES)), _full((1, SSD_D)), _full((1, SSD_D))]
    args += [p["ssd_conv_w"], p["ssd_conv_b"], p["ssd_dt_bias"], p["ssd_a_log"], p["ssd_d"], p["ssd_norm"]]
    if h0 is not None:
        in_specs.append(st_spec)
        args.append(h0)
    scratch = [pltpu.VMEM((SUBLANES, SSD_CONV_DIM), F32)] if carry else []
    return pl.pallas_call(
        functools.partial(_ssd_body, nb=nb, seg=seg, carry=carry, has_state=h0 is not None),
        grid=(n_batch // nb, n_chunks), in_specs=in_specs,
        out_specs=[row_spec(SSD_D), st_spec],
        out_shape=[jax.ShapeDtypeStruct((rows, SSD_D), BF16), jax.ShapeDtypeStruct((n_batch * hp, SSD_STATE), F32)],
        scratch_shapes=scratch,
        compiler_params=_cparams("parallel", "arbitrary"), name="ssd",
    )(*args)


def _gla_body(*refs, nb, seg, c, has_state):
    hg_ref, wg_ref, bg_ref, m_ref, ng_ref = refs[0:5]
    i = 5
    s0_ref = None
    if has_state:
        s0_ref = refs[i]
        i += 1
    o_ref, st_ref = refs[i:i + 2]
    r = hg_ref.shape[0]

    @pl.when(pl.program_id(1) == 0)
    def _():
        st_ref[...] = s0_ref[...] if has_state else jnp.zeros_like(st_ref)

    ri = lax.broadcasted_iota(jnp.int32, (c, c), 0)
    ci = lax.broadcasted_iota(jnp.int32, (c, c), 1)
    if nb > 1:
        same = (ri // seg) == (ci // seg)
        tril = jnp.logical_and(same, ci <= ri)
        ones = jnp.where(same, 1.0, 0.0).astype(F32)
    else:
        tril = ci <= ri
        ones = jnp.ones((c, c), F32)
    tri = jnp.where(tril, 1.0, 0.0).astype(F32)
    klane = lax.broadcasted_iota(jnp.int32, (c, GLA_KD), 1) // GLA_DK
    vlane = lax.broadcasted_iota(jnp.int32, (c, GLA_VD), 1) // GLA_DV
    rbv = lax.broadcasted_iota(jnp.int32, (c, GLA_VD), 0) // seg
    blk = (lax.broadcasted_iota(jnp.int32, (GLA_VD, GLA_KD), 0) // GLA_DV
           == lax.broadcasted_iota(jnp.int32, (GLA_VD, GLA_KD), 1) // GLA_DK)

    for sub in range(r // c):
        rs = slice(sub * c, (sub + 1) * c)
        q = hg_ref[rs, 0:GLA_KD] * (GLA_DK ** -0.5)
        k = hg_ref[rs, GLA_KD:2 * GLA_KD]
        vb = hg_ref[rs, 2 * GLA_KD:2 * GLA_KD + GLA_VD].astype(BF16)
        gg = hg_ref[rs, 2 * GLA_KD + GLA_VD:2 * GLA_KD + 2 * GLA_VD]
        glr = hg_ref[rs, 2 * GLA_KD + 2 * GLA_VD:2 * GLA_KD + 2 * GLA_VD + LANES].astype(BF16)
        gate = _log_sigmoid(_dot(glr, wg_ref[...]) + bg_ref[...]) * (1.0 / GLA_GATE_TAU)
        bc = _dot(tri, gate, HI)
        tot = _dot(ones, gate, HI)
        qt = (q * jnp.exp(bc)).astype(BF16)
        kt = (k * jnp.exp(-bc)).astype(BF16)
        kd = (k * jnp.exp(tot - bc)).astype(BF16)
        o = jnp.zeros((c, GLA_VD), F32)
        for hd in range(GLA_HEADS):
            a = _dot_nt(jnp.where(klane == hd, qt, jnp.zeros_like(qt)), kt)
            a = jnp.where(tril, a, 0.0).astype(BF16)
            o = o + _dot(a, jnp.where(vlane == hd, vb, jnp.zeros_like(vb)))
        zs = _dot_nt(qt, st_ref[...].astype(BF16))
        if nb == 1:
            o = o + zs
        else:
            ost = jnp.zeros((c, GLA_VD), F32)
            for b in range(nb):
                ost = jnp.where(rbv == b, zs[:, b * GLA_VD:(b + 1) * GLA_VD], ost)
            o = o + ost
        msq = _dot((o * o).astype(BF16), m_ref[...])
        on = o * lax.rsqrt(msq + EPS) * ng_ref[...]
        o_ref[rs, :] = (on * _silu(gg)).astype(o_ref.dtype)

        vexp = vb
        if nb > 1:
            vexp = jnp.concatenate([jnp.where(rbv == b, vb, jnp.zeros_like(vb)) for b in range(nb)], axis=1)
        ds = _dot_tn(vexp, kd)
        etot = jnp.exp(tot)
        for b in range(nb):
            r0 = b * seg if nb > 1 else 0
            sl = slice(b * GLA_VD, (b + 1) * GLA_VD)
            st_ref[sl, :] = st_ref[sl, :] * etot[r0:r0 + 1, :] + jnp.where(blk, ds[sl, :], 0.0)


def gla(hg, s0, p, n_batch, nb, seg, c, r, n_steps):
    rows, width = hg.shape
    assert rows == (n_batch // nb) * n_steps * r
    row_spec = lambda n: pl.BlockSpec((r, n), lambda b, s: (b * n_steps + s, 0))
    st_spec = pl.BlockSpec((nb * GLA_VD, GLA_KD), lambda b, s: (b, 0))
    in_specs = [row_spec(width), _full((LANES, GLA_KD)), _full((1, GLA_KD)), _full((GLA_VD, GLA_VD)),
                _full((1, GLA_VD))]
    args = [hg, p["gla_w_gate"], p["gla_b_gate"], p["m64"], p["gla_norm"]]
    if s0 is not None:
        in_specs.append(st_spec)
        args.append(s0)
    return pl.pallas_call(
        functools.partial(_gla_body, nb=nb, seg=seg, c=c, has_state=s0 is not None),
        grid=(n_batch // nb, n_steps), in_specs=in_specs,
        out_specs=[row_spec(GLA_VD), st_spec],
        out_shape=[jax.ShapeDtypeStruct((rows, GLA_VD), BF16), jax.ShapeDtypeStruct((n_batch * GLA_VD, GLA_KD), F32)],
        compiler_params=_cparams("parallel", "arbitrary"), name="gla",
    )(*args)


def _q_absorb_body(q_ref, wt_ref, sel_ref, qa_out, qr_out):
    for hd in range(MLA_HEADS):
        qh = q_ref[:, hd * LANES:(hd + 1) * LANES].astype(F32)
        qa_out[:, hd * LANES:(hd + 1) * LANES] = _dot(qh, wt_ref[hd], HI).astype(BF16)
        qr_out[:, hd * LANES:(hd + 1) * LANES] = _dot(qh, sel_ref[...], HI).astype(BF16)


def q_absorb(q_cat, p, tm):
    rows, hw = q_cat.shape
    row_spec = pl.BlockSpec((tm, hw), lambda i: (i, 0))
    return pl.pallas_call(
        _q_absorb_body, grid=(rows // tm,),
        in_specs=[row_spec, _full((MLA_HEADS, LANES, LANES)), _full((LANES, LANES))],
        out_specs=[row_spec, row_spec],
        out_shape=[jax.ShapeDtypeStruct((rows, hw), BF16)] * 2,
        compiler_params=_cparams("parallel"), name="q_absorb",
    )(q_cat, p["wukt_g"], p["rope_sel"])


PAGED_CP = 4


def _mla_paged_body(pt_ref, ptn_ref, wukt_ref, qa_ref, qr_ref, cnew_ref, krnew_ref, wuv_ref, lat_hbm, kr_hbm,
                    o_ref, lat_buf, kr_buf, sem, m_ref, l_ref, acc_ref, *, layer, n_pages, n_batch, seq):
    bi = pl.program_id(0)
    cp = PAGED_CP
    nc = n_pages // cp
    kc = cp * PAGE_SIZE
    nq = MLA_HEADS * seq
    nk = MLA_HEADS * MLA_NOPE

    def copies(table, chunk, slot):
        out = []
        for pg in range(cp):
            pid = table[0, 0, chunk * cp + pg]
            dst = pl.ds(pg * PAGE_SIZE, PAGE_SIZE)
            out.append(pltpu.make_async_copy(lat_hbm.at[layer, pid], lat_buf.at[slot, dst, :], sem.at[0, slot]))
            out.append(pltpu.make_async_copy(kr_hbm.at[layer, pid], kr_buf.at[slot, dst, :], sem.at[1, slot]))
        return out

    def start(table, chunk, slot):
        for cpy in copies(table, chunk, slot):
            cpy.start()

    def wait(slot):
        for cpy in copies(pt_ref, 0, slot):
            cpy.wait()

    @pl.when(bi == 0)
    def _():
        start(pt_ref, 0, 0)

    m_ref[...] = jnp.full(m_ref.shape, -jnp.inf, F32)
    l_ref[...] = jnp.zeros(l_ref.shape, F32)
    acc_ref[...] = jnp.zeros(acc_ref.shape, F32)
    lhs = jnp.concatenate([wukt_ref[...], qa_ref[0]], axis=0)
    qr = qr_ref[0]

    def attend(lat, krp, mask):
        latb = lat.astype(BF16)
        big = _dot_nt(lhs, latb)
        kt = big[0:nk, :]
        ss = jnp.sum((kt * kt).reshape(MLA_HEADS, MLA_NOPE, kt.shape[1]), axis=1) * (1.0 / MLA_NOPE)
        rinv = lax.rsqrt(ss + EPS)
        rexp = jnp.concatenate([jnp.broadcast_to(rinv[hd:hd + 1, :], (seq, rinv.shape[1]))
                                for hd in range(MLA_HEADS)], axis=0)
        s = big[nk:, :] * rexp + _dot_nt(qr, krp.astype(BF16))
        if mask is not None:
            s = jnp.where(mask, s, -jnp.inf)
        m_old = m_ref[...]
        m_new = jnp.maximum(m_old, jnp.max(s, axis=-1, keepdims=True))
        pr = jnp.exp(s - m_new[:, 0:1])
        corr = jnp.exp(m_old - m_new)
        l_ref[...] = l_ref[...] * corr + jnp.sum(pr, axis=-1, keepdims=True)
        acc_ref[...] = acc_ref[...] * corr + _dot(pr.astype(BF16), latb)
        m_ref[...] = m_new

    def pair_body(jj, carry_):
        for slot in range(2):
            chunk = 2 * jj + slot
            nxt = chunk + 1

            @pl.when(nxt < nc)
            def _():
                start(pt_ref, nxt, 1 - slot)

            @pl.when(jnp.logical_and(nxt == nc, bi + 1 < n_batch))
            def _():
                start(ptn_ref, 0, 1 - slot)

            wait(slot)
            attend(lat_buf[slot], kr_buf[slot], None)
        return carry_

    lax.fori_loop(0, nc // 2, pair_body, 0)

    pad = LANES - seq
    lat_new = jnp.concatenate([cnew_ref[0], jnp.zeros((pad, MLA_KV_RANK), F32)], axis=0)
    kr_new = jnp.concatenate([krnew_ref[0], jnp.zeros((pad, MLA_ROPE), F32)], axis=0)
    qtok = lax.broadcasted_iota(jnp.int32, (nq, LANES), 0) % seq
    key = lax.broadcasted_iota(jnp.int32, (nq, LANES), 1)
    attend(lat_new, kr_new, key <= qtok)

    olat = (acc_ref[...] / l_ref[...]).astype(BF16)
    o = jnp.zeros((seq, MLA_HEADS * MLA_V), F32)
    for hd in range(MLA_HEADS):
        o = o + _dot(olat[hd * seq:(hd + 1) * seq, :], wuv_ref[hd * LANES:(hd + 1) * LANES, :])
    o_ref[0] = o.astype(o_ref.dtype)


def mla_paged(page_table, wukt, qa, qr, c_new, kr_new, wuv_bd, cache_lat, cache_kr, layer):
    n_batch, n_pages = page_table.shape
    seq = c_new.shape[1]
    assert n_pages % (2 * PAGED_CP) == 0
    nq = MLA_HEADS * seq
    pt3 = page_table.reshape(n_batch, 1, n_pages)
    smem_spec = lambda f: pl.BlockSpec((1, 1, n_pages), f, memory_space=pltpu.SMEM)
    per_b = lambda shp: pl.BlockSpec((1,) + shp, lambda b: (b, 0, 0))
    kc = PAGED_CP * PAGE_SIZE
    return pl.pallas_call(
        functools.partial(_mla_paged_body, layer=layer, n_pages=n_pages, n_batch=n_batch, seq=seq),
        grid=(n_batch,),
        in_specs=[smem_spec(lambda b: (b, 0, 0)),
                  smem_spec(lambda b: (jnp.minimum(b + 1, n_batch - 1), 0, 0)),
                  _full(wukt.shape), per_b(qa.shape[1:]), per_b(qr.shape[1:]), per_b(c_new.shape[1:]),
                  per_b(kr_new.shape[1:]),
                  _full(wuv_bd.shape),
                  pl.BlockSpec(memory_space=pl.ANY), pl.BlockSpec(memory_space=pl.ANY)],
        out_specs=per_b((seq, MLA_HEADS * MLA_V)),
        out_shape=jax.ShapeDtypeStruct((n_batch, seq, MLA_HEADS * MLA_V), BF16),
        scratch_shapes=[pltpu.VMEM((2, kc, MLA_KV_RANK), F32), pltpu.VMEM((2, kc, MLA_ROPE), F32),
                        pltpu.SemaphoreType.DMA((2, 2)),
                        pltpu.VMEM((nq, LANES), F32), pltpu.VMEM((nq, LANES), F32), pltpu.VMEM((nq, LANES), F32)],
        compiler_params=_cparams("arbitrary"), name="mla_paged",
    )(pt3, pt3, wukt, qa, qr, c_new, kr_new, wuv_bd, cache_lat, cache_kr)


def _block_diag_mean(n, blocks):
    idx = jnp.arange(n)
    m = jnp.zeros((n, n), F32)
    for start, size in blocks:
        inb = jnp.logical_and(idx >= start, idx < start + size)
        m = m + jnp.where(jnp.logical_and(inb[:, None], inb[None, :]), 1.0 / size, 0.0)
    return m.astype(BF16)


def _rope_tables(pos):
    half = MLA_ROPE // 2
    inv = ROPE_THETA ** (-jnp.arange(half, dtype=F32) / half)
    ang = pos.astype(F32)[:, None] * inv[None, :]
    cos, sin = jnp.cos(ang), jnp.sin(ang)
    n = pos.shape[0]
    one, zero = jnp.ones((n, MLA_NOPE), F32), jnp.zeros((n, MLA_NOPE), F32)
    tail1, tail0 = jnp.ones((n, LANES - MLA_QK), F32), jnp.zeros((n, LANES - MLA_QK), F32)
    return (jnp.concatenate([one, cos, cos, tail1], axis=1), jnp.concatenate([zero, -sin, sin, tail0], axis=1))


def _pack_layer(l, w):
    f = lambda name: w[name][l]
    bf = lambda a: a.astype(BF16)
    p = {}
    w_in = f("w_in")
    o = 0
    cols = {}
    for name, size in (("cq", MLA_Q_RANK), ("ckv", MLA_KV_RANK), ("krr", MLA_ROPE), ("z", SSD_D), ("xbc", SSD_CONV_DIM),
                       ("dt", SSD_HEADS), ("gq", GLA_KD), ("gk", GLA_KD), ("gv", GLA_VD), ("glr", GLA_GATE_RANK),
                       ("gg", GLA_VD)):
        cols[name] = w_in[:, o:o + size]
        o += size
    d = w_in.shape[0]
    krr_pad = jnp.zeros((d, LANES), F32).at[:, MLA_NOPE:MLA_QK].set(cols["krr"])
    glr_pad = jnp.zeros((d, LANES), F32).at[:, :GLA_GATE_RANK].set(cols["glr"])
    p["w_mla"] = bf(jnp.concatenate([cols["cq"], cols["ckv"], krr_pad], axis=1))
    p["w_z"] = bf(cols["z"])
    p["w_xbc"] = bf(cols["xbc"])
    p["w_dt"] = bf(jnp.repeat(cols["dt"], LANES, axis=1))
    p["w_gla"] = bf(jnp.concatenate([cols["gq"], cols["gk"], cols["gv"], cols["gg"], glr_pad], axis=1))
    p["norm_mix"] = f("norm_mix")

    def head_pad(wm, n_real):
        k = wm.shape[0]
        out = jnp.zeros((k, MLA_HEADS, LANES), F32).at[:, :, :n_real].set(wm)
        return out.reshape(k, MLA_HEADS * LANES)

    p["wq_pad"] = bf(head_pad(f("mla_w_uq"), MLA_QK))
    p["wk_pad"] = bf(head_pad(f("mla_w_uk"), MLA_NOPE))
    scale = MLA_QK ** -0.5
    zpad = jnp.zeros((LANES - MLA_QK,), F32)
    p["gq_vec"] = (jnp.concatenate([f("mla_qn_norm"), f("mla_qr_norm"), zpad]) * scale).reshape(1, LANES)
    p["gk_vec"] = jnp.concatenate([f("mla_kn_norm"), jnp.zeros((LANES - MLA_NOPE,), F32)]).reshape(1, LANES)
    p["gkr_vec"] = jnp.concatenate([jnp.zeros((MLA_NOPE,), F32), f("mla_kr_norm"), zpad]).reshape(1, LANES)
    p["mla_q_norm"] = f("mla_q_norm").reshape(1, MLA_Q_RANK)
    p["mla_kv_norm"] = f("mla_kv_norm").reshape(1, MLA_KV_RANK)
    p["mh"] = _block_diag_mean(LANES, [(0, MLA_NOPE), (MLA_NOPE, MLA_ROPE)])
    w_uv = f("mla_w_uv")
    eye = jnp.eye(MLA_HEADS, dtype=F32)
    p["wuv_bd"] = bf(jnp.einsum("chv,hg->hcgv", w_uv, eye).reshape(MLA_HEADS * LANES, MLA_HEADS * MLA_V))
    w_uk = f("mla_w_uk")
    wt = jnp.transpose(w_uk, (1, 2, 0)) * f("mla_kn_norm")[None, :, None]
    p["wukt_g"] = jnp.zeros((MLA_HEADS, LANES, LANES), F32).at[:, :MLA_NOPE, :].set(wt)
    p["wukt"] = bf(jnp.transpose(w_uk, (1, 2, 0)).reshape(MLA_HEADS * MLA_NOPE, MLA_KV_RANK))
    p["rope_sel"] = jnp.zeros((LANES, LANES), F32).at[MLA_NOPE + jnp.arange(MLA_ROPE), jnp.arange(MLA_ROPE)].set(1.0)

    p["ssd_conv_w"] = jnp.zeros((SUBLANES, SSD_CONV_DIM), F32).at[:SSD_CONV].set(f("ssd_conv_w"))
    p["ssd_conv_b"] = f("ssd_conv_b").reshape(1, SSD_CONV_DIM)
    p["ssd_dt_bias"] = jnp.repeat(f("ssd_dt_bias"), LANES).reshape(1, SSD_HEADS * LANES)
    p["ssd_a_log"] = jnp.repeat(f("ssd_a_log"), LANES).reshape(1, SSD_HEADS * LANES)
    p["ssd_d"] = jnp.repeat(f("ssd_d"), SSD_HEADDIM).reshape(1, SSD_D)
    p["ssd_norm"] = f("ssd_norm").reshape(1, SSD_D)
    p["gla_w_gate"] = bf(jnp.zeros((LANES, GLA_KD), F32).at[:GLA_GATE_RANK].set(f("gla_w_gate")))
    p["gla_b_gate"] = f("gla_b_gate").reshape(1, GLA_KD)
    p["gla_norm"] = f("gla_norm").reshape(1, GLA_VD)
    p["m64"] = _block_diag_mean(MEM_D, [(i * 64, 64) for i in range(4)])
    w_o = f("w_o")
    p["wo_mla"] = bf(w_o[:MLA_HEADS * MLA_V])
    p["wo_ssd"] = bf(w_o[MLA_HEADS * MLA_V:MLA_HEADS * MLA_V + SSD_D])
    p["wo_gla"] = bf(w_o[MLA_HEADS * MLA_V + SSD_D:])
    p["norm_mem"] = f("norm_mem").reshape(1, D_MODEL)
    p["mem_wq"], p["mem_wk"], p["mem_wv"], p["mem_wo"] = bf(f("mem_wq")), bf(f("mem_wk")), bf(f("mem_wv")), bf(f("mem_wo"))
    p["mem_q_gain"] = (jnp.tile(f("mem_q_norm"), MEM_HEADS) * MEM_HEAD_DIM ** -0.5).reshape(1, MEM_D)
    p["mem_k_gain"] = jnp.tile(f("mem_k_norm"), MEM_HEADS).reshape(1, MEM_D)
    p["norm_ffn"] = f("norm_ffn").reshape(1, D_MODEL)
    w_up = f("ffn_w_up")
    p["ffn_wu"], p["ffn_wv"], p["ffn_wd"] = bf(w_up[:, :D_FF]), bf(w_up[:, D_FF:]), bf(f("ffn_w_down"))
    p["ffn_conv_w"] = jnp.zeros((SUBLANES, D_FF), F32).at[:FFN_CONV].set(f("ffn_conv_w"))
    p["ffn_conv_b"] = f("ffn_conv_b").reshape(1, D_FF)
    return p


def _pad_rows(state, seq):
    b, k, c = state.shape
    return jnp.pad(state, ((0, 0), (0, seq - k), (0, 0))).reshape(b * seq, c)


def _layer(x3, p, cos_t, sin_t, tm, sample):
    b, L, d = x3.shape
    rows = b * L
    x = x3.reshape(rows, d)
    h_mla, z, xbc, dtr, h_gla = norm_matmul(
        x, p["norm_mix"], [p["w_mla"], p["w_z"], p["w_xbc"], p["w_dt"], p["w_gla"]], [F32] * 5, tm)
    q_cat, k_cat, c_kv, c_bf, kr_vec = mla_prep(h_mla, cos_t, sin_t, p, tm)
    kr = kr_vec[:, MLA_NOPE:MLA_QK]
    hw = MLA_HEADS * LANES
    if sample is None:
        o_mla = mla_prompt(q_cat.reshape(b, L, hw), k_cat.reshape(b, L, hw), c_bf.reshape(b, L, LANES), p["wuv_bd"])
        o_mla = o_mla.reshape(rows, MLA_HEADS * MLA_V)
        o_ssd, ssd_h = ssd(z, xbc, dtr, None, None, p, b, 1, L, L // SSD_CHUNK)
        gr = 256
        o_gla, gla_st = gla(h_gla, None, p, b, 1, L, GLA_CHUNK, gr, L // gr)
    else:
        qa, qrp = q_absorb(q_cat, p, tm)
        to_hq = lambda a, n: a.reshape(b, L, MLA_HEADS, LANES)[..., :n].transpose(0, 2, 1, 3).reshape(b, MLA_HEADS * L, n)
        o_mla = mla_paged(sample["page_table"], p["wukt"], to_hq(qa, LANES), to_hq(qrp, MLA_ROPE),
                          c_kv.reshape(b, L, LANES),
                          kr.reshape(b, L, MLA_ROPE), p["wuv_bd"], sample["cache_lat"], sample["cache_kr"],
                          sample["layer"])
        o_mla = o_mla.reshape(rows, MLA_HEADS * MLA_V)
        nb = SSD_CHUNK // L
        o_ssd, ssd_h = ssd(z, xbc, dtr, _pad_rows(sample["ssd_conv"], L),
                           sample["ssd"].reshape(b * SSD_D, SSD_STATE), p, b, nb, L, 1)
        eye = jnp.eye(GLA_HEADS, dtype=F32)
        st0 = jnp.einsum("bhkv,hg->bhvgk", sample["gla"], eye).reshape(b * GLA_VD, GLA_KD)
        o_gla, gla_st = gla(h_gla, st0, p, b, nb, L, nb * L, nb * L, 1)
    x = residual_matmul(x, [o_mla, o_ssd, o_gla], [p["wo_mla"], p["wo_ssd"], p["wo_gla"]], tm)

    if sample is None:
        mk, mv = mem_kv(p["mem_prompt"], p, 512)
        x = mem_attend(x, mk, mv, p, 512, 1, L)
    else:
        mk = sample["mem_k"].reshape(b * N_MEM, MEM_D)
        mv = sample["mem_v"].reshape(b * N_MEM, MEM_D)
        nbm = 128 // L
        x = mem_attend(x, mk, mv, p, nbm * L, nbm, L)

    if sample is None:
        ft = 512
        x, u_tail = ffn(x, None, p, ft, True, L // ft, L, SUBLANES)
        ffn_conv = u_tail.reshape(b, L // ft, SUBLANES, D_FF)[:, -1, SUBLANES - (FFN_CONV - 1):, :]
    else:
        ft = min(rows, 512)
        x, u_full = ffn(x, _pad_rows(sample["ffn_conv"], L), p, ft, False, 1, L, ft)
        ffn_conv = u_full.reshape(b, L, D_FF)[:, L - (FFN_CONV - 1):, :]

    xbc3 = xbc.reshape(b, L, SSD_CONV_DIM)
    out = dict(
        x=x.reshape(b, L, d), lat=c_kv.reshape(b, L, MLA_KV_RANK), kr=kr.reshape(b, L, MLA_ROPE),
        ssd_conv=xbc3[:, L - (SSD_CONV - 1):, :], ssd=ssd_h.reshape(b, SSD_HEADS, SSD_HEADDIM, SSD_STATE),
        gla=jnp.einsum("bhvhk->bhkv", gla_st.reshape(b, GLA_HEADS, GLA_DV, GLA_HEADS, GLA_DK)),
        ffn_conv=ffn_conv)
    if sample is None:
        out["mem_k"] = mk.reshape(b, N_MEM, MEM_HEADS, MEM_HEAD_DIM)
        out["mem_v"] = mv.reshape(b, N_MEM, MEM_HEADS, MEM_HEAD_DIM)
    return out


def kernel(x_prompt, x_sample, cache_mla_latent, cache_mla_krope, cache_mem_k, cache_mem_v, state_ssd_conv, state_ssd, state_gla, state_ffn_conv, page_table, mem_prompt, norm_mix, w_in, mla_q_norm, mla_w_uq, mla_kv_norm, mla_w_uk, mla_w_uv, mla_qn_norm, mla_qr_norm, mla_kn_norm, mla_kr_norm, ssd_conv_w, ssd_conv_b, ssd_dt_bias, ssd_a_log, ssd_d, ssd_norm, gla_w_gate, gla_b_gate, gla_norm, w_o, norm_mem, mem_wq, mem_wk, mem_wv, mem_wo, mem_q_norm, mem_k_norm, norm_ffn, ffn_w_up, ffn_conv_w, ffn_conv_b, ffn_w_down):
    weights = dict(
        norm_mix=norm_mix, w_in=w_in, mla_q_norm=mla_q_norm, mla_w_uq=mla_w_uq, mla_kv_norm=mla_kv_norm,
        mla_w_uk=mla_w_uk, mla_w_uv=mla_w_uv, mla_qn_norm=mla_qn_norm, mla_qr_norm=mla_qr_norm,
        mla_kn_norm=mla_kn_norm, mla_kr_norm=mla_kr_norm, ssd_conv_w=ssd_conv_w, ssd_conv_b=ssd_conv_b,
        ssd_dt_bias=ssd_dt_bias, ssd_a_log=ssd_a_log, ssd_d=ssd_d, ssd_norm=ssd_norm, gla_w_gate=gla_w_gate,
        gla_b_gate=gla_b_gate, gla_norm=gla_norm, w_o=w_o, norm_mem=norm_mem, mem_wq=mem_wq, mem_wk=mem_wk,
        mem_wv=mem_wv, mem_wo=mem_wo, mem_q_norm=mem_q_norm, mem_k_norm=mem_k_norm, norm_ffn=norm_ffn,
        ffn_w_up=ffn_w_up, ffn_conv_w=ffn_conv_w, ffn_conv_b=ffn_conv_b, ffn_w_down=ffn_w_down)
    depth = w_in.shape[0]
    bp, lp, _ = x_prompt.shape
    bs, ls, _ = x_sample.shape
    assert ls >= SSD_CONV - 1 and SSD_CHUNK % ls == 0
    past_len = page_table.shape[1] * PAGE_SIZE
    tm_p, tm_s = 512, min(512, bs * ls)
    cos_p, sin_p = _rope_tables(jnp.arange(lp))
    cos_s, sin_s = _rope_tables(past_len + jnp.arange(ls))
    cos_s, sin_s = jnp.tile(cos_s, (tm_s // ls, 1)), jnp.tile(sin_s, (tm_s // ls, 1))
    mem2d = mem_prompt.reshape(bp * N_MEM, D_MODEL)

    xp, xs = x_prompt, x_sample
    outs_p, outs_s = [], []
    for l in range(depth):
        p = _pack_layer(l, weights)
        p["mem_prompt"] = mem2d
        op = _layer(xp, p, cos_p, sin_p, tm_p, None)
        xp = op["x"]
        outs_p.append(op)
        sample = dict(layer=l, page_table=page_table, cache_lat=cache_mla_latent, cache_kr=cache_mla_krope,
                      mem_k=cache_mem_k[l], mem_v=cache_mem_v[l], ssd_conv=state_ssd_conv[l], ssd=state_ssd[l],
                      gla=state_gla[l], ffn_conv=state_ffn_conv[l])
        os_ = _layer(xs, p, cos_s, sin_s, tm_s, sample)
        xs = os_["x"]
        outs_s.append(os_)

    st = lambda outs, key: jnp.stack([o[key] for o in outs])
    return (xp, xs,
            st(outs_p, "lat"), st(outs_p, "kr"), st(outs_p, "mem_k"), st(outs_p, "mem_v"),
            st(outs_p, "ssd_conv"), st(outs_p, "ssd"), st(outs_p, "gla"), st(outs_p, "ffn_conv"),
            st(outs_s, "lat"), st(outs_s, "kr"), st(outs_s, "ssd_conv"), st(outs_s, "ssd"), st(outs_s, "gla"),
            st(outs_s, "ffn_conv"))
```

```python
import functools
import math

import jax
import jax.numpy as jnp
from jax import lax
from jax.experimental import pallas as pl
from jax.experimental.pallas import tpu as pltpu

F32 = jnp.float32
BF16 = jnp.bfloat16
EPS = 1e-6
LANES = 128
SUBLANES = 8
VMEM_LIMIT = 56 * 1024 * 1024

D_MODEL = 1024
MLA_HEADS, MLA_NOPE, MLA_ROPE, MLA_V = 8, 64, 32, 64
MLA_QK = MLA_NOPE + MLA_ROPE
MLA_Q_RANK, MLA_KV_RANK = 256, 128
ROPE_THETA = 10000.0
PAGE_SIZE = 128
SSD_HEADS, SSD_HEADDIM, SSD_GROUPS, SSD_STATE, SSD_CONV = 4, 64, 2, 128, 4
SSD_D = SSD_HEADS * SSD_HEADDIM
SSD_CONV_DIM = SSD_D + 2 * SSD_GROUPS * SSD_STATE
SSD_CHUNK = 128
GLA_HEADS, GLA_DK, GLA_DV = 4, 32, 64
GLA_KD, GLA_VD = GLA_HEADS * GLA_DK, GLA_HEADS * GLA_DV
GLA_GATE_RANK, GLA_GATE_TAU, GLA_CHUNK = 16, 16.0, 64
N_MEM, MEM_HEADS, MEM_HEAD_DIM = 256, 4, 64
MEM_D = MEM_HEADS * MEM_HEAD_DIM
D_FF, FFN_CONV = 2816, 3
FFN_CHUNK = 256

HI = lax.Precision.HIGHEST


def _cparams(*sem):
    return pltpu.CompilerParams(dimension_semantics=sem, vmem_limit_bytes=VMEM_LIMIT)


def _dot(a, b, precision=None):
    return jnp.dot(a, b, preferred_element_type=F32, precision=precision)


def _dot_nt(a, b):
    return lax.dot_general(a, b, (((1,), (1,)), ((), ())), preferred_element_type=F32)


def _dot_tn(a, b):
    return lax.dot_general(a, b, (((0,), (0,)), ((), ())), preferred_element_type=F32)


def _rms(x, g):
    return x * lax.rsqrt(jnp.mean(x * x, axis=-1, keepdims=True) + EPS) * g


def _sigmoid(x):
    return 1.0 / (1.0 + jnp.exp(-x))


def _silu(x):
    return x * _sigmoid(x)


def _softplus(x):
    return jnp.maximum(x, 0.0) + jnp.log1p(jnp.exp(-jnp.abs(x)))


def _log_sigmoid(x):
    return jnp.minimum(x, 0.0) - jnp.log1p(jnp.exp(-jnp.abs(x)))


def _full(shape):
    return pl.BlockSpec(shape, lambda *_: (0,) * len(shape))


def _norm_mm_body(x_ref, g_ref, *refs, n_out, has_norm):
    w_refs, o_refs = refs[:n_out], refs[n_out:]
    x = x_ref[...]
    if has_norm:
        x = _rms(x, g_ref[...])
    xb = x.astype(BF16)
    for w_ref, o_ref in zip(w_refs, o_refs):
        o_ref[...] = _dot(xb, w_ref[...]).astype(o_ref.dtype)


def norm_matmul(x, g, ws, out_dtypes, tm, has_norm=True):
    rows, k = x.shape
    assert rows % tm == 0
    in_specs = [pl.BlockSpec((tm, k), lambda i: (i, 0)), _full((1, k))]
    in_specs += [_full(w.shape) for w in ws]
    out_specs = [pl.BlockSpec((tm, w.shape[1]), lambda i: (i, 0)) for w in ws]
    out_shape = [jax.ShapeDtypeStruct((rows, w.shape[1]), dt) for w, dt in zip(ws, out_dtypes)]
    return pl.pallas_call(
        functools.partial(_norm_mm_body, n_out=len(ws), has_norm=has_norm),
        grid=(rows // tm,), in_specs=in_specs, out_specs=out_specs, out_shape=out_shape,
        compiler_params=_cparams("parallel"), name="norm_matmul",
    )(x, g.reshape(1, k), *ws)


def _mla_prep_body(h_ref, cos_ref, sin_ref, qng_ref, wq_ref, gq_ref, kvg_ref, wk_ref, gk_ref, krg_ref, mh_ref,
                   q_out, k_out, ckv_out, cbf_out, kr_out):
    tm = h_ref.shape[0]
    h = h_ref[...]
    cq, ckv, krr = h[:, 0:256], h[:, 256:384], h[:, 384:512]
    cos, sin = cos_ref[...], sin_ref[...]
    lane = lax.broadcasted_iota(jnp.int32, (tm, LANES), 1)
    mh = mh_ref[...]

    def rope(xv):
        rot = jnp.where(lane < MLA_NOPE + MLA_ROPE // 2,
                        pltpu.roll(xv, LANES - MLA_ROPE // 2, 1), pltpu.roll(xv, MLA_ROPE // 2, 1))
        return xv * cos + rot * sin

    def seg_norm(xv, g):
        msq = _dot((xv * xv).astype(BF16), mh)
        return xv * lax.rsqrt(msq + EPS) * g

    q = _dot(_rms(cq, qng_ref[...]).astype(BF16), wq_ref[...])
    c = _rms(ckv, kvg_ref[...])
    cb = c.astype(BF16)
    ckv_out[...] = c
    cbf_out[...] = jnp.concatenate([cb, jnp.ones_like(cb)], axis=1)
    kr = rope(seg_norm(krr, krg_ref[...]))
    kr_out[...] = kr
    k = _dot(cb, wk_ref[...])
    gq, gk = gq_ref[...], gk_ref[...]
    for hd in range(MLA_HEADS):
        sl = slice(hd * LANES, (hd + 1) * LANES)
        q_out[:, sl] = rope(seg_norm(q[:, sl], gq)).astype(BF16)
        k_out[:, sl] = (seg_norm(k[:, sl], gk) + kr).astype(BF16)


def mla_prep(h_mla, cos_t, sin_t, p, tm):
    rows = h_mla.shape[0]
    nt = cos_t.shape[0] // tm
    row_spec = lambda n: pl.BlockSpec((tm, n), lambda i: (i, 0))
    tab_spec = pl.BlockSpec((tm, LANES), lambda i: (i % nt, 0))
    hw = MLA_HEADS * LANES
    return pl.pallas_call(
        _mla_prep_body, grid=(rows // tm,),
        in_specs=[row_spec(512), tab_spec, tab_spec, _full((1, 256)), _full((256, hw)), _full((1, LANES)),
                  _full((1, LANES)), _full((LANES, hw)), _full((1, LANES)), _full((1, LANES)), _full((LANES, LANES))],
        out_specs=[row_spec(hw), row_spec(hw), row_spec(LANES), row_spec(2 * LANES), row_spec(LANES)],
        out_shape=[jax.ShapeDtypeStruct((rows, hw), BF16), jax.ShapeDtypeStruct((rows, hw), BF16),
                   jax.ShapeDtypeStruct((rows, LANES), F32), jax.ShapeDtypeStruct((rows, 2 * LANES), BF16),
                   jax.ShapeDtypeStruct((rows, LANES), F32)],
        compiler_params=_cparams("parallel"), name="mla_prep",
    )(h_mla, cos_t, sin_t, p["mla_q_norm"], p["wq_pad"], p["gq_vec"], p["mla_kv_norm"], p["wk_pad"], p["gk_vec"],
      p["gkr_vec"], p["mh"])


def _mla_prompt_body(q_ref, k_ref, c_ref, wuv_ref, o_ref, m_ref, l_ref, acc_ref, olat_ref, *, tq):
    qi = pl.program_id(1)
    row = lax.broadcasted_iota(jnp.int32, (tq, tq), 0)
    col = lax.broadcasted_iota(jnp.int32, (tq, tq), 1)
    causal = col <= row
    m_ref[...] = jnp.full(m_ref.shape, -jnp.inf, F32)
    l_ref[...] = jnp.zeros(l_ref.shape, F32)
    acc_ref[...] = jnp.zeros(acc_ref.shape, F32)

    def step(j, masked):
        start = pl.multiple_of(j * tq, tq)
        cblk = c_ref[0, pl.ds(start, tq), :]
        for hd in range(MLA_HEADS):
            sl = slice(hd * LANES, (hd + 1) * LANES)
            s = _dot_nt(q_ref[0, :, sl], k_ref[0, pl.ds(start, tq), sl])
            if masked:
                s = jnp.where(causal, s, -jnp.inf)
            m_old = m_ref[hd]
            m_new = jnp.maximum(m_old, jnp.max(s, axis=-1, keepdims=True))
            p = jnp.exp(s - jnp.concatenate([m_new] * (tq // LANES), axis=1))
            corr = jnp.exp(m_old - m_new)
            pv = _dot(p.astype(BF16), cblk)
            acc_ref[hd] = acc_ref[hd] * corr + pv[:, :LANES]
            l_ref[hd] = l_ref[hd] * corr + pv[:, LANES:]
            m_ref[hd] = m_new

    def body(j, carry):
        step(j, False)
        return carry

    lax.fori_loop(0, qi, body, 0)
    step(qi, True)
    for hd in range(MLA_HEADS):
        olat_ref[:, hd * LANES:(hd + 1) * LANES] = (acc_ref[hd] / l_ref[hd]).astype(BF16)
    o_ref[0] = _dot(olat_ref[...], wuv_ref[...]).astype(o_ref.dtype)


def mla_prompt(q_cat, k_cat, c_bf, wuv_bd, tq=256):
    b, L, hw = q_cat.shape
    return pl.pallas_call(
        functools.partial(_mla_prompt_body, tq=tq), grid=(b, L // tq),
        in_specs=[pl.BlockSpec((1, tq, hw), lambda bi, qi: (bi, qi, 0)),
                  pl.BlockSpec((1, L, hw), lambda bi, qi: (bi, 0, 0)),
                  pl.BlockSpec((1, L, 2 * LANES), lambda bi, qi: (bi, 0, 0)),
                  _full(wuv_bd.shape)],
        out_specs=pl.BlockSpec((1, tq, MLA_HEADS * MLA_V), lambda bi, qi: (bi, qi, 0)),
        out_shape=jax.ShapeDtypeStruct((b, L, MLA_HEADS * MLA_V), BF16),
        scratch_shapes=[pltpu.VMEM((MLA_HEADS, tq, LANES), F32), pltpu.VMEM((MLA_HEADS, tq, LANES), F32),
                        pltpu.VMEM((MLA_HEADS, tq, LANES), F32), pltpu.VMEM((tq, hw), BF16)],
        compiler_params=_cparams("parallel", "arbitrary"), name="mla_prompt",
    )(q_cat, k_cat, c_bf, wuv_bd)


def _res_mm_body(x_ref, *refs, n_in):
    a_refs, w_refs, o_ref = refs[:n_in], refs[n_in:2 * n_in], refs[2 * n_in]
    acc = x_ref[...]
    for a_ref, w_ref in zip(a_refs, w_refs):
        acc = acc + _dot(a_ref[...], w_ref[...])
    o_ref[...] = acc


def residual_matmul(x, acts, ws, tm):
    rows, d = x.shape
    in_specs = [pl.BlockSpec((tm, d), lambda i: (i, 0))]
    in_specs += [pl.BlockSpec((tm, a.shape[1]), lambda i: (i, 0)) for a in acts]
    in_specs += [_full(w.shape) for w in ws]
    return pl.pallas_call(
        functools.partial(_res_mm_body, n_in=len(acts)), grid=(rows // tm,),
        in_specs=in_specs, out_specs=pl.BlockSpec((tm, d), lambda i: (i, 0)),
        out_shape=jax.ShapeDtypeStruct((rows, d), F32),
        compiler_params=_cparams("parallel"), name="out_proj",
    )(x, *acts, *ws)


def _mem_kv_body(x_ref, wk_ref, wv_ref, m_ref, g_ref, k_out, v_out):
    xb = x_ref[...].astype(BF16)
    k = _dot(xb, wk_ref[...])
    msq = _dot((k * k).astype(BF16), m_ref[...])
    k_out[...] = k * lax.rsqrt(msq + EPS) * g_ref[...]
    v_out[...] = _dot(xb, wv_ref[...])


def mem_kv(mem2d, p, tm):
    rows, d = mem2d.shape
    row = lambda n: pl.BlockSpec((tm, n), lambda i: (i, 0))
    return pl.pallas_call(
        _mem_kv_body, grid=(rows // tm,),
        in_specs=[row(d), _full((d, MEM_D)), _full((d, MEM_D)), _full((MEM_D, MEM_D)), _full((1, MEM_D))],
        out_specs=[row(MEM_D), row(MEM_D)],
        out_shape=[jax.ShapeDtypeStruct((rows, MEM_D), F32)] * 2,
        compiler_params=_cparams("parallel"), name="mem_kv",
    )(mem2d, p["mem_wk"], p["mem_wv"], p["m64"], p["mem_k_gain"])


def _mem_attend_body(x_ref, g_ref, wq_ref, m_ref, gq_ref, k_ref, v_ref, wo_ref, o_ref, *, nb, kv_t):
    r = x_ref.shape[0]
    x = x_ref[...]
    q = _dot(_rms(x, g_ref[...]).astype(BF16), wq_ref[...])
    msq = _dot((q * q).astype(BF16), m_ref[...])
    qn = (q * lax.rsqrt(msq + EPS) * gq_ref[...]).astype(BF16)
    if kv_t:
        kb = jnp.concatenate([k_ref[b] for b in range(nb)], axis=1).astype(BF16)
        vb = jnp.concatenate([v_ref[b] for b in range(nb)], axis=1).astype(BF16)
    else:
        kb = k_ref[...].astype(BF16)
        vb = v_ref[...].astype(BF16)
    lane = lax.broadcasted_iota(jnp.int32, (r, MEM_D), 1) // MEM_HEAD_DIM
    if nb > 1:
        rb = lax.broadcasted_iota(jnp.int32, (r, nb * N_MEM), 0) // (r // nb)
        cb = lax.broadcasted_iota(jnp.int32, (r, nb * N_MEM), 1) // N_MEM
        same = rb == cb
    o = jnp.zeros((r, MEM_D), F32)
    for hd in range(MEM_HEADS):
        qm = jnp.where(lane == hd, qn, jnp.zeros_like(qn))
        s = _dot(qm, kb) if kv_t else _dot_nt(qm, kb)
        if nb > 1:
            s = jnp.where(same, s, -jnp.inf)
        pr = jnp.exp(s - jnp.max(s, axis=-1, keepdims=True))
        prb = pr.astype(BF16)
        pv = (_dot_nt(prb, vb) if kv_t else _dot(prb, vb)) / jnp.sum(pr, axis=-1, keepdims=True)
        o = jnp.where(lane == hd, pv, o)
    o_ref[...] = x + _dot(o.astype(BF16), wo_ref[...])


def mem_attend(x, k, v, p, r, nb, rows_per_batch, kv_t, layer=0):
    rows, d = x.shape
    if kv_t:
        kv_spec = pl.BlockSpec((None, nb, MEM_D, N_MEM), lambda i: (layer, i, 0, 0))
    elif nb == 1:
        kv_spec = pl.BlockSpec((N_MEM, MEM_D), lambda i: (i // (rows_per_batch // r), 0))
    else:
        kv_spec = pl.BlockSpec((nb * N_MEM, MEM_D), lambda i: (i, 0))
    return pl.pallas_call(
        functools.partial(_mem_attend_body, nb=nb, kv_t=kv_t), grid=(rows // r,),
        in_specs=[pl.BlockSpec((r, d), lambda i: (i, 0)), _full((1, d)), _full((d, MEM_D)), _full((MEM_D, MEM_D)),
                  _full((1, MEM_D)), kv_spec, kv_spec, _full((MEM_D, d))],
        out_specs=pl.BlockSpec((r, d), lambda i: (i, 0)),
        out_shape=jax.ShapeDtypeStruct((rows, d), F32),
        compiler_params=_cparams("parallel"), name="mem_attend",
    )(x, p["norm_mem"], p["mem_wq"], p["m64"], p["mem_q_gain"], k, v, p["mem_wo"])


def _ffn_body(x_ref, g_ref, *refs, carry, tiles_per_seq, seg, tail_rows):
    if carry:
        wu_ref, wv_ref, wd_ref, cw_ref, cb_ref, o_ref, tail_ref, xn_ref, carry_ref = refs
        halo_ref = None
    else:
        halo_ref, wu_ref, wv_ref, wd_ref, cw_ref, cb_ref, o_ref, tail_ref, xn_ref = refs
        carry_ref = None
    tm = x_ref.shape[0]
    x = x_ref[...]
    xn_ref[...] = _rms(x, g_ref[...]).astype(BF16)
    row = lax.broadcasted_iota(jnp.int32, (tm, FFN_CHUNK), 0)
    if carry:
        @pl.when(pl.program_id(0) % tiles_per_seq == 0)
        def _():
            carry_ref[...] = jnp.zeros_like(carry_ref)
    else:
        t = row % seg
    acc = x
    for j in range(D_FF // FFN_CHUNK):
        sl = slice(j * FFN_CHUNK, (j + 1) * FFN_CHUNK)
        xn = xn_ref[...]
        u = _dot(xn, wu_ref[:, sl])
        v = _dot(xn, wv_ref[:, sl])
        um1 = pltpu.roll(u, 1, 0)
        um2 = pltpu.roll(u, 2, 0)
        if carry:
            c6 = carry_ref[6:7, sl]
            c7 = carry_ref[7:8, sl]
            um1 = jnp.where(row == 0, c7, um1)
            um2 = jnp.where(row == 0, c6, jnp.where(row == 1, c7, um2))
            carry_ref[:, sl] = u[tm - SUBLANES:, :]
        else:
            hal = halo_ref[:, sl]
            um1 = jnp.where(t >= 1, um1, pltpu.roll(hal, tm - 1, 0))
            um2 = jnp.where(t >= 2, um2, hal)
        tail_ref[:, sl] = u[tm - tail_rows:, :]
        uc = um2 * cw_ref[0:1, sl] + um1 * cw_ref[1:2, sl] + u * cw_ref[2:3, sl] + cb_ref[:, sl]
        a = (_silu(uc) * v).astype(BF16)
        acc = acc + _dot(a, wd_ref[sl, :])
    o_ref[...] = acc


def ffn(x, halo, p, tm, carry, tiles_per_seq, seg, tail_rows):
    rows, d = x.shape
    n_tiles = rows // tm
    row_spec = lambda n: pl.BlockSpec((tm, n), lambda i: (i, 0))
    in_specs = [row_spec(d), _full((1, d))]
    args = [x, p["norm_ffn"]]
    if not carry:
        in_specs.append(row_spec(D_FF))
        args.append(halo)
    in_specs += [_full((d, D_FF)), _full((d, D_FF)), _full((D_FF, d)), _full((SUBLANES, D_FF)), _full((1, D_FF))]
    args += [p["ffn_wu"], p["ffn_wv"], p["ffn_wd"], p["ffn_conv_w"], p["ffn_conv_b"]]
    scratch = [pltpu.VMEM((tm, d), BF16)]
    if carry:
        scratch.append(pltpu.VMEM((SUBLANES, D_FF), F32))
    return pl.pallas_call(
        functools.partial(_ffn_body, carry=carry, tiles_per_seq=tiles_per_seq, seg=seg, tail_rows=tail_rows),
        grid=(n_tiles,), in_specs=in_specs,
        out_specs=[row_spec(d), pl.BlockSpec((tail_rows, D_FF), lambda i: (i, 0))],
        out_shape=[jax.ShapeDtypeStruct((rows, d), F32), jax.ShapeDtypeStruct((n_tiles * tail_rows, D_FF), F32)],
        scratch_shapes=scratch,
        compiler_params=_cparams("arbitrary"), name="ffn",
    )(*args)


def _ssd_body(*refs, ns, nb, seg, carry, has_state):
    z_ref, xbc_ref, dtr_ref = refs[0:3]
    i = 3
    halo_ref = None
    if not carry:
        halo_ref = refs[i]
        i += 1
    consts = refs[i:i + 6]
    i += 6
    h0_ref = None
    if has_state:
        h0_ref = refs[i]
        i += 1
    o_ref, h_ref = refs[i:i + 2]
    tail_ref = refs[i + 2] if carry else None

    @pl.when(pl.program_id(1) == 0)
    def _():
        h_ref[...] = h0_ref[...] if has_state else jnp.zeros_like(h_ref)
        if carry:
            tail_ref[...] = jnp.zeros_like(tail_ref)

    for s in range(ns):
        _ssd_stream(z_ref.at[s], xbc_ref.at[s], dtr_ref.at[s], None if carry else halo_ref.at[s], consts,
                    o_ref.at[s], h_ref.at[s], tail_ref.at[s] if carry else None, nb=nb, seg=seg, carry=carry)


def _ssd_stream(z_ref, xbc_ref, dtr_ref, halo_ref, consts, o_ref, h_ref, tail_ref, *, nb, seg, carry):
    cw_ref, cb_ref, dtb_ref, alog_ref, d_ref, ng_ref = consts
    r = z_ref.shape[0]
    hp = SSD_HEADS * SSD_HEADDIM
    gp = hp // SSD_GROUPS

    x = xbc_ref[...]
    row = lax.broadcasted_iota(jnp.int32, (r, SSD_CONV_DIM), 0)
    row8 = lax.broadcasted_iota(jnp.int32, (SUBLANES, SSD_CONV_DIM), 0)

    def prev(k):
        xs = pltpu.roll(x, k, 0)
        if carry:
            tl = pltpu.roll(tail_ref[...], k, 0)
            top = jnp.where(row8 < k, tl, xs[:SUBLANES])
            return jnp.concatenate([top, xs[SUBLANES:]], axis=0)
        sh = SSD_CONV - 1 - k
        hal = halo_ref[...]
        hs_ = hal if sh == 0 else pltpu.roll(hal, r - sh, 0)
        return jnp.where(row % seg >= k, xs, hs_)

    u = (cb_ref[...] + x * cw_ref[3:4, :] + prev(1) * cw_ref[2:3, :] + prev(2) * cw_ref[1:2, :]
         + prev(3) * cw_ref[0:1, :])
    if carry:
        tail_ref[...] = x[r - SUBLANES:, :]
    u = _silu(u)

    ri = lax.broadcasted_iota(jnp.int32, (r, r), 0)
    ci = lax.broadcasted_iota(jnp.int32, (r, r), 1)
    if nb > 1:
        same = (ri // seg) == (ci // seg)
        tril = jnp.logical_and(same, ci <= ri)
        ones = jnp.where(same, 1.0, 0.0).astype(F32)
    else:
        tril = ci <= ri
        ones = jnp.ones((r, r), F32)
    tri = jnp.where(tril, 1.0, 0.0).astype(F32)

    dt = _softplus(dtr_ref[...] + dtb_ref[...])
    dta = dt * (-jnp.exp(alog_ref[...]))
    cs = _dot(tri, dta, HI)
    tot = _dot(ones, dta, HI)
    ecs = jnp.exp(cs)
    wend = jnp.exp(tot - cs) * dt
    etot = jnp.exp(tot)

    lane = lax.broadcasted_iota(jnp.int32, (r, LANES), 1)
    rb = lax.broadcasted_iota(jnp.int32, (r, LANES), 0) // seg
    lo = lane < SSD_HEADDIM

    def pair(a, b):
        return jnp.where(lo, a, b)

    def hs(a, hd):
        return a[:, hd * LANES:(hd + 1) * LANES]

    zz = z_ref[...]
    hall = h_ref[...].astype(BF16)
    for g in range(SSD_GROUPS):
        h0i, h1i = 2 * g, 2 * g + 1
        gs = slice(g * LANES, (g + 1) * LANES)
        xg = u[:, gs]
        bg = u[:, SSD_D + g * SSD_STATE:SSD_D + (g + 1) * SSD_STATE].astype(BF16)
        cg = u[:, SSD_D + SSD_GROUPS * SSD_STATE + g * SSD_STATE:
               SSD_D + SSD_GROUPS * SSD_STATE + (g + 1) * SSD_STATE].astype(BF16)
        cbm = _dot_nt(cg, bg)
        w0 = (cbm * jnp.where(tril, jnp.exp(hs(cs, h0i) - hs(cs, h0i).T), 0.0)).astype(BF16)
        w1 = (cbm * jnp.where(tril, jnp.exp(hs(cs, h1i) - hs(cs, h1i).T), 0.0)).astype(BF16)
        xdt = (xg * pair(hs(dt, h0i), hs(dt, h1i))).astype(BF16)
        y = pair(_dot(w0, xdt), _dot(w1, xdt))
        zst = _dot_nt(cg, hall)
        if nb == 1:
            yst = zst[:, g * gp:(g + 1) * gp]
        else:
            yst = jnp.zeros((r, gp), F32)
            for b in range(nb):
                yst = jnp.where(rb == b, zst[:, b * hp + g * gp:b * hp + (g + 1) * gp], yst)
        y = y + yst * pair(hs(ecs, h0i), hs(ecs, h1i)) + d_ref[:, gs] * xg
        y = y * _silu(zz[:, gs])
        o_ref[:, gs] = _rms(y, ng_ref[:, gs]).astype(o_ref.dtype)

        xw = (xg * pair(hs(wend, h0i), hs(wend, h1i))).astype(BF16)
        if nb > 1:
            xw = jnp.concatenate([jnp.where(rb == b, xw, jnp.zeros_like(xw)) for b in range(nb)], axis=1)
        dh = _dot_tn(xw, bg)
        for b in range(nb):
            r0 = b * seg if nb > 1 else 0
            dec = jnp.concatenate(
                [jnp.broadcast_to(hs(etot, h0i)[r0:r0 + 1, :], (SSD_HEADDIM, LANES)),
                 jnp.broadcast_to(hs(etot, h1i)[r0:r0 + 1, :], (SSD_HEADDIM, LANES))], axis=0)
            sl = slice(b * hp + g * gp, b * hp + (g + 1) * gp)
            h_ref[sl, :] = h_ref[sl, :] * dec + dh[b * gp:(b + 1) * gp, :]


def ssd(z, xbc, dtr, halo, h0, p, nb, seg, ns):
    n_groups, lg, _ = z.shape
    r = SSD_CHUNK
    carry = halo is None
    n_chunks = lg // r
    assert lg % r == 0 and n_groups % ns == 0 and (carry or n_chunks == 1)
    hp = SSD_HEADS * SSD_HEADDIM
    row_spec = lambda n: pl.BlockSpec((ns, r, n), lambda b, c: (b, c, 0))
    st_spec = pl.BlockSpec((ns, nb * hp, SSD_STATE), lambda b, c: (b, 0, 0))
    in_specs = [row_spec(SSD_D), row_spec(SSD_CONV_DIM), row_spec(SSD_HEADS * LANES)]
    args = [z, xbc, dtr]
    if not carry:
        in_specs.append(row_spec(SSD_CONV_DIM))
        args.append(halo)
    in_specs += [_full((SUBLANES, SSD_CONV_DIM)), _full((1, SSD_CONV_DIM)), _full((1, SSD_HEADS * LANES)),
                 _full((1, SSD_HEADS * LANES)), _full((1, SSD_D)), _full((1, SSD_D))]
    args += [p["ssd_conv_w"], p["ssd_conv_b"], p["ssd_dt_bias"], p["ssd_a_log"], p["ssd_d"], p["ssd_norm"]]
    if h0 is not None:
        in_specs.append(st_spec)
        args.append(h0)
    scratch = [pltpu.VMEM((ns, SUBLANES, SSD_CONV_DIM), F32)] if carry else []
    return pl.pallas_call(
        functools.partial(_ssd_body, ns=ns, nb=nb, seg=seg, carry=carry, has_state=h0 is not None),
        grid=(n_groups // ns, n_chunks), in_specs=in_specs,
        out_specs=[row_spec(SSD_D), st_spec],
        out_shape=[jax.ShapeDtypeStruct((n_groups, lg, SSD_D), BF16),
                   jax.ShapeDtypeStruct((n_groups, nb * hp, SSD_STATE), F32)],
        scratch_shapes=scratch,
        compiler_params=_cparams("parallel", "arbitrary"), name="ssd",
    )(*args)


def _gla_body(*refs, ns, nb, seg, c, has_state):
    hg3_ref, wg_ref, bg_ref, m_ref, ng_ref = refs[0:5]
    i = 5
    s0_ref = None
    if has_state:
        s0_ref = refs[i]
        i += 1
    o3_ref, st3_ref = refs[i:i + 2]
    r = hg3_ref.shape[1]

    @pl.when(pl.program_id(1) == 0)
    def _():
        st3_ref[...] = s0_ref[...] if has_state else jnp.zeros_like(st3_ref)

    ri = lax.broadcasted_iota(jnp.int32, (c, c), 0)
    ci = lax.broadcasted_iota(jnp.int32, (c, c), 1)
    if nb > 1:
        same = (ri // seg) == (ci // seg)
        tril = jnp.logical_and(same, ci <= ri)
        ones = jnp.where(same, 1.0, 0.0).astype(F32)
    else:
        tril = ci <= ri
        ones = jnp.ones((c, c), F32)
    tri = jnp.where(tril, 1.0, 0.0).astype(F32)
    klane = lax.broadcasted_iota(jnp.int32, (c, GLA_KD), 1) // GLA_DK
    vlane = lax.broadcasted_iota(jnp.int32, (c, GLA_VD), 1) // GLA_DV
    rbv = lax.broadcasted_iota(jnp.int32, (c, GLA_VD), 0) // seg
    blk = (lax.broadcasted_iota(jnp.int32, (GLA_VD, GLA_KD), 0) // GLA_DV
           == lax.broadcasted_iota(jnp.int32, (GLA_VD, GLA_KD), 1) // GLA_DK)

    for sub, s in [(a, b) for a in range(r // c) for b in range(ns)]:
        hg_ref, o_ref, st_ref = hg3_ref.at[s], o3_ref.at[s], st3_ref.at[s]
        rs = slice(sub * c, (sub + 1) * c)
        q = hg_ref[rs, 0:GLA_KD] * (GLA_DK ** -0.5)
        k = hg_ref[rs, GLA_KD:2 * GLA_KD]
        vb = hg_ref[rs, 2 * GLA_KD:2 * GLA_KD + GLA_VD].astype(BF16)
        gg = hg_ref[rs, 2 * GLA_KD + GLA_VD:2 * GLA_KD + 2 * GLA_VD]
        glr = hg_ref[rs, 2 * GLA_KD + 2 * GLA_VD:2 * GLA_KD + 2 * GLA_VD + LANES].astype(BF16)
        gate = _log_sigmoid(_dot(glr, wg_ref[...]) + bg_ref[...]) * (1.0 / GLA_GATE_TAU)
        bc = _dot(tri, gate, HI)
        tot = _dot(ones, gate, HI)
        qt = (q * jnp.exp(bc)).astype(BF16)
        kt = (k * jnp.exp(-bc)).astype(BF16)
        kd = (k * jnp.exp(tot - bc)).astype(BF16)
        o = jnp.zeros((c, GLA_VD), F32)
        for hd in range(GLA_HEADS):
            a = _dot_nt(jnp.where(klane == hd, qt, jnp.zeros_like(qt)), kt)
            a = jnp.where(tril, a, 0.0).astype(BF16)
            o = o + _dot(a, jnp.where(vlane == hd, vb, jnp.zeros_like(vb)))
        zs = _dot_nt(qt, st_ref[...].astype(BF16))
        if nb == 1:
            o = o + zs
        else:
            ost = jnp.zeros((c, GLA_VD), F32)
            for b in range(nb):
                ost = jnp.where(rbv == b, zs[:, b * GLA_VD:(b + 1) * GLA_VD], ost)
            o = o + ost
        msq = _dot((o * o).astype(BF16), m_ref[...])
        on = o * lax.rsqrt(msq + EPS) * ng_ref[...]
        o_ref[rs, :] = (on * _silu(gg)).astype(o_ref.dtype)

        vexp = vb
        if nb > 1:
            vexp = jnp.concatenate([jnp.where(rbv == b, vb, jnp.zeros_like(vb)) for b in range(nb)], axis=1)
        ds = _dot_tn(vexp, kd)
        etot = jnp.exp(tot)
        for b in range(nb):
            r0 = b * seg if nb > 1 else 0
            sl = slice(b * GLA_VD, (b + 1) * GLA_VD)
            st_ref[sl, :] = st_ref[sl, :] * etot[r0:r0 + 1, :] + jnp.where(blk, ds[sl, :], 0.0)


def gla(hg, s0, p, nb, seg, c, r, ns):
    n_groups, lg, width = hg.shape
    n_steps = lg // r
    assert lg % r == 0 and r % c == 0 and n_groups % ns == 0
    row_spec = lambda n: pl.BlockSpec((ns, r, n), lambda b, s: (b, s, 0))
    st_spec = pl.BlockSpec((ns, nb * GLA_VD, GLA_KD), lambda b, s: (b, 0, 0))
    in_specs = [row_spec(width), _full((LANES, GLA_KD)), _full((1, GLA_KD)), _full((GLA_VD, GLA_VD)),
                _full((1, GLA_VD))]
    args = [hg, p["gla_w_gate"], p["gla_b_gate"], p["m64"], p["gla_norm"]]
    if s0 is not None:
        in_specs.append(st_spec)
        args.append(s0)
    return pl.pallas_call(
        functools.partial(_gla_body, ns=ns, nb=nb, seg=seg, c=c, has_state=s0 is not None),
        grid=(n_groups // ns, n_steps), in_specs=in_specs,
        out_specs=[row_spec(GLA_VD), st_spec],
        out_shape=[jax.ShapeDtypeStruct((n_groups, lg, GLA_VD), BF16),
                   jax.ShapeDtypeStruct((n_groups, nb * GLA_VD, GLA_KD), F32)],
        compiler_params=_cparams("parallel", "arbitrary"), name="gla",
    )(*args)


def _q_absorb_body(q_ref, wt_ref, sel_ref, qa_out, qr_out):
    for hd in range(MLA_HEADS):
        qh = q_ref[:, hd * LANES:(hd + 1) * LANES].astype(F32)
        qa_out[:, hd * LANES:(hd + 1) * LANES] = _dot(qh, wt_ref[hd], HI).astype(BF16)
        qr_out[:, hd * LANES:(hd + 1) * LANES] = _dot(qh, sel_ref[...], HI).astype(BF16)


def q_absorb(q_cat, p, tm):
    rows, hw = q_cat.shape
    row_spec = pl.BlockSpec((tm, hw), lambda i: (i, 0))
    return pl.pallas_call(
        _q_absorb_body, grid=(rows // tm,),
        in_specs=[row_spec, _full((MLA_HEADS, LANES, LANES)), _full((LANES, LANES))],
        out_specs=[row_spec, row_spec],
        out_shape=[jax.ShapeDtypeStruct((rows, hw), BF16)] * 2,
        compiler_params=_cparams("parallel"), name="q_absorb",
    )(q_cat, p["wukt_g"], p["rope_sel"])


PAGED_CP = 16
PAGED_SUB = 512


def _mla_paged_body(pt_ref, ptn_ref, wukt_ref, qa_ref, qr_ref, cnew_ref, krnew_ref, wuv_ref, lat_hbm, kr_hbm,
                    o_ref, lat_buf, kr_buf, sem, m_ref, l_ref, acc_ref, *, layer, n_pages, n_batch, seq, cp):
    bi = pl.program_id(0)
    nc = n_pages // cp
    nq = MLA_HEADS * seq
    nk = MLA_HEADS * MLA_NOPE

    def copies(table, chunk, slot):
        out = []
        for pg in range(cp):
            pid = table[0, 0, chunk * cp + pg]
            dst = pl.ds(pg * PAGE_SIZE, PAGE_SIZE)
            out.append(pltpu.make_async_copy(lat_hbm.at[layer, pid], lat_buf.at[slot, dst, :], sem.at[0, slot]))
            out.append(pltpu.make_async_copy(kr_hbm.at[layer, pid], kr_buf.at[slot, :, dst], sem.at[1, slot]))
        return out

    def start(table, chunk, slot):
        for cpy in copies(table, chunk, slot):
            cpy.start()

    def wait(slot):
        for cpy in copies(pt_ref, 0, slot):
            cpy.wait()

    @pl.when(bi == 0)
    def _():
        start(pt_ref, 0, 0)

    m_ref[...] = jnp.full(m_ref.shape, -jnp.inf, F32)
    l_ref[...] = jnp.zeros(l_ref.shape, F32)
    acc_ref[...] = jnp.zeros(acc_ref.shape, F32)
    lhs = jnp.concatenate([wukt_ref[...], qa_ref[0]], axis=0)
    qr = qr_ref[0]

    def scores(latb, krt):
        big = _dot_nt(lhs, latb)
        kt = big[0:nk, :]
        ss = jnp.sum((kt * kt).reshape(MLA_HEADS, MLA_NOPE, kt.shape[1]), axis=1) * (1.0 / MLA_NOPE)
        rinv = lax.rsqrt(ss + EPS)
        rexp = jnp.concatenate([jnp.broadcast_to(rinv[hd:hd + 1, :], (seq, rinv.shape[1]))
                                for hd in range(MLA_HEADS)], axis=0)
        return big[nk:, :] * rexp + _dot(qr, krt.astype(BF16))

    def attend(lat, krt, mask):
        keys = lat.shape[0]
        sub = min(PAGED_SUB, keys)
        latb = lat.astype(BF16)
        s = jnp.concatenate([scores(latb[i * sub:(i + 1) * sub, :], krt[:, i * sub:(i + 1) * sub])
                             for i in range(keys // sub)], axis=1)
        if mask is not None:
            s = jnp.where(mask, s, -jnp.inf)
        m_old = m_ref[...]
        m_new = jnp.maximum(m_old, jnp.max(s, axis=-1, keepdims=True))
        pr = jnp.exp(s - jnp.concatenate([m_new] * (keys // LANES), axis=1))
        corr = jnp.exp(m_old - m_new)
        l_ref[...] = l_ref[...] * corr + jnp.sum(pr, axis=-1, keepdims=True)
        acc_ref[...] = acc_ref[...] * corr + _dot(pr.astype(BF16), latb)
        m_ref[...] = m_new

    lat_new = jnp.concatenate([cnew_ref[0], jnp.zeros((LANES - seq, MLA_KV_RANK), F32)], axis=0)
    qtok = lax.broadcasted_iota(jnp.int32, (nq, LANES), 0) % seq
    key = lax.broadcasted_iota(jnp.int32, (nq, LANES), 1)
    attend(lat_new, krnew_ref[0], key <= qtok)

    def pair_body(jj, carry_):
        for slot in range(2):
            chunk = 2 * jj + slot
            nxt = chunk + 1

            @pl.when(nxt < nc)
            def _():
                start(pt_ref, nxt, 1 - slot)

            @pl.when(jnp.logical_and(nxt == nc, bi + 1 < n_batch))
            def _():
                start(ptn_ref, 0, 1 - slot)

            wait(slot)
            attend(lat_buf[slot], kr_buf[slot], None)
        return carry_

    lax.fori_loop(0, nc // 2, pair_body, 0)

    olat = (acc_ref[...] / l_ref[...]).astype(BF16)
    o = jnp.zeros((seq, MLA_HEADS * MLA_V), F32)
    for hd in range(MLA_HEADS):
        o = o + _dot(olat[hd * seq:(hd + 1) * seq, :], wuv_ref[hd * LANES:(hd + 1) * LANES, :])
    o_ref[0] = o.astype(o_ref.dtype)


def mla_paged(page_table, wukt, qa, qr, c_new, krt_new, wuv_bd, cache_lat, cache_krt, layer):
    n_batch, n_pages = page_table.shape
    seq = c_new.shape[1]
    cp = min(PAGED_CP, n_pages // 2)
    assert n_pages % (2 * cp) == 0
    nq = MLA_HEADS * seq
    pt3 = page_table.reshape(n_batch, 1, n_pages)
    smem_spec = lambda f: pl.BlockSpec((1, 1, n_pages), f, memory_space=pltpu.SMEM)
    per_b = lambda shp: pl.BlockSpec((1,) + shp, lambda b: (b, 0, 0))
    kc = cp * PAGE_SIZE
    kr_new, cache_kr = krt_new, cache_krt
    return pl.pallas_call(
        functools.partial(_mla_paged_body, layer=layer, n_pages=n_pages, n_batch=n_batch, seq=seq, cp=cp),
        grid=(n_batch,),
        in_specs=[smem_spec(lambda b: (b, 0, 0)),
                  smem_spec(lambda b: (jnp.minimum(b + 1, n_batch - 1), 0, 0)),
                  _full(wukt.shape), per_b(qa.shape[1:]), per_b(qr.shape[1:]), per_b(c_new.shape[1:]),
                  per_b(kr_new.shape[1:]),
                  _full(wuv_bd.shape),
                  pl.BlockSpec(memory_space=pl.ANY), pl.BlockSpec(memory_space=pl.ANY)],
        out_specs=per_b((seq, MLA_HEADS * MLA_V)),
        out_shape=jax.ShapeDtypeStruct((n_batch, seq, MLA_HEADS * MLA_V), BF16),
        scratch_shapes=[pltpu.VMEM((2, kc, MLA_KV_RANK), F32), pltpu.VMEM((2, MLA_ROPE, kc), F32),
                        pltpu.SemaphoreType.DMA((2, 2)),
                        pltpu.VMEM((nq, LANES), F32), pltpu.VMEM((nq, LANES), F32), pltpu.VMEM((nq, LANES), F32)],
        compiler_params=_cparams("arbitrary"), name="mla_paged",
    )(pt3, pt3, wukt, qa, qr, c_new, kr_new, wuv_bd, cache_lat, cache_kr)


def _block_diag_mean(n, blocks):
    idx = jnp.arange(n)
    m = jnp.zeros((n, n), F32)
    for start, size in blocks:
        inb = jnp.logical_and(idx >= start, idx < start + size)
        m = m + jnp.where(jnp.logical_and(inb[:, None], inb[None, :]), 1.0 / size, 0.0)
    return m.astype(BF16)


def _rope_tables(pos):
    half = MLA_ROPE // 2
    inv = ROPE_THETA ** (-jnp.arange(half, dtype=F32) / half)
    ang = pos.astype(F32)[:, None] * inv[None, :]
    cos, sin = jnp.cos(ang), jnp.sin(ang)
    n = pos.shape[0]
    one, zero = jnp.ones((n, MLA_NOPE), F32), jnp.zeros((n, MLA_NOPE), F32)
    tail1, tail0 = jnp.ones((n, LANES - MLA_QK), F32), jnp.zeros((n, LANES - MLA_QK), F32)
    return (jnp.concatenate([one, cos, cos, tail1], axis=1), jnp.concatenate([zero, -sin, sin, tail0], axis=1))


def _pack_layer(l, w):
    f = lambda name: w[name][l]
    bf = lambda a: a.astype(BF16)
    p = {}
    w_in = f("w_in")
    o = 0
    cols = {}
    for name, size in (("cq", MLA_Q_RANK), ("ckv", MLA_KV_RANK), ("krr", MLA_ROPE), ("z", SSD_D), ("xbc", SSD_CONV_DIM),
                       ("dt", SSD_HEADS), ("gq", GLA_KD), ("gk", GLA_KD), ("gv", GLA_VD), ("glr", GLA_GATE_RANK),
                       ("gg", GLA_VD)):
        cols[name] = w_in[:, o:o + size]
        o += size
    d = w_in.shape[0]
    krr_pad = jnp.zeros((d, LANES), F32).at[:, MLA_NOPE:MLA_QK].set(cols["krr"])
    glr_pad = jnp.zeros((d, LANES), F32).at[:, :GLA_GATE_RANK].set(cols["glr"])
    p["w_mla"] = bf(jnp.concatenate([cols["cq"], cols["ckv"], krr_pad], axis=1))
    p["w_z"] = bf(cols["z"])
    p["w_xbc"] = bf(cols["xbc"])
    p["w_dt"] = bf(jnp.repeat(cols["dt"], LANES, axis=1))
    p["w_gla"] = bf(jnp.concatenate([cols["gq"], cols["gk"], cols["gv"], cols["gg"], glr_pad], axis=1))
    p["norm_mix"] = f("norm_mix")

    def head_pad(wm, n_real):
        k = wm.shape[0]
        out = jnp.zeros((k, MLA_HEADS, LANES), F32).at[:, :, :n_real].set(wm)
        return out.reshape(k, MLA_HEADS * LANES)

    p["wq_pad"] = bf(head_pad(f("mla_w_uq"), MLA_QK))
    p["wk_pad"] = bf(head_pad(f("mla_w_uk"), MLA_NOPE))
    scale = MLA_QK ** -0.5
    zpad = jnp.zeros((LANES - MLA_QK,), F32)
    p["gq_vec"] = (jnp.concatenate([f("mla_qn_norm"), f("mla_qr_norm"), zpad]) * scale).reshape(1, LANES)
    p["gk_vec"] = jnp.concatenate([f("mla_kn_norm"), jnp.zeros((LANES - MLA_NOPE,), F32)]).reshape(1, LANES)
    p["gkr_vec"] = jnp.concatenate([jnp.zeros((MLA_NOPE,), F32), f("mla_kr_norm"), zpad]).reshape(1, LANES)
    p["mla_q_norm"] = f("mla_q_norm").reshape(1, MLA_Q_RANK)
    p["mla_kv_norm"] = f("mla_kv_norm").reshape(1, MLA_KV_RANK)
    p["mh"] = _block_diag_mean(LANES, [(0, MLA_NOPE), (MLA_NOPE, MLA_ROPE)])
    w_uv = f("mla_w_uv")
    eye = jnp.eye(MLA_HEADS, dtype=F32)
    p["wuv_bd"] = bf(jnp.einsum("chv,hg->hcgv", w_uv, eye).reshape(MLA_HEADS * LANES, MLA_HEADS * MLA_V))
    w_uk = f("mla_w_uk")
    wt = jnp.transpose(w_uk, (1, 2, 0)) * f("mla_kn_norm")[None, :, None]
    p["wukt_g"] = jnp.zeros((MLA_HEADS, LANES, LANES), F32).at[:, :MLA_NOPE, :].set(wt)
    p["wukt"] = bf(jnp.transpose(w_uk, (1, 2, 0)).reshape(MLA_HEADS * MLA_NOPE, MLA_KV_RANK))
    p["rope_sel"] = jnp.zeros((LANES, LANES), F32).at[MLA_NOPE + jnp.arange(MLA_ROPE), jnp.arange(MLA_ROPE)].set(1.0)

    p["ssd_conv_w"] = jnp.zeros((SUBLANES, SSD_CONV_DIM), F32).at[:SSD_CONV].set(f("ssd_conv_w"))
    p["ssd_conv_b"] = f("ssd_conv_b").reshape(1, SSD_CONV_DIM)
    p["ssd_dt_bias"] = jnp.repeat(f("ssd_dt_bias"), LANES).reshape(1, SSD_HEADS * LANES)
    p["ssd_a_log"] = jnp.repeat(f("ssd_a_log"), LANES).reshape(1, SSD_HEADS * LANES)
    p["ssd_d"] = jnp.repeat(f("ssd_d"), SSD_HEADDIM).reshape(1, SSD_D)
    p["ssd_norm"] = f("ssd_norm").reshape(1, SSD_D)
    p["gla_w_gate"] = bf(jnp.zeros((LANES, GLA_KD), F32).at[:GLA_GATE_RANK].set(f("gla_w_gate")))
    p["gla_b_gate"] = f("gla_b_gate").reshape(1, GLA_KD)
    p["gla_norm"] = f("gla_norm").reshape(1, GLA_VD)
    p["m64"] = _block_diag_mean(MEM_D, [(i * 64, 64) for i in range(4)])
    w_o = f("w_o")
    p["wo_mla"] = bf(w_o[:MLA_HEADS * MLA_V])
    p["wo_ssd"] = bf(w_o[MLA_HEADS * MLA_V:MLA_HEADS * MLA_V + SSD_D])
    p["wo_gla"] = bf(w_o[MLA_HEADS * MLA_V + SSD_D:])
    p["norm_mem"] = f("norm_mem").reshape(1, D_MODEL)
    p["mem_wq"], p["mem_wk"], p["mem_wv"], p["mem_wo"] = bf(f("mem_wq")), bf(f("mem_wk")), bf(f("mem_wv")), bf(f("mem_wo"))
    p["mem_q_gain"] = (jnp.tile(f("mem_q_norm"), MEM_HEADS) * MEM_HEAD_DIM ** -0.5).reshape(1, MEM_D)
    p["mem_k_gain"] = jnp.tile(f("mem_k_norm"), MEM_HEADS).reshape(1, MEM_D)
    p["norm_ffn"] = f("norm_ffn").reshape(1, D_MODEL)
    w_up = f("ffn_w_up")
    p["ffn_wu"], p["ffn_wv"], p["ffn_wd"] = bf(w_up[:, :D_FF]), bf(w_up[:, D_FF:]), bf(f("ffn_w_down"))
    p["ffn_conv_w"] = jnp.zeros((SUBLANES, D_FF), F32).at[:FFN_CONV].set(f("ffn_conv_w"))
    p["ffn_conv_b"] = f("ffn_conv_b").reshape(1, D_FF)
    return p


def _pad_rows(state, seq):
    b, k, c = state.shape
    return jnp.pad(state, ((0, 0), (0, seq - k), (0, 0))).reshape(b * seq, c)


def _layer(x3, p, cos_t, sin_t, tm, sample):
    b, L, d = x3.shape
    rows = b * L
    x = x3.reshape(rows, d)
    h_mla, z, xbc, dtr, h_gla = norm_matmul(
        x, p["norm_mix"], [p["w_mla"], p["w_z"], p["w_xbc"], p["w_dt"], p["w_gla"]], [F32] * 5, tm)
    q_cat, k_cat, c_kv, c_bf, kr_vec = mla_prep(h_mla, cos_t, sin_t, p, tm)
    kr = kr_vec[:, MLA_NOPE:MLA_QK]
    hw = MLA_HEADS * LANES
    if sample is None:
        o_mla = mla_prompt(q_cat.reshape(b, L, hw), k_cat.reshape(b, L, hw), c_bf.reshape(b, L, 2 * LANES),
                           p["wuv_bd"])
        o_mla = o_mla.reshape(rows, MLA_HEADS * MLA_V)
        g3 = lambda a: a.reshape(b, L, a.shape[-1])
        ns = math.gcd(b, 4)
        o_ssd, ssd_h = ssd(g3(z), g3(xbc), g3(dtr), None, None, p, 1, L, ns)
        o_gla, gla_st = gla(g3(h_gla), None, p, 1, L, GLA_CHUNK, 256, ns)
    else:
        qa, qrp = q_absorb(q_cat, p, tm)
        to_hq = lambda a, n: a.reshape(b, L, MLA_HEADS, LANES)[..., :n].transpose(0, 2, 1, 3).reshape(b, MLA_HEADS * L, n)
        krt_new = jnp.pad(jnp.swapaxes(kr.reshape(b, L, MLA_ROPE), 1, 2), ((0, 0), (0, 0), (0, LANES - L)))
        o_mla = mla_paged(sample["page_table"], p["wukt"], to_hq(qa, LANES), to_hq(qrp, MLA_ROPE),
                          c_kv.reshape(b, L, LANES), krt_new, p["wuv_bd"], sample["cache_lat"], sample["cache_krt"],
                          sample["layer"])
        o_mla = o_mla.reshape(rows, MLA_HEADS * MLA_V)
        nb = SSD_CHUNK // L
        ng = b // nb
        g3 = lambda a: a.reshape(ng, nb * L, a.shape[-1])
        ns = math.gcd(ng, 2)
        o_ssd, ssd_h = ssd(g3(z), g3(xbc), g3(dtr), g3(_pad_rows(sample["ssd_conv"], L)),
                           sample["ssd"].reshape(ng, nb * SSD_D, SSD_STATE), p, nb, L, ns)
        eye = jnp.eye(GLA_HEADS, dtype=F32)
        st0 = jnp.einsum("bhkv,hg->bhvgk", sample["gla"], eye).reshape(ng, nb * GLA_VD, GLA_KD)
        o_gla, gla_st = gla(g3(h_gla), st0, p, nb, L, nb * L, nb * L, ns)
    o_ssd, o_gla = o_ssd.reshape(rows, SSD_D), o_gla.reshape(rows, GLA_VD)
    x = residual_matmul(x, [o_mla, o_ssd, o_gla], [p["wo_mla"], p["wo_ssd"], p["wo_gla"]], tm)

    if sample is None:
        mk, mv = mem_kv(p["mem_prompt"], p, 512)
        x = mem_attend(x, mk, mv, p, 512, 1, L, False)
    else:
        nbm = 128 // L
        x = mem_attend(x, sample["mem_kt"], sample["mem_vt"], p, nbm * L, nbm, L, True, sample["layer"])

    if sample is None:
        ft = 512
        x, u_tail = ffn(x, None, p, ft, True, L // ft, L, SUBLANES)
        ffn_conv = u_tail.reshape(b, L // ft, SUBLANES, D_FF)[:, -1, SUBLANES - (FFN_CONV - 1):, :]
    else:
        ft = min(rows, 512)
        x, u_full = ffn(x, _pad_rows(sample["ffn_conv"], L), p, ft, False, 1, L, ft)
        ffn_conv = u_full.reshape(b, L, D_FF)[:, L - (FFN_CONV - 1):, :]

    xbc3 = xbc.reshape(b, L, SSD_CONV_DIM)
    out = dict(
        x=x.reshape(b, L, d), lat=c_kv.reshape(b, L, MLA_KV_RANK), kr=kr.reshape(b, L, MLA_ROPE),
        ssd_conv=xbc3[:, L - (SSD_CONV - 1):, :], ssd=ssd_h.reshape(b, SSD_HEADS, SSD_HEADDIM, SSD_STATE),
        gla=jnp.einsum("bhvhk->bhkv", gla_st.reshape(b, GLA_HEADS, GLA_DV, GLA_HEADS, GLA_DK)),
        ffn_conv=ffn_conv)
    if sample is None:
        out["mem_k"] = mk.reshape(b, N_MEM, MEM_HEADS, MEM_HEAD_DIM)
        out["mem_v"] = mv.reshape(b, N_MEM, MEM_HEADS, MEM_HEAD_DIM)
    return out


def kernel(x_prompt, x_sample, cache_mla_latent, cache_mla_krope, cache_mem_k, cache_mem_v, state_ssd_conv, state_ssd, state_gla, state_ffn_conv, page_table, mem_prompt, norm_mix, w_in, mla_q_norm, mla_w_uq, mla_kv_norm, mla_w_uk, mla_w_uv, mla_qn_norm, mla_qr_norm, mla_kn_norm, mla_kr_norm, ssd_conv_w, ssd_conv_b, ssd_dt_bias, ssd_a_log, ssd_d, ssd_norm, gla_w_gate, gla_b_gate, gla_norm, w_o, norm_mem, mem_wq, mem_wk, mem_wv, mem_wo, mem_q_norm, mem_k_norm, norm_ffn, ffn_w_up, ffn_conv_w, ffn_conv_b, ffn_w_down):
    weights = dict(
        norm_mix=norm_mix, w_in=w_in, mla_q_norm=mla_q_norm, mla_w_uq=mla_w_uq, mla_kv_norm=mla_kv_norm,
        mla_w_uk=mla_w_uk, mla_w_uv=mla_w_uv, mla_qn_norm=mla_qn_norm, mla_qr_norm=mla_qr_norm,
        mla_kn_norm=mla_kn_norm, mla_kr_norm=mla_kr_norm, ssd_conv_w=ssd_conv_w, ssd_conv_b=ssd_conv_b,
        ssd_dt_bias=ssd_dt_bias, ssd_a_log=ssd_a_log, ssd_d=ssd_d, ssd_norm=ssd_norm, gla_w_gate=gla_w_gate,
        gla_b_gate=gla_b_gate, gla_norm=gla_norm, w_o=w_o, norm_mem=norm_mem, mem_wq=mem_wq, mem_wk=mem_wk,
        mem_wv=mem_wv, mem_wo=mem_wo, mem_q_norm=mem_q_norm, mem_k_norm=mem_k_norm, norm_ffn=norm_ffn,
        ffn_w_up=ffn_w_up, ffn_conv_w=ffn_conv_w, ffn_conv_b=ffn_conv_b, ffn_w_down=ffn_w_down)
    depth = w_in.shape[0]
    bp, lp, _ = x_prompt.shape
    bs, ls, _ = x_sample.shape
    assert ls >= SSD_CONV - 1 and SSD_CHUNK % ls == 0
    past_len = page_table.shape[1] * PAGE_SIZE
    tm_p, tm_s = 512, min(512, bs * ls)
    cos_p, sin_p = _rope_tables(jnp.arange(lp))
    cos_s, sin_s = _rope_tables(past_len + jnp.arange(ls))
    cos_s, sin_s = jnp.tile(cos_s, (tm_s // ls, 1)), jnp.tile(sin_s, (tm_s // ls, 1))
    mem2d = mem_prompt.reshape(bp * N_MEM, D_MODEL)
    cache_krt = jnp.swapaxes(cache_mla_krope, 2, 3)
    mem_kt = jnp.transpose(cache_mem_k, (0, 1, 3, 4, 2)).reshape(depth, bs, MEM_D, N_MEM)
    mem_vt = jnp.transpose(cache_mem_v, (0, 1, 3, 4, 2)).reshape(depth, bs, MEM_D, N_MEM)

    xp, xs = x_prompt, x_sample
    outs_p, outs_s = [], []
    for l in range(depth):
        p = _pack_layer(l, weights)
        p["mem_prompt"] = mem2d
        op = _layer(xp, p, cos_p, sin_p, tm_p, None)
        xp = op["x"]
        outs_p.append(op)
        sample = dict(layer=l, page_table=page_table, cache_lat=cache_mla_latent, cache_krt=cache_krt,
                      mem_kt=mem_kt, mem_vt=mem_vt, ssd_conv=state_ssd_conv[l], ssd=state_ssd[l],
                      gla=state_gla[l], ffn_conv=state_ffn_conv[l])
        os_ = _layer(xs, p, cos_s, sin_s, tm_s, sample)
        xs = os_["x"]
        outs_s.append(os_)

    st = lambda outs, key: jnp.stack([o[key] for o in outs])
    return (xp, xs,
            st(outs_p, "lat"), st(outs_p, "kr"), st(outs_p, "mem_k"), st(outs_p, "mem_v"),
            st(outs_p, "ssd_conv"), st(outs_p, "ssd"), st(outs_p, "gla"), st(outs_p, "ffn_conv"),
            st(outs_s, "lat"), st(outs_s, "kr"), st(outs_s, "ssd_conv"), st(outs_s, "ssd"), st(outs_s, "gla"),
            st(outs_s, "ffn_conv"))
```

```python
import functools
import math

import jax
import jax.numpy as jnp
from jax import lax
from jax.experimental import pallas as pl
from jax.experimental.pallas import tpu as pltpu

F32 = jnp.float32
BF16 = jnp.bfloat16
EPS = 1e-6
LANES = 128
SUBLANES = 8
VMEM_LIMIT = 56 * 1024 * 1024

D_MODEL = 1024
MLA_HEADS, MLA_NOPE, MLA_ROPE, MLA_V = 8, 64, 32, 64
MLA_QK = MLA_NOPE + MLA_ROPE
MLA_Q_RANK, MLA_KV_RANK = 256, 128
ROPE_THETA = 10000.0
PAGE_SIZE = 128
SSD_HEADS, SSD_HEADDIM, SSD_GROUPS, SSD_STATE, SSD_CONV = 4, 64, 2, 128, 4
SSD_D = SSD_HEADS * SSD_HEADDIM
SSD_CONV_DIM = SSD_D + 2 * SSD_GROUPS * SSD_STATE
SSD_CHUNK = 128
GLA_HEADS, GLA_DK, GLA_DV = 4, 32, 64
GLA_KD, GLA_VD = GLA_HEADS * GLA_DK, GLA_HEADS * GLA_DV
GLA_GATE_RANK, GLA_GATE_TAU, GLA_CHUNK = 16, 16.0, 64
N_MEM, MEM_HEADS, MEM_HEAD_DIM = 256, 4, 64
MEM_D = MEM_HEADS * MEM_HEAD_DIM
D_FF, FFN_CONV = 2816, 3
FFN_CHUNK = 256
FFN_ROWS = 256

HI = lax.Precision.HIGHEST


def _cparams(*sem):
    return pltpu.CompilerParams(dimension_semantics=sem, vmem_limit_bytes=VMEM_LIMIT)


def _dot(a, b, precision=None):
    return jnp.dot(a, b, preferred_element_type=F32, precision=precision)


def _dot_nt(a, b):
    return lax.dot_general(a, b, (((1,), (1,)), ((), ())), preferred_element_type=F32)


def _dot_tn(a, b):
    return lax.dot_general(a, b, (((0,), (0,)), ((), ())), preferred_element_type=F32)


def _split3(x):
    x1 = x.astype(BF16)
    r1 = x - x1.astype(F32)
    x2 = r1.astype(BF16)
    x3 = (r1 - x2.astype(F32)).astype(BF16)
    return x1, x2, x3


def _dot_sel(sel, x):
    return sum(_dot(sel, t) for t in _split3(x))


def _dot_rep(x, sel):
    return sum(_dot(t, sel) for t in _split3(x))


def _seg_total(cs, n_seg, seg_len):
    return jnp.concatenate(
        [jnp.broadcast_to(cs[(i + 1) * seg_len - 1:(i + 1) * seg_len, :], (seg_len, cs.shape[1]))
         for i in range(n_seg)], axis=0)


def _rms(x, g):
    return x * lax.rsqrt(jnp.mean(x * x, axis=-1, keepdims=True) + EPS) * g


def _sigmoid(x):
    return 1.0 / (1.0 + jnp.exp(-x))


def _silu(x):
    return x * _sigmoid(x)


def _softplus(x):
    return jnp.maximum(x, 0.0) + jnp.log1p(jnp.exp(-jnp.abs(x)))


def _log_sigmoid(x):
    return jnp.minimum(x, 0.0) - jnp.log1p(jnp.exp(-jnp.abs(x)))


def _full(shape):
    return pl.BlockSpec(shape, lambda *_: (0,) * len(shape))


def _resident(shape):
    return pl.BlockSpec(shape, lambda *_: (0,) * len(shape), pipeline_mode=pl.Buffered(1))


def _norm_mm_body(x_ref, g_ref, *refs, n_out, has_norm):
    w_refs, o_refs = refs[:n_out], refs[n_out:]
    x = x_ref[...]
    if has_norm:
        x = _rms(x, g_ref[...])
    xb = x.astype(BF16)
    for w_ref, o_ref in zip(w_refs, o_refs):
        o_ref[...] = _dot(xb, w_ref[...]).astype(o_ref.dtype)


def norm_matmul(x, g, ws, out_dtypes, tm, has_norm=True):
    rows, k = x.shape
    assert rows % tm == 0
    in_specs = [pl.BlockSpec((tm, k), lambda i: (i, 0)), _full((1, k))]
    in_specs += [_full(w.shape) for w in ws]
    out_specs = [pl.BlockSpec((tm, w.shape[1]), lambda i: (i, 0)) for w in ws]
    out_shape = [jax.ShapeDtypeStruct((rows, w.shape[1]), dt) for w, dt in zip(ws, out_dtypes)]
    return pl.pallas_call(
        functools.partial(_norm_mm_body, n_out=len(ws), has_norm=has_norm),
        grid=(rows // tm,), in_specs=in_specs, out_specs=out_specs, out_shape=out_shape,
        compiler_params=_cparams("parallel"), name="norm_matmul",
    )(x, g.reshape(1, k), *ws)


def _mla_prep_body(h_ref, cos_ref, sin_ref, qng_ref, wq_ref, gq_ref, kvg_ref, wk_ref, gk_ref, krg_ref, mh_ref,
                   q_out, k_out, ckv_out, cbf_out, kr_out):
    tm = h_ref.shape[0]
    h = h_ref[...]
    cq, ckv, krr = h[:, 0:256], h[:, 256:384], h[:, 384:512]
    cos, sin = cos_ref[...], sin_ref[...]
    lane = lax.broadcasted_iota(jnp.int32, (tm, LANES), 1)
    mh = mh_ref[...]

    def rope(xv):
        rot = jnp.where(lane < MLA_NOPE + MLA_ROPE // 2,
                        pltpu.roll(xv, LANES - MLA_ROPE // 2, 1), pltpu.roll(xv, MLA_ROPE // 2, 1))
        return xv * cos + rot * sin

    def seg_norm(xv, g):
        msq = _dot((xv * xv).astype(BF16), mh)
        return xv * lax.rsqrt(msq + EPS) * g

    q = _dot(_rms(cq, qng_ref[...]).astype(BF16), wq_ref[...])
    c = _rms(ckv, kvg_ref[...])
    cb = c.astype(BF16)
    ckv_out[...] = c
    cbf_out[...] = jnp.concatenate([cb, jnp.ones_like(cb)], axis=1)
    kr = rope(seg_norm(krr, krg_ref[...]))
    kr_out[...] = kr
    k = _dot(cb, wk_ref[...])
    gq, gk = gq_ref[...], gk_ref[...]
    for hd in range(MLA_HEADS):
        sl = slice(hd * LANES, (hd + 1) * LANES)
        q_out[:, sl] = rope(seg_norm(q[:, sl], gq)).astype(BF16)
        k_out[:, sl] = (seg_norm(k[:, sl], gk) + kr).astype(BF16)


def mla_prep(h_mla, cos_t, sin_t, p, tm):
    rows = h_mla.shape[0]
    nt = cos_t.shape[0] // tm
    row_spec = lambda n: pl.BlockSpec((tm, n), lambda i: (i, 0))
    tab_spec = pl.BlockSpec((tm, LANES), lambda i: (i % nt, 0))
    hw = MLA_HEADS * LANES
    return pl.pallas_call(
        _mla_prep_body, grid=(rows // tm,),
        in_specs=[row_spec(512), tab_spec, tab_spec, _full((1, 256)), _full((256, hw)), _full((1, LANES)),
                  _full((1, LANES)), _full((LANES, hw)), _full((1, LANES)), _full((1, LANES)), _full((LANES, LANES))],
        out_specs=[row_spec(hw), row_spec(hw), row_spec(LANES), row_spec(2 * LANES), row_spec(LANES)],
        out_shape=[jax.ShapeDtypeStruct((rows, hw), BF16), jax.ShapeDtypeStruct((rows, hw), BF16),
                   jax.ShapeDtypeStruct((rows, LANES), F32), jax.ShapeDtypeStruct((rows, 2 * LANES), BF16),
                   jax.ShapeDtypeStruct((rows, LANES), F32)],
        compiler_params=_cparams("parallel"), name="mla_prep",
    )(h_mla, cos_t, sin_t, p["mla_q_norm"], p["wq_pad"], p["gq_vec"], p["mla_kv_norm"], p["wk_pad"], p["gk_vec"],
      p["gkr_vec"], p["mh"])


def _mla_prompt_body(q_ref, k_ref, c_ref, wuv_ref, o_ref, m_ref, l_ref, acc_ref, olat_ref, *, tq):
    qi = pl.program_id(1)
    row = lax.broadcasted_iota(jnp.int32, (tq, tq), 0)
    col = lax.broadcasted_iota(jnp.int32, (tq, tq), 1)
    causal = col <= row
    m_ref[...] = jnp.full(m_ref.shape, -jnp.inf, F32)
    l_ref[...] = jnp.zeros(l_ref.shape, F32)
    acc_ref[...] = jnp.zeros(acc_ref.shape, F32)

    def step(j, masked):
        start = pl.multiple_of(j * tq, tq)
        cblk = c_ref[0, pl.ds(start, tq), :]
        for hd in range(MLA_HEADS):
            sl = slice(hd * LANES, (hd + 1) * LANES)
            s = _dot_nt(q_ref[0, :, sl], k_ref[0, pl.ds(start, tq), sl])
            if masked:
                s = jnp.where(causal, s, -jnp.inf)
            m_old = m_ref[hd]
            m_new = jnp.maximum(m_old, jnp.max(s, axis=-1, keepdims=True))
            p = jnp.exp2(s - jnp.concatenate([m_new] * (tq // LANES), axis=1))
            corr = jnp.exp2(m_old - m_new)
            pv = _dot(p.astype(BF16), cblk)
            acc_ref[hd] = acc_ref[hd] * corr + pv[:, :LANES]
            l_ref[hd] = l_ref[hd] * corr + pv[:, LANES:]
            m_ref[hd] = m_new

    def body(j, carry):
        step(j, False)
        return carry

    lax.fori_loop(0, qi, body, 0)
    step(qi, True)
    for hd in range(MLA_HEADS):
        olat_ref[:, hd * LANES:(hd + 1) * LANES] = (acc_ref[hd] / l_ref[hd]).astype(BF16)
    o_ref[0] = _dot(olat_ref[...], wuv_ref[...]).astype(o_ref.dtype)


def mla_prompt(q_cat, k_cat, c_bf, wuv_bd, tq=256):
    b, L, hw = q_cat.shape
    return pl.pallas_call(
        functools.partial(_mla_prompt_body, tq=tq), grid=(b, L // tq),
        in_specs=[pl.BlockSpec((1, tq, hw), lambda bi, qi: (bi, qi, 0)),
                  pl.BlockSpec((1, L, hw), lambda bi, qi: (bi, 0, 0)),
                  pl.BlockSpec((1, L, 2 * LANES), lambda bi, qi: (bi, 0, 0)),
                  _full(wuv_bd.shape)],
        out_specs=pl.BlockSpec((1, tq, MLA_HEADS * MLA_V), lambda bi, qi: (bi, qi, 0)),
        out_shape=jax.ShapeDtypeStruct((b, L, MLA_HEADS * MLA_V), BF16),
        scratch_shapes=[pltpu.VMEM((MLA_HEADS, tq, LANES), F32), pltpu.VMEM((MLA_HEADS, tq, LANES), F32),
                        pltpu.VMEM((MLA_HEADS, tq, LANES), F32), pltpu.VMEM((tq, hw), BF16)],
        compiler_params=_cparams("parallel", "arbitrary"), name="mla_prompt",
    )(q_cat, k_cat, c_bf, wuv_bd)


def _res_mm_body(x_ref, *refs, n_in):
    a_refs, w_refs, o_ref = refs[:n_in], refs[n_in:2 * n_in], refs[2 * n_in]
    acc = x_ref[...]
    for a_ref, w_ref in zip(a_refs, w_refs):
        acc = acc + _dot(a_ref[...], w_ref[...])
    o_ref[...] = acc


def residual_matmul(x, acts, ws, tm):
    rows, d = x.shape
    in_specs = [pl.BlockSpec((tm, d), lambda i: (i, 0))]
    in_specs += [pl.BlockSpec((tm, a.shape[1]), lambda i: (i, 0)) for a in acts]
    in_specs += [_full(w.shape) for w in ws]
    return pl.pallas_call(
        functools.partial(_res_mm_body, n_in=len(acts)), grid=(rows // tm,),
        in_specs=in_specs, out_specs=pl.BlockSpec((tm, d), lambda i: (i, 0)),
        out_shape=jax.ShapeDtypeStruct((rows, d), F32),
        compiler_params=_cparams("parallel"), name="out_proj",
    )(x, *acts, *ws)


def _mem_kv_body(x_ref, wk_ref, wv_ref, m_ref, g_ref, k_out, v_out):
    xb = x_ref[...].astype(BF16)
    k = _dot(xb, wk_ref[...])
    msq = _dot((k * k).astype(BF16), m_ref[...])
    k_out[...] = k * lax.rsqrt(msq + EPS) * g_ref[...]
    v_out[...] = _dot(xb, wv_ref[...])


def mem_kv(mem2d, p, tm):
    rows, d = mem2d.shape
    row = lambda n: pl.BlockSpec((tm, n), lambda i: (i, 0))
    return pl.pallas_call(
        _mem_kv_body, grid=(rows // tm,),
        in_specs=[row(d), _full((d, MEM_D)), _full((d, MEM_D)), _full((MEM_D, MEM_D)), _full((1, MEM_D))],
        out_specs=[row(MEM_D), row(MEM_D)],
        out_shape=[jax.ShapeDtypeStruct((rows, MEM_D), F32)] * 2,
        compiler_params=_cparams("parallel"), name="mem_kv",
    )(mem2d, p["mem_wk"], p["mem_wv"], p["m64"], p["mem_k_gain"])


def _mem_attend_body(x_ref, g_ref, wq_ref, m_ref, gq_ref, k_ref, v_ref, wo_ref, o_ref, *, nb, kv_t):
    r = x_ref.shape[0]
    x = x_ref[...]
    q = _dot(_rms(x, g_ref[...]).astype(BF16), wq_ref[...])
    msq = _dot((q * q).astype(BF16), m_ref[...])
    qn = (q * lax.rsqrt(msq + EPS) * gq_ref[...]).astype(BF16)
    if kv_t:
        kb = jnp.concatenate([k_ref[b] for b in range(nb)], axis=1).astype(BF16)
        vb = jnp.concatenate([v_ref[b] for b in range(nb)], axis=1).astype(BF16)
    else:
        kb = k_ref[...].astype(BF16)
        vb = v_ref[...].astype(BF16)
    lane = lax.broadcasted_iota(jnp.int32, (r, MEM_D), 1) // MEM_HEAD_DIM
    if nb > 1:
        rb = lax.broadcasted_iota(jnp.int32, (r, nb * N_MEM), 0) // (r // nb)
        cb = lax.broadcasted_iota(jnp.int32, (r, nb * N_MEM), 1) // N_MEM
        same = rb == cb
    o = jnp.zeros((r, MEM_D), F32)
    for hd in range(MEM_HEADS):
        qm = jnp.where(lane == hd, qn, jnp.zeros_like(qn))
        s = _dot(qm, kb) if kv_t else _dot_nt(qm, kb)
        if nb > 1:
            s = jnp.where(same, s, -jnp.inf)
        pr = jnp.exp(s - jnp.max(s, axis=-1, keepdims=True))
        prb = pr.astype(BF16)
        pv = (_dot_nt(prb, vb) if kv_t else _dot(prb, vb)) / jnp.sum(pr, axis=-1, keepdims=True)
        o = jnp.where(lane == hd, pv, o)
    o_ref[...] = x + _dot(o.astype(BF16), wo_ref[...])


def mem_attend(x, k, v, p, r, nb, rows_per_batch, kv_t, layer=0):
    rows, d = x.shape
    if kv_t:
        kv_spec = pl.BlockSpec((None, nb, MEM_D, N_MEM), lambda i: (layer, i, 0, 0))
    elif nb == 1:
        kv_spec = pl.BlockSpec((N_MEM, MEM_D), lambda i: (i // (rows_per_batch // r), 0))
    else:
        kv_spec = pl.BlockSpec((nb * N_MEM, MEM_D), lambda i: (i, 0))
    return pl.pallas_call(
        functools.partial(_mem_attend_body, nb=nb, kv_t=kv_t), grid=(rows // r,),
        in_specs=[pl.BlockSpec((r, d), lambda i: (i, 0)), _full((1, d)), _full((d, MEM_D)), _full((MEM_D, MEM_D)),
                  _full((1, MEM_D)), kv_spec, kv_spec, _full((MEM_D, d))],
        out_specs=pl.BlockSpec((r, d), lambda i: (i, 0)),
        out_shape=jax.ShapeDtypeStruct((rows, d), F32),
        compiler_params=_cparams("parallel"), name="mem_attend",
    )(x, p["norm_mem"], p["mem_wq"], p["m64"], p["mem_q_gain"], k, v, p["mem_wo"])


def _ffn_body(x_ref, g_ref, *refs, carry, tiles_per_seq, seg, tail_rows):
    if carry:
        wu_ref, wv_ref, wd_ref, cw_ref, cb_ref, o_ref, tail_ref, xn_ref, carry_ref = refs
        halo_ref = None
    else:
        halo_ref, wu_ref, wv_ref, wd_ref, cw_ref, cb_ref, o_ref, tail_ref, xn_ref = refs
        carry_ref = None
    tm = x_ref.shape[0]
    x = x_ref[...]
    xn_ref[...] = _rms(x, g_ref[...]).astype(BF16)
    row = lax.broadcasted_iota(jnp.int32, (tm, FFN_CHUNK), 0)
    if carry:
        @pl.when(pl.program_id(0) % tiles_per_seq == 0)
        def _():
            carry_ref[...] = jnp.zeros_like(carry_ref)
    else:
        t = row % seg
    n_chunks = D_FF // FFN_CHUNK

    def up(j):
        sl = slice(j * FFN_CHUNK, (j + 1) * FFN_CHUNK)
        xn = xn_ref[...]
        return _dot(xn, wu_ref[:, sl]), _dot(xn, wv_ref[:, sl])

    acc = x
    uv = up(0)
    for j in range(n_chunks):
        sl = slice(j * FFN_CHUNK, (j + 1) * FFN_CHUNK)
        u, v = uv
        if j + 1 < n_chunks:
            uv = up(j + 1)
        um1 = pltpu.roll(u, 1, 0)
        um2 = pltpu.roll(u, 2, 0)
        if carry:
            c6 = carry_ref[6:7, sl]
            c7 = carry_ref[7:8, sl]
            um1 = jnp.where(row == 0, c7, um1)
            um2 = jnp.where(row == 0, c6, jnp.where(row == 1, c7, um2))
            carry_ref[:, sl] = u[tm - SUBLANES:, :]
        else:
            hal = halo_ref[:, sl]
            um1 = jnp.where(t >= 1, um1, pltpu.roll(hal, tm - 1, 0))
            um2 = jnp.where(t >= 2, um2, hal)
        tail_ref[:, sl] = u[tm - tail_rows:, :]
        uc = um2 * cw_ref[0:1, sl] + um1 * cw_ref[1:2, sl] + u * cw_ref[2:3, sl] + cb_ref[:, sl]
        a = (_silu(uc) * v).astype(BF16)
        acc = acc + _dot(a, wd_ref[sl, :])
    o_ref[...] = acc


def ffn(x, halo, p, tm, carry, tiles_per_seq, seg, tail_rows):
    rows, d = x.shape
    n_tiles = rows // tm
    row_spec = lambda n: pl.BlockSpec((tm, n), lambda i: (i, 0))
    in_specs = [row_spec(d), _full((1, d))]
    args = [x, p["norm_ffn"]]
    if not carry:
        in_specs.append(row_spec(D_FF))
        args.append(halo)
    in_specs += [_resident((d, D_FF)), _resident((d, D_FF)), _resident((D_FF, d)), _full((SUBLANES, D_FF)),
                 _full((1, D_FF))]
    args += [p["ffn_wu"], p["ffn_wv"], p["ffn_wd"], p["ffn_conv_w"], p["ffn_conv_b"]]
    scratch = [pltpu.VMEM((tm, d), BF16)]
    if carry:
        scratch.append(pltpu.VMEM((SUBLANES, D_FF), F32))
    return pl.pallas_call(
        functools.partial(_ffn_body, carry=carry, tiles_per_seq=tiles_per_seq, seg=seg, tail_rows=tail_rows),
        grid=(n_tiles,), in_specs=in_specs,
        out_specs=[row_spec(d), pl.BlockSpec((tail_rows, D_FF), lambda i: (i, 0))],
        out_shape=[jax.ShapeDtypeStruct((rows, d), F32), jax.ShapeDtypeStruct((n_tiles * tail_rows, D_FF), F32)],
        scratch_shapes=scratch,
        compiler_params=_cparams("arbitrary"), name="ffn",
    )(*args)


def _ssd_body(*refs, ns, nb, seg, carry, has_state):
    z_ref, xbc_ref, dtr_ref = refs[0:3]
    i = 3
    halo_ref = None
    if not carry:
        halo_ref = refs[i]
        i += 1
    consts = refs[i:i + 7]
    i += 7
    h0_ref = None
    if has_state:
        h0_ref = refs[i]
        i += 1
    o_ref, h_ref = refs[i:i + 2]
    tail_ref = refs[i + 2] if carry else None

    @pl.when(pl.program_id(1) == 0)
    def _():
        h_ref[...] = h0_ref[...] if has_state else jnp.zeros_like(h_ref)
        if carry:
            tail_ref[...] = jnp.zeros_like(tail_ref)

    for s in range(ns):
        _ssd_stream(z_ref.at[s], xbc_ref.at[s], dtr_ref.at[s], None if carry else halo_ref.at[s], consts,
                    o_ref.at[s], h_ref.at[s], tail_ref.at[s] if carry else None, nb=nb, seg=seg, carry=carry)


def _ssd_stream(z_ref, xbc_ref, dtr_ref, halo_ref, consts, o_ref, h_ref, tail_ref, *, nb, seg, carry):
    cw_ref, cb_ref, dtb_ref, alog_ref, d_ref, ng_ref, rep_ref = consts
    r = z_ref.shape[0]
    hp = SSD_HEADS * SSD_HEADDIM
    gp = hp // SSD_GROUPS

    x = xbc_ref[...]
    row = lax.broadcasted_iota(jnp.int32, (r, SSD_CONV_DIM), 0)
    row8 = lax.broadcasted_iota(jnp.int32, (SUBLANES, SSD_CONV_DIM), 0)

    def prev(k):
        xs = pltpu.roll(x, k, 0)
        if carry:
            tl = pltpu.roll(tail_ref[...], k, 0)
            top = jnp.where(row8 < k, tl, xs[:SUBLANES])
            return jnp.concatenate([top, xs[SUBLANES:]], axis=0)
        sh = SSD_CONV - 1 - k
        hal = halo_ref[...]
        hs_ = hal if sh == 0 else pltpu.roll(hal, r - sh, 0)
        return jnp.where(row % seg >= k, xs, hs_)

    u = (cb_ref[...] + x * cw_ref[3:4, :] + prev(1) * cw_ref[2:3, :] + prev(2) * cw_ref[1:2, :]
         + prev(3) * cw_ref[0:1, :])
    if carry:
        tail_ref[...] = x[r - SUBLANES:, :]
    u = _silu(u)

    ri = lax.broadcasted_iota(jnp.int32, (r, r), 0)
    ci = lax.broadcasted_iota(jnp.int32, (r, r), 1)
    if nb > 1:
        tril = jnp.logical_and((ri // seg) == (ci // seg), ci <= ri)
    else:
        tril = ci <= ri
    tri = jnp.where(tril, 1.0, 0.0).astype(BF16)

    dtc = _softplus(dtr_ref[...] + dtb_ref[...])
    csc = _dot_sel(tri, dtc * (-jnp.exp(alog_ref[...])))
    dt = _dot_rep(dtc, rep_ref[...])
    cs = _dot_rep(csc, rep_ref[...])
    tot = _seg_total(cs, nb, seg) if nb > 1 else _seg_total(cs, 1, r)

    lane = lax.broadcasted_iota(jnp.int32, (r, LANES), 1)
    rb = lax.broadcasted_iota(jnp.int32, (r, LANES), 0) // seg
    lo = lane < SSD_HEADDIM

    def pair(a, b):
        return jnp.where(lo, a, b)

    def hs(a, hd):
        return a[:, hd * LANES:(hd + 1) * LANES]

    zz = z_ref[...]
    hall = h_ref[...].astype(BF16)
    for g in range(SSD_GROUPS):
        h0i, h1i = 2 * g, 2 * g + 1
        gs = slice(g * LANES, (g + 1) * LANES)
        xg = u[:, gs]
        bg = u[:, SSD_D + g * SSD_STATE:SSD_D + (g + 1) * SSD_STATE].astype(BF16)
        cg = u[:, SSD_D + SSD_GROUPS * SSD_STATE + g * SSD_STATE:
               SSD_D + SSD_GROUPS * SSD_STATE + (g + 1) * SSD_STATE].astype(BF16)
        cbm = _dot_nt(cg, bg)
        w0 = (cbm * jnp.where(tril, jnp.exp(hs(cs, h0i) - hs(cs, h0i).T), 0.0)).astype(BF16)
        w1 = (cbm * jnp.where(tril, jnp.exp(hs(cs, h1i) - hs(cs, h1i).T), 0.0)).astype(BF16)
        dtp = pair(hs(dt, h0i), hs(dt, h1i))
        csp = pair(hs(cs, h0i), hs(cs, h1i))
        totp = pair(hs(tot, h0i), hs(tot, h1i))
        xdt = (xg * dtp).astype(BF16)
        y = pair(_dot(w0, xdt), _dot(w1, xdt))
        zst = _dot_nt(cg, hall)
        if nb == 1:
            yst = zst[:, g * gp:(g + 1) * gp]
        else:
            yst = jnp.zeros((r, gp), F32)
            for b in range(nb):
                yst = jnp.where(rb == b, zst[:, b * hp + g * gp:b * hp + (g + 1) * gp], yst)
        y = y + yst * jnp.exp(csp) + d_ref[:, gs] * xg
        y = y * _silu(zz[:, gs])
        o_ref[:, gs] = _rms(y, ng_ref[:, gs]).astype(o_ref.dtype)

        xw = (xg * (jnp.exp(totp - csp) * dtp)).astype(BF16)
        if nb > 1:
            xw = jnp.concatenate([jnp.where(rb == b, xw, jnp.zeros_like(xw)) for b in range(nb)], axis=1)
        dh = _dot_tn(xw, bg)
        for b in range(nb):
            r0 = b * seg if nb > 1 else 0
            dec = jnp.concatenate(
                [jnp.broadcast_to(jnp.exp(hs(tot, h0i)[r0:r0 + 1, :]), (SSD_HEADDIM, LANES)),
                 jnp.broadcast_to(jnp.exp(hs(tot, h1i)[r0:r0 + 1, :]), (SSD_HEADDIM, LANES))], axis=0)
            sl = slice(b * hp + g * gp, b * hp + (g + 1) * gp)
            h_ref[sl, :] = h_ref[sl, :] * dec + dh[b * gp:(b + 1) * gp, :]


def ssd(z, xbc, dtr, halo, h0, p, nb, seg, ns, layer=0):
    n_groups, lg, _ = z.shape
    r = SSD_CHUNK
    carry = halo is None
    n_chunks = lg // r
    assert lg % r == 0 and n_groups % ns == 0 and (carry or n_chunks == 1)
    hp = SSD_HEADS * SSD_HEADDIM
    row_spec = lambda n: pl.BlockSpec((ns, r, n), lambda b, c: (b, c, 0))
    st_spec = pl.BlockSpec((ns, nb * hp, SSD_STATE), lambda b, c: (b, 0, 0))
    in_specs = [row_spec(SSD_D), row_spec(SSD_CONV_DIM), row_spec(LANES)]
    args = [z, xbc, dtr]
    if not carry:
        in_specs.append(row_spec(SSD_CONV_DIM))
        args.append(halo)
    in_specs += [_full((SUBLANES, SSD_CONV_DIM)), _full((1, SSD_CONV_DIM)), _full((1, LANES)), _full((1, LANES)),
                 _full((1, SSD_D)), _full((1, SSD_D)), _full((LANES, SSD_HEADS * LANES))]
    args += [p["ssd_conv_w"], p["ssd_conv_b"], p["ssd_dt_bias"], p["ssd_a_log"], p["ssd_d"], p["ssd_norm"],
             p["ssd_rep"]]
    if h0 is not None:
        in_specs.append(pl.BlockSpec((None, ns, nb * hp, SSD_STATE), lambda b, c: (layer, b, 0, 0)))
        args.append(h0)
    scratch = [pltpu.VMEM((ns, SUBLANES, SSD_CONV_DIM), F32)] if carry else []
    return pl.pallas_call(
        functools.partial(_ssd_body, ns=ns, nb=nb, seg=seg, carry=carry, has_state=h0 is not None),
        grid=(n_groups // ns, n_chunks), in_specs=in_specs,
        out_specs=[row_spec(SSD_D), st_spec],
        out_shape=[jax.ShapeDtypeStruct((n_groups, lg, SSD_D), BF16),
                   jax.ShapeDtypeStruct((n_groups, nb * hp, SSD_STATE), F32)],
        scratch_shapes=scratch,
        compiler_params=_cparams("parallel", "arbitrary"), name="ssd",
    )(*args)


def _gla_body(*refs, ns, nb, seg, c, has_state):
    hg3_ref, wg_ref, bg_ref, m_ref, ng_ref = refs[0:5]
    i = 5
    s0_ref = None
    if has_state:
        s0_ref = refs[i]
        i += 1
    o3_ref, stc_ref, st3_ref = refs[i:i + 3]
    r = hg3_ref.shape[1]

    def head_blocks():
        for s in range(ns):
            for b in range(nb):
                for hd in range(GLA_HEADS):
                    yield s, slice(b * GLA_VD + hd * GLA_DV, b * GLA_VD + (hd + 1) * GLA_DV), \
                        slice(hd * GLA_DK, (hd + 1) * GLA_DK)

    @pl.when(pl.program_id(1) == 0)
    def _():
        st3_ref[...] = jnp.zeros_like(st3_ref)
        if has_state:
            for s, rows, lanes in head_blocks():
                st3_ref[s, rows, lanes] = s0_ref[s, rows, :]

    sl_ = seg if nb > 1 else c
    ri = lax.broadcasted_iota(jnp.int32, (r, r), 0)
    ci = lax.broadcasted_iota(jnp.int32, (r, r), 1)
    same = (ri // sl_) == (ci // sl_)
    tril = jnp.logical_and(same, ci <= ri)
    tri = jnp.where(tril, 1.0, 0.0).astype(BF16)
    klane = lax.broadcasted_iota(jnp.int32, (r, GLA_KD), 1) // GLA_DK
    vlane = lax.broadcasted_iota(jnp.int32, (r, GLA_VD), 1) // GLA_DV
    rbv = lax.broadcasted_iota(jnp.int32, (r, GLA_VD), 0) // seg
    blk = (lax.broadcasted_iota(jnp.int32, (GLA_VD, GLA_KD), 0) // GLA_DV
           == lax.broadcasted_iota(jnp.int32, (GLA_VD, GLA_KD), 1) // GLA_DK)

    for s in range(ns):
        hg_ref, o_ref, st_ref = hg3_ref.at[s], o3_ref.at[s], st3_ref.at[s]
        q = hg_ref[:, 0:GLA_KD] * (GLA_DK ** -0.5)
        k = hg_ref[:, GLA_KD:2 * GLA_KD]
        vb = hg_ref[:, 2 * GLA_KD:2 * GLA_KD + GLA_VD].astype(BF16)
        gg = hg_ref[:, 2 * GLA_KD + GLA_VD:2 * GLA_KD + 2 * GLA_VD]
        glr = hg_ref[:, 2 * GLA_KD + 2 * GLA_VD:2 * GLA_KD + 2 * GLA_VD + LANES].astype(BF16)
        gate = _log_sigmoid(_dot(glr, wg_ref[...]) + bg_ref[...]) * (1.0 / GLA_GATE_TAU)
        bc = _dot_sel(tri, gate)
        tot = _seg_total(bc, r // sl_, sl_)
        qt = (q * jnp.exp(bc)).astype(BF16)
        kt = (k * jnp.exp(-bc)).astype(BF16)
        kd = (k * jnp.exp(tot - bc)).astype(BF16)
        etot = jnp.exp(tot)
        o = jnp.zeros((r, GLA_VD), F32)
        for hd in range(GLA_HEADS):
            a = _dot_nt(jnp.where(klane == hd, qt, jnp.zeros_like(qt)), kt)
            a = jnp.where(tril, a, 0.0).astype(BF16)
            o = o + _dot(a, jnp.where(vlane == hd, vb, jnp.zeros_like(vb)))
        if nb > 1:
            zs = _dot_nt(qt, st_ref[...].astype(BF16))
            ost = jnp.zeros((r, GLA_VD), F32)
            for b in range(nb):
                ost = jnp.where(rbv == b, zs[:, b * GLA_VD:(b + 1) * GLA_VD], ost)
            o = o + ost
            vexp = jnp.concatenate([jnp.where(rbv == b, vb, jnp.zeros_like(vb)) for b in range(nb)], axis=1)
            ds = _dot_tn(vexp, kd)
            for b in range(nb):
                sl = slice(b * GLA_VD, (b + 1) * GLA_VD)
                st_ref[sl, :] = st_ref[sl, :] * etot[b * seg:b * seg + 1, :] + jnp.where(blk, ds[sl, :], 0.0)
        else:
            st = st_ref[...]
            parts = []
            for sub in range(r // c):
                rs = slice(sub * c, (sub + 1) * c)
                parts.append(_dot_nt(qt[rs, :], st.astype(BF16)))
                st = st * etot[sub * c:sub * c + 1, :] + jnp.where(blk, _dot_tn(vb[rs, :], kd[rs, :]), 0.0)
            st_ref[...] = st
            o = o + jnp.concatenate(parts, axis=0)
        msq = _dot((o * o).astype(BF16), m_ref[...])
        on = o * lax.rsqrt(msq + EPS) * ng_ref[...]
        o_ref[...] = (on * _silu(gg)).astype(o_ref.dtype)

    @pl.when(pl.program_id(1) == pl.num_programs(1) - 1)
    def _():
        for s, rows, lanes in head_blocks():
            stc_ref[s, rows, :] = st3_ref[s, rows, lanes]


def gla(hg, s0, p, nb, seg, c, r, ns):
    n_groups, lg, width = hg.shape
    n_steps = lg // r
    assert lg % r == 0 and r % c == 0 and n_groups % ns == 0
    row_spec = lambda n: pl.BlockSpec((ns, r, n), lambda b, s: (b, s, 0))
    st_spec = pl.BlockSpec((ns, nb * GLA_VD, GLA_DK), lambda b, s: (b, 0, 0))
    in_specs = [row_spec(width), _full((LANES, GLA_KD)), _full((1, GLA_KD)), _full((GLA_VD, GLA_VD)),
                _full((1, GLA_VD))]
    args = [hg, p["gla_w_gate"], p["gla_b_gate"], p["m64"], p["gla_norm"]]
    if s0 is not None:
        in_specs.append(st_spec)
        args.append(s0)
    return pl.pallas_call(
        functools.partial(_gla_body, ns=ns, nb=nb, seg=seg, c=c, has_state=s0 is not None),
        grid=(n_groups // ns, n_steps), in_specs=in_specs,
        out_specs=[row_spec(GLA_VD), st_spec],
        out_shape=[jax.ShapeDtypeStruct((n_groups, lg, GLA_VD), BF16),
                   jax.ShapeDtypeStruct((n_groups, nb * GLA_VD, GLA_DK), F32)],
        scratch_shapes=[pltpu.VMEM((ns, nb * GLA_VD, GLA_KD), F32)],
        compiler_params=_cparams("parallel", "arbitrary"), name="gla",
    )(*args)


def _q_absorb_body(q_ref, wt_ref, sel_ref, qa_out, qr_out):
    for hd in range(MLA_HEADS):
        qh = q_ref[:, hd * LANES:(hd + 1) * LANES].astype(F32)
        qa_out[:, hd * LANES:(hd + 1) * LANES] = _dot(qh, wt_ref[hd], HI).astype(BF16)
        qr_out[:, hd * LANES:(hd + 1) * LANES] = _dot(qh, sel_ref[...], HI).astype(BF16)


def q_absorb(q_cat, p, tm):
    rows, hw = q_cat.shape
    row_spec = pl.BlockSpec((tm, hw), lambda i: (i, 0))
    return pl.pallas_call(
        _q_absorb_body, grid=(rows // tm,),
        in_specs=[row_spec, _full((MLA_HEADS, LANES, LANES)), _full((LANES, LANES))],
        out_specs=[row_spec, row_spec],
        out_shape=[jax.ShapeDtypeStruct((rows, hw), BF16)] * 2,
        compiler_params=_cparams("parallel"), name="q_absorb",
    )(q_cat, p["wukt_g"], p["rope_sel"])


PAGED_CP = 64
PAGED_SUB = 512


def _mla_paged_body(pt_ref, ptn_ref, wukt_ref, qa_ref, qr_ref, cnew_ref, krnew_ref, wuv_ref, lat_hbm, kr_hbm,
                    o_ref, lat_buf, kr_buf, sem, m_ref, l_ref, acc_ref, *, layer, n_pages, n_batch, seq, cp):
    bi = pl.program_id(0)
    nc = n_pages // cp
    nq = MLA_HEADS * seq
    nk = MLA_HEADS * MLA_NOPE

    def copies(table, chunk, slot):
        out = []
        for pg in range(cp):
            pid = table[0, 0, chunk * cp + pg]
            dst = pl.ds(pg * PAGE_SIZE, PAGE_SIZE)
            out.append(pltpu.make_async_copy(lat_hbm.at[layer, pid], lat_buf.at[slot, dst, :], sem.at[0, slot]))
            out.append(pltpu.make_async_copy(kr_hbm.at[layer, pid], kr_buf.at[slot, :, dst], sem.at[1, slot]))
        return out

    def start(table, chunk, slot):
        for cpy in copies(table, chunk, slot):
            cpy.start()

    def wait(slot):
        for cpy in copies(pt_ref, 0, slot):
            cpy.wait()

    @pl.when(bi == 0)
    def _():
        start(pt_ref, 0, 0)

    m_ref[...] = jnp.full(m_ref.shape, -jnp.inf, F32)
    l_ref[...] = jnp.zeros(l_ref.shape, F32)
    acc_ref[...] = jnp.zeros(acc_ref.shape, F32)
    lhs = jnp.concatenate([wukt_ref[...], qa_ref[0]], axis=0)
    qr = qr_ref[0]

    def scores(latb, krt):
        big = _dot_nt(lhs, latb)
        kt = big[0:nk, :]
        ss = jnp.sum((kt * kt).reshape(MLA_HEADS, MLA_NOPE, kt.shape[1]), axis=1) * (1.0 / MLA_NOPE)
        rinv = lax.rsqrt(ss + EPS)
        rexp = jnp.concatenate([jnp.broadcast_to(rinv[hd:hd + 1, :], (seq, rinv.shape[1]))
                                for hd in range(MLA_HEADS)], axis=0)
        return big[nk:, :] * rexp + _dot(qr, krt.astype(BF16))

    def attend(lat, krt, mask):
        keys = lat.shape[0]
        sub = min(PAGED_SUB, keys)
        latb = lat.astype(BF16)
        s = jnp.concatenate([scores(latb[i * sub:(i + 1) * sub, :], krt[:, i * sub:(i + 1) * sub])
                             for i in range(keys // sub)], axis=1)
        if mask is not None:
            s = jnp.where(mask, s, -jnp.inf)
        m_old = m_ref[...]
        m_new = jnp.maximum(m_old, jnp.max(s, axis=-1, keepdims=True))
        pr = jnp.exp2(s - jnp.concatenate([m_new] * (keys // LANES), axis=1))
        corr = jnp.exp2(m_old - m_new)
        l_ref[...] = l_ref[...] * corr + jnp.sum(pr, axis=-1, keepdims=True)
        acc_ref[...] = acc_ref[...] * corr + _dot(pr.astype(BF16), latb)
        m_ref[...] = m_new

    lat_new = jnp.concatenate([cnew_ref[0], jnp.zeros((LANES - seq, MLA_KV_RANK), F32)], axis=0)
    qtok = lax.broadcasted_iota(jnp.int32, (nq, LANES), 0) % seq
    key = lax.broadcasted_iota(jnp.int32, (nq, LANES), 1)
    attend(lat_new, krnew_ref[0], key <= qtok)

    def step(slot, prefetch):
        prefetch()
        wait(slot)
        attend(lat_buf[slot], kr_buf[slot], None)

    def pair_body(jj, carry_):
        step(0, lambda: start(pt_ref, 2 * jj + 1, 1))
        step(1, lambda: start(pt_ref, 2 * jj + 2, 0))
        return carry_

    lax.fori_loop(0, nc // 2 - 1, pair_body, 0)
    step(0, lambda: start(pt_ref, nc - 1, 1))

    def next_batch_prefetch():
        @pl.when(bi + 1 < n_batch)
        def _():
            start(ptn_ref, 0, 0)

    step(1, next_batch_prefetch)

    olat = (acc_ref[...] / l_ref[...]).astype(BF16)
    o = jnp.zeros((seq, MLA_HEADS * MLA_V), F32)
    for hd in range(MLA_HEADS):
        o = o + _dot(olat[hd * seq:(hd + 1) * seq, :], wuv_ref[hd * LANES:(hd + 1) * LANES, :])
    o_ref[0] = o.astype(o_ref.dtype)


def mla_paged(page_table, wukt, qa, qr, c_new, krt_new, wuv_bd, cache_lat, cache_krt, layer):
    n_batch, n_pages = page_table.shape
    seq = c_new.shape[1]
    cp = min(PAGED_CP, n_pages // 2)
    assert n_pages % (2 * cp) == 0
    nq = MLA_HEADS * seq
    pt3 = page_table.reshape(n_batch, 1, n_pages)
    smem_spec = lambda f: pl.BlockSpec((1, 1, n_pages), f, memory_space=pltpu.SMEM)
    per_b = lambda shp: pl.BlockSpec((1,) + shp, lambda b: (b, 0, 0))
    kc = cp * PAGE_SIZE
    kr_new, cache_kr = krt_new, cache_krt
    return pl.pallas_call(
        functools.partial(_mla_paged_body, layer=layer, n_pages=n_pages, n_batch=n_batch, seq=seq, cp=cp),
        grid=(n_batch,),
        in_specs=[smem_spec(lambda b: (b, 0, 0)),
                  smem_spec(lambda b: (jnp.minimum(b + 1, n_batch - 1), 0, 0)),
                  _full(wukt.shape), per_b(qa.shape[1:]), per_b(qr.shape[1:]), per_b(c_new.shape[1:]),
                  per_b(kr_new.shape[1:]),
                  _full(wuv_bd.shape),
                  pl.BlockSpec(memory_space=pl.ANY), pl.BlockSpec(memory_space=pl.ANY)],
        out_specs=per_b((seq, MLA_HEADS * MLA_V)),
        out_shape=jax.ShapeDtypeStruct((n_batch, seq, MLA_HEADS * MLA_V), BF16),
        scratch_shapes=[pltpu.VMEM((2, kc, MLA_KV_RANK), F32), pltpu.VMEM((2, MLA_ROPE, kc), F32),
                        pltpu.SemaphoreType.DMA((2, 2)),
                        pltpu.VMEM((nq, LANES), F32), pltpu.VMEM((nq, LANES), F32), pltpu.VMEM((nq, LANES), F32)],
        compiler_params=_cparams("arbitrary"), name="mla_paged",
    )(pt3, pt3, wukt, qa, qr, c_new, kr_new, wuv_bd, cache_lat, cache_kr)


def _block_diag_mean(n, blocks):
    idx = jnp.arange(n)
    m = jnp.zeros((n, n), F32)
    for start, size in blocks:
        inb = jnp.logical_and(idx >= start, idx < start + size)
        m = m + jnp.where(jnp.logical_and(inb[:, None], inb[None, :]), 1.0 / size, 0.0)
    return m.astype(BF16)


def _rope_tables(pos):
    half = MLA_ROPE // 2
    inv = ROPE_THETA ** (-jnp.arange(half, dtype=F32) / half)
    ang = pos.astype(F32)[:, None] * inv[None, :]
    cos, sin = jnp.cos(ang), jnp.sin(ang)
    n = pos.shape[0]
    one, zero = jnp.ones((n, MLA_NOPE), F32), jnp.zeros((n, MLA_NOPE), F32)
    tail1, tail0 = jnp.ones((n, LANES - MLA_QK), F32), jnp.zeros((n, LANES - MLA_QK), F32)
    return (jnp.concatenate([one, cos, cos, tail1], axis=1), jnp.concatenate([zero, -sin, sin, tail0], axis=1))


def _pack_layer(l, w):
    f = lambda name: w[name][l]
    bf = lambda a: a.astype(BF16)
    p = {}
    w_in = f("w_in")
    o = 0
    cols = {}
    for name, size in (("cq", MLA_Q_RANK), ("ckv", MLA_KV_RANK), ("krr", MLA_ROPE), ("z", SSD_D), ("xbc", SSD_CONV_DIM),
                       ("dt", SSD_HEADS), ("gq", GLA_KD), ("gk", GLA_KD), ("gv", GLA_VD), ("glr", GLA_GATE_RANK),
                       ("gg", GLA_VD)):
        cols[name] = w_in[:, o:o + size]
        o += size
    d = w_in.shape[0]
    krr_pad = jnp.zeros((d, LANES), F32).at[:, MLA_NOPE:MLA_QK].set(cols["krr"])
    glr_pad = jnp.zeros((d, LANES), F32).at[:, :GLA_GATE_RANK].set(cols["glr"])
    p["w_mla"] = bf(jnp.concatenate([cols["cq"], cols["ckv"], krr_pad], axis=1))
    p["w_z"] = bf(cols["z"])
    p["w_xbc"] = bf(cols["xbc"])
    p["w_dt"] = bf(jnp.zeros((d, LANES), F32).at[:, :SSD_HEADS].set(cols["dt"]))
    p["w_gla"] = bf(jnp.concatenate([cols["gq"], cols["gk"], cols["gv"], cols["gg"], glr_pad], axis=1))
    p["norm_mix"] = f("norm_mix")

    def head_pad(wm, n_real):
        k = wm.shape[0]
        out = jnp.zeros((k, MLA_HEADS, LANES), F32).at[:, :, :n_real].set(wm)
        return out.reshape(k, MLA_HEADS * LANES)

    p["wq_pad"] = bf(head_pad(f("mla_w_uq"), MLA_QK))
    p["wk_pad"] = bf(head_pad(f("mla_w_uk"), MLA_NOPE))
    scale = MLA_QK ** -0.5 * math.log2(math.e)
    zpad = jnp.zeros((LANES - MLA_QK,), F32)
    p["gq_vec"] = (jnp.concatenate([f("mla_qn_norm"), f("mla_qr_norm"), zpad]) * scale).reshape(1, LANES)
    p["gk_vec"] = jnp.concatenate([f("mla_kn_norm"), jnp.zeros((LANES - MLA_NOPE,), F32)]).reshape(1, LANES)
    p["gkr_vec"] = jnp.concatenate([jnp.zeros((MLA_NOPE,), F32), f("mla_kr_norm"), zpad]).reshape(1, LANES)
    p["mla_q_norm"] = f("mla_q_norm").reshape(1, MLA_Q_RANK)
    p["mla_kv_norm"] = f("mla_kv_norm").reshape(1, MLA_KV_RANK)
    p["mh"] = _block_diag_mean(LANES, [(0, MLA_NOPE), (MLA_NOPE, MLA_ROPE)])
    w_uv = f("mla_w_uv")
    eye = jnp.eye(MLA_HEADS, dtype=F32)
    p["wuv_bd"] = bf(jnp.einsum("chv,hg->hcgv", w_uv, eye).reshape(MLA_HEADS * LANES, MLA_HEADS * MLA_V))
    w_uk = f("mla_w_uk")
    wt = jnp.transpose(w_uk, (1, 2, 0)) * f("mla_kn_norm")[None, :, None]
    p["wukt_g"] = jnp.zeros((MLA_HEADS, LANES, LANES), F32).at[:, :MLA_NOPE, :].set(wt)
    p["wukt"] = bf(jnp.transpose(w_uk, (1, 2, 0)).reshape(MLA_HEADS * MLA_NOPE, MLA_KV_RANK))
    p["rope_sel"] = jnp.zeros((LANES, LANES), F32).at[MLA_NOPE + jnp.arange(MLA_ROPE), jnp.arange(MLA_ROPE)].set(1.0)

    p["ssd_conv_w"] = jnp.zeros((SUBLANES, SSD_CONV_DIM), F32).at[:SSD_CONV].set(f("ssd_conv_w"))
    p["ssd_conv_b"] = f("ssd_conv_b").reshape(1, SSD_CONV_DIM)
    lane_pad = lambda v: jnp.zeros((1, LANES), F32).at[0, :v.shape[0]].set(v)
    p["ssd_dt_bias"] = lane_pad(f("ssd_dt_bias"))
    p["ssd_a_log"] = lane_pad(f("ssd_a_log"))
    p["ssd_rep"] = bf(jnp.repeat(jnp.eye(LANES, SSD_HEADS, dtype=F32), LANES, axis=1))
    p["ssd_d"] = jnp.repeat(f("ssd_d"), SSD_HEADDIM).reshape(1, SSD_D)
    p["ssd_norm"] = f("ssd_norm").reshape(1, SSD_D)
    p["gla_w_gate"] = bf(jnp.zeros((LANES, GLA_KD), F32).at[:GLA_GATE_RANK].set(f("gla_w_gate")))
    p["gla_b_gate"] = f("gla_b_gate").reshape(1, GLA_KD)
    p["gla_norm"] = f("gla_norm").reshape(1, GLA_VD)
    p["m64"] = _block_diag_mean(MEM_D, [(i * 64, 64) for i in range(4)])
    w_o = f("w_o")
    p["wo_mla"] = bf(w_o[:MLA_HEADS * MLA_V])
    p["wo_ssd"] = bf(w_o[MLA_HEADS * MLA_V:MLA_HEADS * MLA_V + SSD_D])
    p["wo_gla"] = bf(w_o[MLA_HEADS * MLA_V + SSD_D:])
    p["norm_mem"] = f("norm_mem").reshape(1, D_MODEL)
    p["mem_wq"], p["mem_wk"], p["mem_wv"], p["mem_wo"] = bf(f("mem_wq")), bf(f("mem_wk")), bf(f("mem_wv")), bf(f("mem_wo"))
    p["mem_q_gain"] = (jnp.tile(f("mem_q_norm"), MEM_HEADS) * MEM_HEAD_DIM ** -0.5).reshape(1, MEM_D)
    p["mem_k_gain"] = jnp.tile(f("mem_k_norm"), MEM_HEADS).reshape(1, MEM_D)
    p["norm_ffn"] = f("norm_ffn").reshape(1, D_MODEL)
    w_up = f("ffn_w_up")
    p["ffn_wu"], p["ffn_wv"], p["ffn_wd"] = bf(w_up[:, :D_FF]), bf(w_up[:, D_FF:]), bf(f("ffn_w_down"))
    p["ffn_conv_w"] = jnp.zeros((SUBLANES, D_FF), F32).at[:FFN_CONV].set(f("ffn_conv_w"))
    p["ffn_conv_b"] = f("ffn_conv_b").reshape(1, D_FF)
    return p


def _pad_rows(state, seq):
    b, k, c = state.shape
    return jnp.pad(state, ((0, 0), (0, seq - k), (0, 0))).reshape(b * seq, c)


def _layer(x3, p, cos_t, sin_t, tm, sample):
    b, L, d = x3.shape
    rows = b * L
    x = x3.reshape(rows, d)
    h_mla, z, xbc, dtr, h_gla = norm_matmul(
        x, p["norm_mix"], [p["w_mla"], p["w_z"], p["w_xbc"], p["w_dt"], p["w_gla"]], [F32] * 5, tm)
    q_cat, k_cat, c_kv, c_bf, kr_vec = mla_prep(h_mla, cos_t, sin_t, p, tm)
    kr = kr_vec[:, MLA_NOPE:MLA_QK]
    hw = MLA_HEADS * LANES
    if sample is None:
        o_mla = mla_prompt(q_cat.reshape(b, L, hw), k_cat.reshape(b, L, hw), c_bf.reshape(b, L, 2 * LANES),
                           p["wuv_bd"])
        o_mla = o_mla.reshape(rows, MLA_HEADS * MLA_V)
        g3 = lambda a: a.reshape(b, L, a.shape[-1])
        ns = math.gcd(b, 4)
        o_ssd, ssd_h = ssd(g3(z), g3(xbc), g3(dtr), None, None, p, 1, L, ns)
        o_gla, gla_st = gla(g3(h_gla), None, p, 1, L, GLA_CHUNK, 256, ns)
    else:
        qa, qrp = q_absorb(q_cat, p, tm)
        to_hq = lambda a, n: a.reshape(b, L, MLA_HEADS, LANES)[..., :n].transpose(0, 2, 1, 3).reshape(b, MLA_HEADS * L, n)
        krt_new = jnp.pad(jnp.swapaxes(kr.reshape(b, L, MLA_ROPE), 1, 2), ((0, 0), (0, 0), (0, LANES - L)))
        o_mla = mla_paged(sample["page_table"], p["wukt"], to_hq(qa, LANES), to_hq(qrp, MLA_ROPE),
                          c_kv.reshape(b, L, LANES), krt_new, p["wuv_bd"], sample["cache_lat"], sample["cache_krt"],
                          sample["layer"])
        o_mla = o_mla.reshape(rows, MLA_HEADS * MLA_V)
        nb = SSD_CHUNK // L
        ng = b // nb
        g3 = lambda a: a.reshape(ng, nb * L, a.shape[-1])
        ns = math.gcd(ng, 2)
        o_ssd, ssd_h = ssd(g3(z), g3(xbc), g3(dtr), g3(_pad_rows(sample["ssd_conv"], L)),
                           sample["ssd_all"].reshape(-1, ng, nb * SSD_D, SSD_STATE), p, nb, L, ns, sample["layer"])
        st0 = jnp.swapaxes(sample["gla"], 2, 3).reshape(ng, nb * GLA_VD, GLA_DK)
        o_gla, gla_st = gla(g3(h_gla), st0, p, nb, L, nb * L, nb * L, ns)
    o_ssd, o_gla = o_ssd.reshape(rows, SSD_D), o_gla.reshape(rows, GLA_VD)
    x = residual_matmul(x, [o_mla, o_ssd, o_gla], [p["wo_mla"], p["wo_ssd"], p["wo_gla"]], tm)

    if sample is None:
        mk, mv = mem_kv(p["mem_prompt"], p, 512)
        x = mem_attend(x, mk, mv, p, 512, 1, L, False)
    else:
        nbm = 128 // L
        x = mem_attend(x, sample["mem_kt"], sample["mem_vt"], p, nbm * L, nbm, L, True, sample["layer"])

    ft = min(rows, FFN_ROWS)
    if sample is None:
        x, u_tail = ffn(x, None, p, ft, True, L // ft, L, SUBLANES)
        ffn_conv = u_tail.reshape(b, L // ft, SUBLANES, D_FF)[:, -1, SUBLANES - (FFN_CONV - 1):, :]
    else:
        x, u_full = ffn(x, _pad_rows(sample["ffn_conv"], L), p, ft, False, 1, L, ft)
        ffn_conv = u_full.reshape(b, L, D_FF)[:, L - (FFN_CONV - 1):, :]

    xbc3 = xbc.reshape(b, L, SSD_CONV_DIM)
    out = dict(
        x=x.reshape(b, L, d), lat=c_kv.reshape(b, L, MLA_KV_RANK), kr=kr.reshape(b, L, MLA_ROPE),
        ssd_conv=xbc3[:, L - (SSD_CONV - 1):, :], ssd=ssd_h.reshape(b, SSD_HEADS, SSD_HEADDIM, SSD_STATE),
        gla=jnp.swapaxes(gla_st.reshape(b, GLA_HEADS, GLA_DV, GLA_DK), 2, 3),
        ffn_conv=ffn_conv)
    if sample is None:
        out["mem_k"] = mk.reshape(b, N_MEM, MEM_HEADS, MEM_HEAD_DIM)
        out["mem_v"] = mv.reshape(b, N_MEM, MEM_HEADS, MEM_HEAD_DIM)
    return out


def kernel(x_prompt, x_sample, cache_mla_latent, cache_mla_krope, cache_mem_k, cache_mem_v, state_ssd_conv, state_ssd, state_gla, state_ffn_conv, page_table, mem_prompt, norm_mix, w_in, mla_q_norm, mla_w_uq, mla_kv_norm, mla_w_uk, mla_w_uv, mla_qn_norm, mla_qr_norm, mla_kn_norm, mla_kr_norm, ssd_conv_w, ssd_conv_b, ssd_dt_bias, ssd_a_log, ssd_d, ssd_norm, gla_w_gate, gla_b_gate, gla_norm, w_o, norm_mem, mem_wq, mem_wk, mem_wv, mem_wo, mem_q_norm, mem_k_norm, norm_ffn, ffn_w_up, ffn_conv_w, ffn_conv_b, ffn_w_down):
    weights = dict(
        norm_mix=norm_mix, w_in=w_in, mla_q_norm=mla_q_norm, mla_w_uq=mla_w_uq, mla_kv_norm=mla_kv_norm,
        mla_w_uk=mla_w_uk, mla_w_uv=mla_w_uv, mla_qn_norm=mla_qn_norm, mla_qr_norm=mla_qr_norm,
        mla_kn_norm=mla_kn_norm, mla_kr_norm=mla_kr_norm, ssd_conv_w=ssd_conv_w, ssd_conv_b=ssd_conv_b,
        ssd_dt_bias=ssd_dt_bias, ssd_a_log=ssd_a_log, ssd_d=ssd_d, ssd_norm=ssd_norm, gla_w_gate=gla_w_gate,
        gla_b_gate=gla_b_gate, gla_norm=gla_norm, w_o=w_o, norm_mem=norm_mem, mem_wq=mem_wq, mem_wk=mem_wk,
        mem_wv=mem_wv, mem_wo=mem_wo, mem_q_norm=mem_q_norm, mem_k_norm=mem_k_norm, norm_ffn=norm_ffn,
        ffn_w_up=ffn_w_up, ffn_conv_w=ffn_conv_w, ffn_conv_b=ffn_conv_b, ffn_w_down=ffn_w_down)
    depth = w_in.shape[0]
    bp, lp, _ = x_prompt.shape
    bs, ls, _ = x_sample.shape
    assert ls >= SSD_CONV - 1 and SSD_CHUNK % ls == 0
    past_len = page_table.shape[1] * PAGE_SIZE
    tm_p, tm_s = 512, min(512, bs * ls)
    cos_p, sin_p = _rope_tables(jnp.arange(lp))
    cos_s, sin_s = _rope_tables(past_len + jnp.arange(ls))
    cos_s, sin_s = jnp.tile(cos_s, (tm_s // ls, 1)), jnp.tile(sin_s, (tm_s // ls, 1))
    mem2d = mem_prompt.reshape(bp * N_MEM, D_MODEL)
    cache_krt = jnp.swapaxes(cache_mla_krope, 2, 3)
    mem_kt = jnp.transpose(cache_mem_k, (0, 1, 3, 4, 2)).reshape(depth, bs, MEM_D, N_MEM)
    mem_vt = jnp.transpose(cache_mem_v, (0, 1, 3, 4, 2)).reshape(depth, bs, MEM_D, N_MEM)

    xp, xs = x_prompt, x_sample
    outs_p, outs_s = [], []
    for l in range(depth):
        p = _pack_layer(l, weights)
        p["mem_prompt"] = mem2d
        op = _layer(xp, p, cos_p, sin_p, tm_p, None)
        xp = op["x"]
        outs_p.append(op)
        sample = dict(layer=l, page_table=page_table, cache_lat=cache_mla_latent, cache_krt=cache_krt,
                      mem_kt=mem_kt, mem_vt=mem_vt, ssd_conv=state_ssd_conv[l], ssd_all=state_ssd,
                      gla=state_gla[l], ffn_conv=state_ffn_conv[l])
        os_ = _layer(xs, p, cos_s, sin_s, tm_s, sample)
        xs = os_["x"]
        outs_s.append(os_)

    st = lambda outs, key: jnp.stack([o[key] for o in outs])
    return (xp, xs,
            st(outs_p, "lat"), st(outs_p, "kr"), st(outs_p, "mem_k"), st(outs_p, "mem_v"),
            st(outs_p, "ssd_conv"), st(outs_p, "ssd"), st(outs_p, "gla"), st(outs_p, "ffn_conv"),
            st(outs_s, "lat"), st(outs_s, "kr"), st(outs_s, "ssd_conv"), st(outs_s, "ssd"), st(outs_s, "gla"),
            st(outs_s, "ffn_conv"))
```

```python
import functools
import math

import jax
import jax.numpy as jnp
from jax import lax
from jax.experimental import pallas as pl
from jax.experimental.pallas import tpu as pltpu

F32 = jnp.float32
BF16 = jnp.bfloat16
EPS = 1e-6
LANES = 128
SUBLANES = 8
VMEM_LIMIT = 56 * 1024 * 1024

D_MODEL = 1024
MLA_HEADS, MLA_NOPE, MLA_ROPE, MLA_V = 8, 64, 32, 64
MLA_QK = MLA_NOPE + MLA_ROPE
MLA_Q_RANK, MLA_KV_RANK = 256, 128
ROPE_THETA = 10000.0
PAGE_SIZE = 128
SSD_HEADS, SSD_HEADDIM, SSD_GROUPS, SSD_STATE, SSD_CONV = 4, 64, 2, 128, 4
SSD_D = SSD_HEADS * SSD_HEADDIM
SSD_CONV_DIM = SSD_D + 2 * SSD_GROUPS * SSD_STATE
SSD_CHUNK = 128
GLA_HEADS, GLA_DK, GLA_DV = 4, 32, 64
GLA_KD, GLA_VD = GLA_HEADS * GLA_DK, GLA_HEADS * GLA_DV
GLA_GATE_RANK, GLA_GATE_TAU, GLA_CHUNK = 16, 16.0, 64
N_MEM, MEM_HEADS, MEM_HEAD_DIM = 256, 4, 64
MEM_D = MEM_HEADS * MEM_HEAD_DIM
D_FF, FFN_CONV = 2816, 3
FFN_CHUNK = 256
FFN_ROWS = 256

HI = lax.Precision.HIGHEST


def _cparams(*sem):
    return pltpu.CompilerParams(dimension_semantics=sem, vmem_limit_bytes=VMEM_LIMIT)


def _dot(a, b, precision=None):
    return jnp.dot(a, b, preferred_element_type=F32, precision=precision)


def _dot_nt(a, b):
    return lax.dot_general(a, b, (((1,), (1,)), ((), ())), preferred_element_type=F32)


def _dot_tn(a, b):
    return lax.dot_general(a, b, (((0,), (0,)), ((), ())), preferred_element_type=F32)


def _split3(x):
    x1 = x.astype(BF16)
    r1 = x - x1.astype(F32)
    x2 = r1.astype(BF16)
    x3 = (r1 - x2.astype(F32)).astype(BF16)
    return x1, x2, x3


def _dot_sel(sel, x):
    return sum(_dot(sel, t) for t in _split3(x))


def _dot_rep(x, sel):
    return sum(_dot(t, sel) for t in _split3(x))


def _seg_total(cs, n_seg, seg_len):
    return jnp.concatenate(
        [jnp.broadcast_to(cs[(i + 1) * seg_len - 1:(i + 1) * seg_len, :], (seg_len, cs.shape[1]))
         for i in range(n_seg)], axis=0)


def _rms(x, g):
    return x * lax.rsqrt(jnp.mean(x * x, axis=-1, keepdims=True) + EPS) * g


def _sigmoid(x):
    return 1.0 / (1.0 + jnp.exp(-x))


def _silu(x):
    return x * _sigmoid(x)


def _softplus(x):
    return jnp.maximum(x, 0.0) + jnp.log1p(jnp.exp(-jnp.abs(x)))


def _log_sigmoid(x):
    return jnp.minimum(x, 0.0) - jnp.log1p(jnp.exp(-jnp.abs(x)))


class _Stacked:
    def __init__(self, arr, layer):
        self.arr, self.layer = arr, layer

    @property
    def shape(self):
        return self.arr.shape[1:]


def _wspec(w, resident=False):
    shp, layer = w.shape, w.layer
    kw = dict(pipeline_mode=pl.Buffered(1)) if resident else {}
    return pl.BlockSpec((None,) + shp, lambda *_: (layer,) + (0,) * len(shp), **kw)


def _norm_mm_body(x_ref, g_ref, *refs, n_out, has_norm):
    w_refs, o_refs = refs[:n_out], refs[n_out:]
    x = x_ref[...]
    if has_norm:
        x = _rms(x, g_ref[...])
    xb = x.astype(BF16)
    for w_ref, o_ref in zip(w_refs, o_refs):
        o_ref[...] = _dot(xb, w_ref[...]).astype(o_ref.dtype)


def norm_matmul(x, g, ws, out_dtypes, tm, has_norm=True):
    rows, k = x.shape
    assert rows % tm == 0
    in_specs = [pl.BlockSpec((tm, k), lambda i: (i, 0)), _wspec(g)]
    in_specs += [_wspec(w) for w in ws]
    out_specs = [pl.BlockSpec((tm, w.shape[1]), lambda i: (i, 0)) for w in ws]
    out_shape = [jax.ShapeDtypeStruct((rows, w.shape[1]), dt) for w, dt in zip(ws, out_dtypes)]
    return pl.pallas_call(
        functools.partial(_norm_mm_body, n_out=len(ws), has_norm=has_norm),
        grid=(rows // tm,), in_specs=in_specs, out_specs=out_specs, out_shape=out_shape,
        compiler_params=_cparams("parallel"), name="norm_matmul",
    )(x, g.arr, *[w.arr for w in ws])


def _mla_prep_body(h_ref, cos_ref, sin_ref, qng_ref, wq_ref, gq_ref, kvg_ref, wk_ref, gk_ref, krg_ref, mh_ref,
                   q_out, k_out, ckv_out, cbf_out, kr_out):
    tm = h_ref.shape[0]
    h = h_ref[...]
    cq, ckv, krr = h[:, 0:256], h[:, 256:384], h[:, 384:512]
    cos, sin = cos_ref[...], sin_ref[...]
    lane = lax.broadcasted_iota(jnp.int32, (tm, LANES), 1)
    mh = mh_ref[...]

    def rope(xv):
        rot = jnp.where(lane < MLA_NOPE + MLA_ROPE // 2,
                        pltpu.roll(xv, LANES - MLA_ROPE // 2, 1), pltpu.roll(xv, MLA_ROPE // 2, 1))
        return xv * cos + rot * sin

    def seg_norm(xv, g):
        msq = _dot((xv * xv).astype(BF16), mh)
        return xv * lax.rsqrt(msq + EPS) * g

    q = _dot(_rms(cq, qng_ref[...]).astype(BF16), wq_ref[...])
    c = _rms(ckv, kvg_ref[...])
    cb = c.astype(BF16)
    ckv_out[...] = c
    cbf_out[...] = jnp.concatenate([cb, jnp.ones_like(cb)], axis=1)
    kr = rope(seg_norm(krr, krg_ref[...]))
    kr_out[...] = kr
    k = _dot(cb, wk_ref[...])
    gq, gk = gq_ref[...], gk_ref[...]
    for hd in range(MLA_HEADS):
        sl = slice(hd * LANES, (hd + 1) * LANES)
        q_out[:, sl] = rope(seg_norm(q[:, sl], gq)).astype(BF16)
        k_out[:, sl] = (seg_norm(k[:, sl], gk) + kr).astype(BF16)


def mla_prep(h_mla, cos_t, sin_t, p, tm):
    rows = h_mla.shape[0]
    nt = cos_t.shape[0] // tm
    row_spec = lambda n: pl.BlockSpec((tm, n), lambda i: (i, 0))
    tab_spec = pl.BlockSpec((tm, LANES), lambda i: (i % nt, 0))
    hw = MLA_HEADS * LANES
    names = ("mla_q_norm", "wq_pad", "gq_vec", "mla_kv_norm", "wk_pad", "gk_vec", "gkr_vec", "mh")
    return pl.pallas_call(
        _mla_prep_body, grid=(rows // tm,),
        in_specs=[row_spec(512), tab_spec, tab_spec] + [_wspec(p[n]) for n in names],
        out_specs=[row_spec(hw), row_spec(hw), row_spec(LANES), row_spec(2 * LANES), row_spec(LANES)],
        out_shape=[jax.ShapeDtypeStruct((rows, hw), BF16), jax.ShapeDtypeStruct((rows, hw), BF16),
                   jax.ShapeDtypeStruct((rows, LANES), F32), jax.ShapeDtypeStruct((rows, 2 * LANES), BF16),
                   jax.ShapeDtypeStruct((rows, LANES), F32)],
        compiler_params=_cparams("parallel"), name="mla_prep",
    )(h_mla, cos_t, sin_t, *[p[n].arr for n in names])


def _mla_prompt_body(q_ref, k_ref, c_ref, wuv_ref, o_ref, m_ref, l_ref, acc_ref, olat_ref, *, tq, tk):
    qi = pl.program_id(1)

    def causal(n_rows):
        return (lax.broadcasted_iota(jnp.int32, (n_rows, tk), 1)
                <= lax.broadcasted_iota(jnp.int32, (n_rows, tk), 0))

    m_ref[...] = jnp.full(m_ref.shape, -jnp.inf, F32)
    l_ref[...] = jnp.zeros(l_ref.shape, F32)
    acc_ref[...] = jnp.zeros(acc_ref.shape, F32)

    def step(j, r0, masked):
        start = pl.multiple_of(j * tk, tk)
        rows = slice(r0, tq)
        cblk = c_ref[0, pl.ds(start, tk), :]
        for hd in range(MLA_HEADS):
            sl = slice(hd * LANES, (hd + 1) * LANES)
            s = _dot_nt(q_ref[0, rows, sl], k_ref[0, pl.ds(start, tk), sl])
            if masked:
                s = jnp.where(causal(tq - r0), s, -jnp.inf)
            m_old = m_ref[hd, rows, :]
            m_new = jnp.maximum(m_old, jnp.max(s, axis=-1, keepdims=True))
            p = jnp.exp2(s - jnp.concatenate([m_new] * (tk // LANES), axis=1))
            corr = jnp.exp2(m_old - m_new)
            pv = _dot(p.astype(BF16), cblk)
            acc_ref[hd, rows, :] = acc_ref[hd, rows, :] * corr + pv[:, :LANES]
            l_ref[hd, rows, :] = l_ref[hd, rows, :] * corr + pv[:, LANES:]
            m_ref[hd, rows, :] = m_new

    def body(j, carry):
        step(j, 0, False)
        return carry

    n_full = qi * (tq // tk)
    lax.fori_loop(0, n_full, body, 0)
    for d in range(tq // tk):
        step(n_full + d, d * tk, True)
    for hd in range(MLA_HEADS):
        olat_ref[:, hd * LANES:(hd + 1) * LANES] = (acc_ref[hd] / l_ref[hd]).astype(BF16)
    o_ref[0] = _dot(olat_ref[...], wuv_ref[...]).astype(o_ref.dtype)


def mla_prompt(q_cat, k_cat, c_bf, wuv_bd, tq=256, tk=256):
    b, L, hw = q_cat.shape
    assert tq % tk == 0 and L % tq == 0
    return pl.pallas_call(
        functools.partial(_mla_prompt_body, tq=tq, tk=tk), grid=(b, L // tq),
        in_specs=[pl.BlockSpec((1, tq, hw), lambda bi, qi: (bi, qi, 0)),
                  pl.BlockSpec((1, L, hw), lambda bi, qi: (bi, 0, 0)),
                  pl.BlockSpec((1, L, 2 * LANES), lambda bi, qi: (bi, 0, 0)),
                  _wspec(wuv_bd)],
        out_specs=pl.BlockSpec((1, tq, MLA_HEADS * MLA_V), lambda bi, qi: (bi, qi, 0)),
        out_shape=jax.ShapeDtypeStruct((b, L, MLA_HEADS * MLA_V), BF16),
        scratch_shapes=[pltpu.VMEM((MLA_HEADS, tq, LANES), F32), pltpu.VMEM((MLA_HEADS, tq, LANES), F32),
                        pltpu.VMEM((MLA_HEADS, tq, LANES), F32), pltpu.VMEM((tq, hw), BF16)],
        compiler_params=_cparams("parallel", "arbitrary"), name="mla_prompt",
    )(q_cat, k_cat, c_bf, wuv_bd.arr)


def _mem_kv_body(x_ref, wk_ref, wv_ref, m_ref, g_ref, k_out, v_out):
    xb = x_ref[...].astype(BF16)
    k = _dot(xb, wk_ref[...])
    msq = _dot((k * k).astype(BF16), m_ref[...])
    k_out[...] = k * lax.rsqrt(msq + EPS) * g_ref[...]
    v_out[...] = _dot(xb, wv_ref[...])


def mem_kv(mem2d, p, tm):
    rows, d = mem2d.shape
    row = lambda n: pl.BlockSpec((tm, n), lambda i: (i, 0))
    names = ("mem_wk", "mem_wv", "m64", "mem_k_gain")
    return pl.pallas_call(
        _mem_kv_body, grid=(rows // tm,),
        in_specs=[row(d)] + [_wspec(p[n]) for n in names],
        out_specs=[row(MEM_D), row(MEM_D)],
        out_shape=[jax.ShapeDtypeStruct((rows, MEM_D), F32)] * 2,
        compiler_params=_cparams("parallel"), name="mem_kv",
    )(mem2d, *[p[n].arr for n in names])


def _mem_attend_body(x_ref, *refs, nb, kv_t, n_pre):
    a_refs, w_refs = refs[:n_pre], refs[n_pre:2 * n_pre]
    g_ref, wq_ref, m_ref, gq_ref, k_ref, v_ref, wo_ref, o_ref = refs[2 * n_pre:]
    r = x_ref.shape[0]
    x = x_ref[...]
    for a_ref, w_ref in zip(a_refs, w_refs):
        x = x + _dot(a_ref[...], w_ref[...])
    q = _dot(_rms(x, g_ref[...]).astype(BF16), wq_ref[...])
    msq = _dot((q * q).astype(BF16), m_ref[...])
    qn = (q * lax.rsqrt(msq + EPS) * gq_ref[...]).astype(BF16)
    if kv_t:
        kb = jnp.concatenate([k_ref[b] for b in range(nb)], axis=1).astype(BF16)
        vb = jnp.concatenate([v_ref[b] for b in range(nb)], axis=1).astype(BF16)
    else:
        kb = k_ref[...].astype(BF16)
        vb = v_ref[...].astype(BF16)
    lane = lax.broadcasted_iota(jnp.int32, (r, MEM_D), 1) // MEM_HEAD_DIM
    if nb > 1:
        rb = lax.broadcasted_iota(jnp.int32, (r, nb * N_MEM), 0) // (r // nb)
        cb = lax.broadcasted_iota(jnp.int32, (r, nb * N_MEM), 1) // N_MEM
        same = rb == cb
    o = jnp.zeros((r, MEM_D), F32)
    for hd in range(MEM_HEADS):
        qm = jnp.where(lane == hd, qn, jnp.zeros_like(qn))
        s = _dot(qm, kb) if kv_t else _dot_nt(qm, kb)
        if nb > 1:
            s = jnp.where(same, s, -jnp.inf)
        pr = jnp.exp(s - jnp.max(s, axis=-1, keepdims=True))
        prb = pr.astype(BF16)
        pv = (_dot_nt(prb, vb) if kv_t else _dot(prb, vb)) / jnp.sum(pr, axis=-1, keepdims=True)
        o = jnp.where(lane == hd, pv, o)
    o_ref[...] = x + _dot(o.astype(BF16), wo_ref[...])


def mem_attend(x, acts, ws, k, v, p, r, nb, rows_per_batch, kv_t, layer=0):
    rows, d = x.shape
    row = lambda n: pl.BlockSpec((r, n), lambda i: (i, 0))
    names = ("norm_mem", "mem_wq", "m64", "mem_q_gain")
    if kv_t:
        kv_spec = pl.BlockSpec((None, nb, MEM_D, N_MEM), lambda i: (layer, i, 0, 0))
    elif nb == 1:
        kv_spec = pl.BlockSpec((N_MEM, MEM_D), lambda i: (i // (rows_per_batch // r), 0))
    else:
        kv_spec = pl.BlockSpec((nb * N_MEM, MEM_D), lambda i: (i, 0))
    return pl.pallas_call(
        functools.partial(_mem_attend_body, nb=nb, kv_t=kv_t, n_pre=len(acts)), grid=(rows // r,),
        in_specs=[row(d)] + [row(a.shape[1]) for a in acts] + [_wspec(w) for w in ws]
        + [_wspec(p[n]) for n in names] + [kv_spec, kv_spec, _wspec(p["mem_wo"])],
        out_specs=row(d),
        out_shape=jax.ShapeDtypeStruct((rows, d), F32),
        compiler_params=_cparams("parallel"), name="mem_attend",
    )(x, *acts, *[w.arr for w in ws], *[p[n].arr for n in names], k, v, p["mem_wo"].arr)


def _ffn_body(x_ref, g_ref, *refs, carry, tiles_per_seq, seg, tail_rows):
    if carry:
        wu_ref, wv_ref, wd_ref, cw_ref, cb_ref, o_ref, tail_ref, xn_ref, carry_ref = refs
        halo_ref = None
    else:
        halo_ref, wu_ref, wv_ref, wd_ref, cw_ref, cb_ref, o_ref, tail_ref, xn_ref = refs
        carry_ref = None
    tm = x_ref.shape[0]
    x = x_ref[...]
    xn_ref[...] = _rms(x, g_ref[...]).astype(BF16)
    row = lax.broadcasted_iota(jnp.int32, (tm, FFN_CHUNK), 0)
    if carry:
        @pl.when(pl.program_id(0) % tiles_per_seq == 0)
        def _():
            carry_ref[...] = jnp.zeros_like(carry_ref)
    else:
        t = row % seg
    n_chunks = D_FF // FFN_CHUNK

    def up(j):
        sl = slice(j * FFN_CHUNK, (j + 1) * FFN_CHUNK)
        xn = xn_ref[...]
        return _dot(xn, wu_ref[:, sl]), _dot(xn, wv_ref[:, sl])

    acc = x
    uv = up(0)
    for j in range(n_chunks):
        sl = slice(j * FFN_CHUNK, (j + 1) * FFN_CHUNK)
        u, v = uv
        if j + 1 < n_chunks:
            uv = up(j + 1)
        um1 = pltpu.roll(u, 1, 0)
        um2 = pltpu.roll(u, 2, 0)
        if carry:
            c6 = carry_ref[6:7, sl]
            c7 = carry_ref[7:8, sl]
            um1 = jnp.where(row == 0, c7, um1)
            um2 = jnp.where(row == 0, c6, jnp.where(row == 1, c7, um2))
            carry_ref[:, sl] = u[tm - SUBLANES:, :]
        else:
            hal = halo_ref[:, sl]
            um1 = jnp.where(t >= 1, um1, pltpu.roll(hal, tm - 1, 0))
            um2 = jnp.where(t >= 2, um2, hal)
        tail_ref[:, sl] = u[tm - tail_rows:, :]
        uc = um2 * cw_ref[0:1, sl] + um1 * cw_ref[1:2, sl] + u * cw_ref[2:3, sl] + cb_ref[:, sl]
        a = (_silu(uc) * v).astype(BF16)
        acc = acc + _dot(a, wd_ref[sl, :])
    o_ref[...] = acc


def ffn(x, halo, p, tm, carry, tiles_per_seq, seg, tail_rows):
    rows, d = x.shape
    n_tiles = rows // tm
    row_spec = lambda n: pl.BlockSpec((tm, n), lambda i: (i, 0))
    in_specs = [row_spec(d), _wspec(p["norm_ffn"])]
    args = [x, p["norm_ffn"].arr]
    if not carry:
        in_specs.append(row_spec(D_FF))
        args.append(halo)
    in_specs += [_wspec(p[n], resident=True) for n in ("ffn_wu", "ffn_wv", "ffn_wd")]
    in_specs += [_wspec(p["ffn_conv_w"]), _wspec(p["ffn_conv_b"])]
    args += [p[n].arr for n in ("ffn_wu", "ffn_wv", "ffn_wd", "ffn_conv_w", "ffn_conv_b")]
    scratch = [pltpu.VMEM((tm, d), BF16)]
    if carry:
        scratch.append(pltpu.VMEM((SUBLANES, D_FF), F32))
    return pl.pallas_call(
        functools.partial(_ffn_body, carry=carry, tiles_per_seq=tiles_per_seq, seg=seg, tail_rows=tail_rows),
        grid=(n_tiles,), in_specs=in_specs,
        out_specs=[row_spec(d), pl.BlockSpec((tail_rows, D_FF), lambda i: (i, 0))],
        out_shape=[jax.ShapeDtypeStruct((rows, d), F32), jax.ShapeDtypeStruct((n_tiles * tail_rows, D_FF), F32)],
        scratch_shapes=scratch,
        compiler_params=_cparams("arbitrary"), name="ffn",
    )(*args)


def _ssd_body(*refs, ns, nb, seg, carry, has_state):
    z_ref, xbc_ref, dtr_ref = refs[0:3]
    i = 3
    halo_ref = None
    if not carry:
        halo_ref = refs[i]
        i += 1
    consts = refs[i:i + 7]
    i += 7
    h0_ref = None
    if has_state:
        h0_ref = refs[i]
        i += 1
    o_ref, h_ref = refs[i:i + 2]
    tail_ref = refs[i + 2] if carry else None

    @pl.when(pl.program_id(1) == 0)
    def _():
        h_ref[...] = h0_ref[...] if has_state else jnp.zeros_like(h_ref)
        if carry:
            tail_ref[...] = jnp.zeros_like(tail_ref)

    for s in range(ns):
        _ssd_stream(z_ref.at[s], xbc_ref.at[s], dtr_ref.at[s], None if carry else halo_ref.at[s], consts,
                    o_ref.at[s], h_ref.at[s], tail_ref.at[s] if carry else None, nb=nb, seg=seg, carry=carry)


def _ssd_stream(z_ref, xbc_ref, dtr_ref, halo_ref, consts, o_ref, h_ref, tail_ref, *, nb, seg, carry):
    cw_ref, cb_ref, dtb_ref, alog_ref, d_ref, ng_ref, rep_ref = consts
    r = z_ref.shape[0]
    hp = SSD_HEADS * SSD_HEADDIM
    gp = hp // SSD_GROUPS

    x = xbc_ref[...]
    row = lax.broadcasted_iota(jnp.int32, (r, SSD_CONV_DIM), 0)
    row8 = lax.broadcasted_iota(jnp.int32, (SUBLANES, SSD_CONV_DIM), 0)

    def prev(k):
        xs = pltpu.roll(x, k, 0)
        if carry:
            tl = pltpu.roll(tail_ref[...], k, 0)
            top = jnp.where(row8 < k, tl, xs[:SUBLANES])
            return jnp.concatenate([top, xs[SUBLANES:]], axis=0)
        sh = SSD_CONV - 1 - k
        hal = halo_ref[...]
        hs_ = hal if sh == 0 else pltpu.roll(hal, r - sh, 0)
        return jnp.where(row % seg >= k, xs, hs_)

    u = (cb_ref[...] + x * cw_ref[3:4, :] + prev(1) * cw_ref[2:3, :] + prev(2) * cw_ref[1:2, :]
         + prev(3) * cw_ref[0:1, :])
    if carry:
        tail_ref[...] = x[r - SUBLANES:, :]
    u = _silu(u)

    ri = lax.broadcasted_iota(jnp.int32, (r, r), 0)
    ci = lax.broadcasted_iota(jnp.int32, (r, r), 1)
    if nb > 1:
        tril = jnp.logical_and((ri // seg) == (ci // seg), ci <= ri)
    else:
        tril = ci <= ri
    tri = jnp.where(tril, 1.0, 0.0).astype(BF16)

    dtc = _softplus(dtr_ref[...] + dtb_ref[...])
    csc = _dot_sel(tri, dtc * (-jnp.exp(alog_ref[...])))
    dt = _dot_rep(dtc, rep_ref[...])
    cs = _dot_rep(csc, rep_ref[...])
    tot = _seg_total(cs, nb, seg) if nb > 1 else _seg_total(cs, 1, r)

    lane = lax.broadcasted_iota(jnp.int32, (r, LANES), 1)
    rb = lax.broadcasted_iota(jnp.int32, (r, LANES), 0) // seg
    lo = lane < SSD_HEADDIM

    def pair(a, b):
        return jnp.where(lo, a, b)

    def hs(a, hd):
        return a[:, hd * LANES:(hd + 1) * LANES]

    zz = z_ref[...]
    hall = h_ref[...].astype(BF16)
    for g in range(SSD_GROUPS):
        h0i, h1i = 2 * g, 2 * g + 1
        gs = slice(g * LANES, (g + 1) * LANES)
        xg = u[:, gs]
        bg = u[:, SSD_D + g * SSD_STATE:SSD_D + (g + 1) * SSD_STATE].astype(BF16)
        cg = u[:, SSD_D + SSD_GROUPS * SSD_STATE + g * SSD_STATE:
               SSD_D + SSD_GROUPS * SSD_STATE + (g + 1) * SSD_STATE].astype(BF16)
        cbm = _dot_nt(cg, bg)
        w0 = (cbm * jnp.where(tril, jnp.exp(hs(cs, h0i) - hs(cs, h0i).T), 0.0)).astype(BF16)
        w1 = (cbm * jnp.where(tril, jnp.exp(hs(cs, h1i) - hs(cs, h1i).T), 0.0)).astype(BF16)
        dtp = pair(hs(dt, h0i), hs(dt, h1i))
        csp = pair(hs(cs, h0i), hs(cs, h1i))
        totp = pair(hs(tot, h0i), hs(tot, h1i))
        xdt = (xg * dtp).astype(BF16)
        y = pair(_dot(w0, xdt), _dot(w1, xdt))
        zst = _dot_nt(cg, hall)
        if nb == 1:
            yst = zst[:, g * gp:(g + 1) * gp]
        else:
            yst = jnp.zeros((r, gp), F32)
            for b in range(nb):
                yst = jnp.where(rb == b, zst[:, b * hp + g * gp:b * hp + (g + 1) * gp], yst)
        y = y + yst * jnp.exp(csp) + d_ref[:, gs] * xg
        y = y * _silu(zz[:, gs])
        o_ref[:, gs] = _rms(y, ng_ref[:, gs]).astype(o_ref.dtype)

        xw = (xg * (jnp.exp(totp - csp) * dtp)).astype(BF16)
        if nb > 1:
            xw = jnp.concatenate([jnp.where(rb == b, xw, jnp.zeros_like(xw)) for b in range(nb)], axis=1)
        dh = _dot_tn(xw, bg)
        for b in range(nb):
            r0 = b * seg if nb > 1 else 0
            dec = jnp.concatenate(
                [jnp.broadcast_to(jnp.exp(hs(tot, h0i)[r0:r0 + 1, :]), (SSD_HEADDIM, LANES)),
                 jnp.broadcast_to(jnp.exp(hs(tot, h1i)[r0:r0 + 1, :]), (SSD_HEADDIM, LANES))], axis=0)
            sl = slice(b * hp + g * gp, b * hp + (g + 1) * gp)
            h_ref[sl, :] = h_ref[sl, :] * dec + dh[b * gp:(b + 1) * gp, :]


def ssd(z, xbc, dtr, halo, h0, p, nb, seg, ns, layer=0):
    n_groups, lg, _ = z.shape
    r = SSD_CHUNK
    carry = halo is None
    n_chunks = lg // r
    assert lg % r == 0 and n_groups % ns == 0 and (carry or n_chunks == 1)
    hp = SSD_HEADS * SSD_HEADDIM
    row_spec = lambda n: pl.BlockSpec((ns, r, n), lambda b, c: (b, c, 0))
    st_spec = pl.BlockSpec((ns, nb * hp, SSD_STATE), lambda b, c: (b, 0, 0))
    in_specs = [row_spec(SSD_D), row_spec(SSD_CONV_DIM), row_spec(LANES)]
    args = [z, xbc, dtr]
    if not carry:
        in_specs.append(row_spec(SSD_CONV_DIM))
        args.append(halo)
    names = ("ssd_conv_w", "ssd_conv_b", "ssd_dt_bias", "ssd_a_log", "ssd_d", "ssd_norm", "ssd_rep")
    in_specs += [_wspec(p[n]) for n in names]
    args += [p[n].arr for n in names]
    if h0 is not None:
        in_specs.append(pl.BlockSpec((None, ns, nb * hp, SSD_STATE), lambda b, c: (layer, b, 0, 0)))
        args.append(h0)
    scratch = [pltpu.VMEM((ns, SUBLANES, SSD_CONV_DIM), F32)] if carry else []
    return pl.pallas_call(
        functools.partial(_ssd_body, ns=ns, nb=nb, seg=seg, carry=carry, has_state=h0 is not None),
        grid=(n_groups // ns, n_chunks), in_specs=in_specs,
        out_specs=[row_spec(SSD_D), st_spec],
        out_shape=[jax.ShapeDtypeStruct((n_groups, lg, SSD_D), BF16),
                   jax.ShapeDtypeStruct((n_groups, nb * hp, SSD_STATE), F32)],
        scratch_shapes=scratch,
        compiler_params=_cparams("parallel", "arbitrary"), name="ssd",
    )(*args)


def _gla_body(*refs, ns, nb, seg, c, has_state):
    hg3_ref, wg_ref, bg_ref, m_ref, ng_ref = refs[0:5]
    i = 5
    s0_ref = None
    if has_state:
        s0_ref = refs[i]
        i += 1
    o3_ref, stc_ref, st3_ref = refs[i:i + 3]
    r = hg3_ref.shape[1]

    def head_blocks():
        for s in range(ns):
            for b in range(nb):
                for hd in range(GLA_HEADS):
                    yield s, slice(b * GLA_VD + hd * GLA_DV, b * GLA_VD + (hd + 1) * GLA_DV), \
                        slice(hd * GLA_DK, (hd + 1) * GLA_DK)

    @pl.when(pl.program_id(1) == 0)
    def _():
        st3_ref[...] = jnp.zeros_like(st3_ref)
        if has_state:
            for s, rows, lanes in head_blocks():
                st3_ref[s, rows, lanes] = s0_ref[s, rows, :]

    sl_ = seg if nb > 1 else c
    ri = lax.broadcasted_iota(jnp.int32, (r, r), 0)
    ci = lax.broadcasted_iota(jnp.int32, (r, r), 1)
    same = (ri // sl_) == (ci // sl_)
    tril = jnp.logical_and(same, ci <= ri)
    tri = jnp.where(tril, 1.0, 0.0).astype(BF16)
    klane = lax.broadcasted_iota(jnp.int32, (r, GLA_KD), 1) // GLA_DK
    vlane = lax.broadcasted_iota(jnp.int32, (r, GLA_VD), 1) // GLA_DV
    rbv = lax.broadcasted_iota(jnp.int32, (r, GLA_VD), 0) // seg
    blk = (lax.broadcasted_iota(jnp.int32, (GLA_VD, GLA_KD), 0) // GLA_DV
           == lax.broadcasted_iota(jnp.int32, (GLA_VD, GLA_KD), 1) // GLA_DK)

    for s in range(ns):
        hg_ref, o_ref, st_ref = hg3_ref.at[s], o3_ref.at[s], st3_ref.at[s]
        q = hg_ref[:, 0:GLA_KD] * (GLA_DK ** -0.5)
        k = hg_ref[:, GLA_KD:2 * GLA_KD]
        vb = hg_ref[:, 2 * GLA_KD:2 * GLA_KD + GLA_VD].astype(BF16)
        gg = hg_ref[:, 2 * GLA_KD + GLA_VD:2 * GLA_KD + 2 * GLA_VD]
        glr = hg_ref[:, 2 * GLA_KD + 2 * GLA_VD:2 * GLA_KD + 2 * GLA_VD + LANES].astype(BF16)
        gate = _log_sigmoid(_dot(glr, wg_ref[...]) + bg_ref[...]) * (1.0 / GLA_GATE_TAU)
        bc = _dot_sel(tri, gate)
        tot = _seg_total(bc, r // sl_, sl_)
        qt = (q * jnp.exp(bc)).astype(BF16)
        kt = (k * jnp.exp(-bc)).astype(BF16)
        kd = (k * jnp.exp(tot - bc)).astype(BF16)
        etot = jnp.exp(tot)
        o = jnp.zeros((r, GLA_VD), F32)
        for hd in range(GLA_HEADS):
            a = _dot_nt(jnp.where(klane == hd, qt, jnp.zeros_like(qt)), kt)
            a = jnp.where(tril, a, 0.0).astype(BF16)
            o = o + _dot(a, jnp.where(vlane == hd, vb, jnp.zeros_like(vb)))
        if nb > 1:
            zs = _dot_nt(qt, st_ref[...].astype(BF16))
            ost = jnp.zeros((r, GLA_VD), F32)
            for b in range(nb):
                ost = jnp.where(rbv == b, zs[:, b * GLA_VD:(b + 1) * GLA_VD], ost)
            o = o + ost
            vexp = jnp.concatenate([jnp.where(rbv == b, vb, jnp.zeros_like(vb)) for b in range(nb)], axis=1)
            ds = _dot_tn(vexp, kd)
            for b in range(nb):
                sl = slice(b * GLA_VD, (b + 1) * GLA_VD)
                st_ref[sl, :] = st_ref[sl, :] * etot[b * seg:b * seg + 1, :] + jnp.where(blk, ds[sl, :], 0.0)
        else:
            st = st_ref[...]
            parts = []
            for sub in range(r // c):
                rs = slice(sub * c, (sub + 1) * c)
                parts.append(_dot_nt(qt[rs, :], st.astype(BF16)))
                st = st * etot[sub * c:sub * c + 1, :] + jnp.where(blk, _dot_tn(vb[rs, :], kd[rs, :]), 0.0)
            st_ref[...] = st
            o = o + jnp.concatenate(parts, axis=0)
        msq = _dot((o * o).astype(BF16), m_ref[...])
        on = o * lax.rsqrt(msq + EPS) * ng_ref[...]
        o_ref[...] = (on * _silu(gg)).astype(o_ref.dtype)

    @pl.when(pl.program_id(1) == pl.num_programs(1) - 1)
    def _():
        for s, rows, lanes in head_blocks():
            stc_ref[s, rows, :] = st3_ref[s, rows, lanes]


def gla(hg, s0, p, nb, seg, c, r, ns):
    n_groups, lg, width = hg.shape
    n_steps = lg // r
    assert lg % r == 0 and r % c == 0 and n_groups % ns == 0
    row_spec = lambda n: pl.BlockSpec((ns, r, n), lambda b, s: (b, s, 0))
    st_spec = pl.BlockSpec((ns, nb * GLA_VD, GLA_DK), lambda b, s: (b, 0, 0))
    names = ("gla_w_gate", "gla_b_gate", "m64", "gla_norm")
    in_specs = [row_spec(width)] + [_wspec(p[n]) for n in names]
    args = [hg] + [p[n].arr for n in names]
    if s0 is not None:
        in_specs.append(st_spec)
        args.append(s0)
    return pl.pallas_call(
        functools.partial(_gla_body, ns=ns, nb=nb, seg=seg, c=c, has_state=s0 is not None),
        grid=(n_groups // ns, n_steps), in_specs=in_specs,
        out_specs=[row_spec(GLA_VD), st_spec],
        out_shape=[jax.ShapeDtypeStruct((n_groups, lg, GLA_VD), BF16),
                   jax.ShapeDtypeStruct((n_groups, nb * GLA_VD, GLA_DK), F32)],
        scratch_shapes=[pltpu.VMEM((ns, nb * GLA_VD, GLA_KD), F32)],
        compiler_params=_cparams("parallel", "arbitrary"), name="gla",
    )(*args)


def _q_absorb_body(q_ref, wt_ref, sel_ref, qa_out, qr_out):
    for hd in range(MLA_HEADS):
        qh = q_ref[:, hd * LANES:(hd + 1) * LANES].astype(F32)
        qa_out[:, hd * LANES:(hd + 1) * LANES] = _dot(qh, wt_ref[hd], HI).astype(BF16)
        qr_out[:, hd * LANES:(hd + 1) * LANES] = _dot(qh, sel_ref[...], HI).astype(BF16)


def q_absorb(q_cat, p, tm):
    rows, hw = q_cat.shape
    row_spec = pl.BlockSpec((tm, hw), lambda i: (i, 0))
    return pl.pallas_call(
        _q_absorb_body, grid=(rows // tm,),
        in_specs=[row_spec, _wspec(p["wukt_g"]), _wspec(p["rope_sel"])],
        out_specs=[row_spec, row_spec],
        out_shape=[jax.ShapeDtypeStruct((rows, hw), BF16)] * 2,
        compiler_params=_cparams("parallel"), name="q_absorb",
    )(q_cat, p["wukt_g"].arr, p["rope_sel"].arr)


PAGED_CP = 64
PAGED_SUB = 512


def _mla_paged_body(pt_ref, ptn_ref, wukt_ref, qa_ref, qr_ref, cnew_ref, krnew_ref, wuv_ref, lat_hbm, kr_hbm,
                    o_ref, lat_buf, kr_buf, sem, m_ref, l_ref, acc_ref, *, layer, n_pages, n_batch, seq, cp):
    bi = pl.program_id(0)
    nc = n_pages // cp
    nq = MLA_HEADS * seq
    nk = MLA_HEADS * MLA_NOPE

    def copies(table, chunk, slot):
        out = []
        for pg in range(cp):
            pid = table[0, 0, chunk * cp + pg]
            dst = pl.ds(pg * PAGE_SIZE, PAGE_SIZE)
            out.append(pltpu.make_async_copy(lat_hbm.at[layer, pid], lat_buf.at[slot, dst, :], sem.at[0, slot]))
            out.append(pltpu.make_async_copy(kr_hbm.at[layer, pid], kr_buf.at[slot, :, dst], sem.at[1, slot]))
        return out

    def start(table, chunk, slot):
        for cpy in copies(table, chunk, slot):
            cpy.start()

    def wait(slot):
        for cpy in copies(pt_ref, 0, slot):
            cpy.wait()

    @pl.when(bi == 0)
    def _():
        start(pt_ref, 0, 0)

    m_ref[...] = jnp.full(m_ref.shape, -jnp.inf, F32)
    l_ref[...] = jnp.zeros(l_ref.shape, F32)
    acc_ref[...] = jnp.zeros(acc_ref.shape, F32)
    lhs = jnp.concatenate([wukt_ref[...], qa_ref[0]], axis=0)
    qr = qr_ref[0]

    def scores(latb, krt):
        big = _dot_nt(lhs, latb)
        kt = big[0:nk, :]
        ss = jnp.sum((kt * kt).reshape(MLA_HEADS, MLA_NOPE, kt.shape[1]), axis=1) * (1.0 / MLA_NOPE)
        rinv = lax.rsqrt(ss + EPS)
        rexp = jnp.concatenate([jnp.broadcast_to(rinv[hd:hd + 1, :], (seq, rinv.shape[1]))
                                for hd in range(MLA_HEADS)], axis=0)
        return big[nk:, :] * rexp + _dot(qr, krt.astype(BF16))

    def attend(lat, krt, mask):
        keys = lat.shape[0]
        sub = min(PAGED_SUB, keys)
        latb = lat.astype(BF16)
        s = jnp.concatenate([scores(latb[i * sub:(i + 1) * sub, :], krt[:, i * sub:(i + 1) * sub])
                             for i in range(keys // sub)], axis=1)
        if mask is not None:
            s = jnp.where(mask, s, -jnp.inf)
        m_old = m_ref[...]
        m_new = jnp.maximum(m_old, jnp.max(s, axis=-1, keepdims=True))
        pr = jnp.exp2(s - jnp.concatenate([m_new] * (keys // LANES), axis=1))
        corr = jnp.exp2(m_old - m_new)
        l_ref[...] = l_ref[...] * corr + jnp.sum(pr, axis=-1, keepdims=True)
        acc_ref[...] = acc_ref[...] * corr + _dot(pr.astype(BF16), latb)
        m_ref[...] = m_new

    lat_new = jnp.concatenate([cnew_ref[0], jnp.zeros((LANES - seq, MLA_KV_RANK), F32)], axis=0)
    qtok = lax.broadcasted_iota(jnp.int32, (nq, LANES), 0) % seq
    key = lax.broadcasted_iota(jnp.int32, (nq, LANES), 1)
    attend(lat_new, krnew_ref[0], key <= qtok)

    def step(slot, prefetch):
        prefetch()
        wait(slot)
        attend(lat_buf[slot], kr_buf[slot], None)

    def pair_body(jj, carry_):
        step(0, lambda: start(pt_ref, 2 * jj + 1, 1))
        step(1, lambda: start(pt_ref, 2 * jj + 2, 0))
        return carry_

    lax.fori_loop(0, nc // 2 - 1, pair_body, 0)
    step(0, lambda: start(pt_ref, nc - 1, 1))

    def next_batch_prefetch():
        @pl.when(bi + 1 < n_batch)
        def _():
            start(ptn_ref, 0, 0)

    step(1, next_batch_prefetch)

    olat = (acc_ref[...] / l_ref[...]).astype(BF16)
    o = jnp.zeros((seq, MLA_HEADS * MLA_V), F32)
    for hd in range(MLA_HEADS):
        o = o + _dot(olat[hd * seq:(hd + 1) * seq, :], wuv_ref[hd * LANES:(hd + 1) * LANES, :])
    o_ref[0] = o.astype(o_ref.dtype)


def mla_paged(page_table, wukt, qa, qr, c_new, krt_new, wuv_bd, cache_lat, cache_krt, layer):
    n_batch, n_pages = page_table.shape
    seq = c_new.shape[1]
    cp = min(PAGED_CP, n_pages // 2)
    assert n_pages % (2 * cp) == 0
    nq = MLA_HEADS * seq
    pt3 = page_table.reshape(n_batch, 1, n_pages)
    smem_spec = lambda f: pl.BlockSpec((1, 1, n_pages), f, memory_space=pltpu.SMEM)
    per_b = lambda shp: pl.BlockSpec((1,) + shp, lambda b: (b, 0, 0))
    kc = cp * PAGE_SIZE
    kr_new, cache_kr = krt_new, cache_krt
    return pl.pallas_call(
        functools.partial(_mla_paged_body, layer=layer, n_pages=n_pages, n_batch=n_batch, seq=seq, cp=cp),
        grid=(n_batch,),
        in_specs=[smem_spec(lambda b: (b, 0, 0)),
                  smem_spec(lambda b: (jnp.minimum(b + 1, n_batch - 1), 0, 0)),
                  _wspec(wukt), per_b(qa.shape[1:]), per_b(qr.shape[1:]), per_b(c_new.shape[1:]),
                  per_b(kr_new.shape[1:]),
                  _wspec(wuv_bd),
                  pl.BlockSpec(memory_space=pl.ANY), pl.BlockSpec(memory_space=pl.ANY)],
        out_specs=per_b((seq, MLA_HEADS * MLA_V)),
        out_shape=jax.ShapeDtypeStruct((n_batch, seq, MLA_HEADS * MLA_V), BF16),
        scratch_shapes=[pltpu.VMEM((2, kc, MLA_KV_RANK), F32), pltpu.VMEM((2, MLA_ROPE, kc), F32),
                        pltpu.SemaphoreType.DMA((2, 2)),
                        pltpu.VMEM((nq, LANES), F32), pltpu.VMEM((nq, LANES), F32), pltpu.VMEM((nq, LANES), F32)],
        compiler_params=_cparams("arbitrary"), name="mla_paged",
    )(pt3, pt3, wukt.arr, qa, qr, c_new, kr_new, wuv_bd.arr, cache_lat, cache_kr)


def _block_diag_mean(n, blocks):
    idx = jnp.arange(n)
    m = jnp.zeros((n, n), F32)
    for start, size in blocks:
        inb = jnp.logical_and(idx >= start, idx < start + size)
        m = m + jnp.where(jnp.logical_and(inb[:, None], inb[None, :]), 1.0 / size, 0.0)
    return m.astype(BF16)


def _rope_tables(pos):
    half = MLA_ROPE // 2
    inv = ROPE_THETA ** (-jnp.arange(half, dtype=F32) / half)
    ang = pos.astype(F32)[:, None] * inv[None, :]
    cos, sin = jnp.cos(ang), jnp.sin(ang)
    n = pos.shape[0]
    one, zero = jnp.ones((n, MLA_NOPE), F32), jnp.zeros((n, MLA_NOPE), F32)
    tail1, tail0 = jnp.ones((n, LANES - MLA_QK), F32), jnp.zeros((n, LANES - MLA_QK), F32)
    return (jnp.concatenate([one, cos, cos, tail1], axis=1), jnp.concatenate([zero, -sin, sin, tail0], axis=1))


def _pack_one(w):
    f = lambda name: w[name]
    bf = lambda a: a.astype(BF16)
    p = {}
    w_in = f("w_in")
    o = 0
    cols = {}
    for name, size in (("cq", MLA_Q_RANK), ("ckv", MLA_KV_RANK), ("krr", MLA_ROPE), ("z", SSD_D), ("xbc", SSD_CONV_DIM),
                       ("dt", SSD_HEADS), ("gq", GLA_KD), ("gk", GLA_KD), ("gv", GLA_VD), ("glr", GLA_GATE_RANK),
                       ("gg", GLA_VD)):
        cols[name] = w_in[:, o:o + size]
        o += size
    d = w_in.shape[0]
    krr_pad = jnp.zeros((d, LANES), F32).at[:, MLA_NOPE:MLA_QK].set(cols["krr"])
    glr_pad = jnp.zeros((d, LANES), F32).at[:, :GLA_GATE_RANK].set(cols["glr"])
    p["w_mla"] = bf(jnp.concatenate([cols["cq"], cols["ckv"], krr_pad], axis=1))
    p["w_z"] = bf(cols["z"])
    p["w_xbc"] = bf(cols["xbc"])
    p["w_dt"] = bf(jnp.zeros((d, LANES), F32).at[:, :SSD_HEADS].set(cols["dt"]))
    p["w_gla"] = bf(jnp.concatenate([cols["gq"], cols["gk"], cols["gv"], cols["gg"], glr_pad], axis=1))
    p["norm_mix"] = f("norm_mix").reshape(1, d)

    def head_pad(wm, n_real):
        k = wm.shape[0]
        out = jnp.zeros((k, MLA_HEADS, LANES), F32).at[:, :, :n_real].set(wm)
        return out.reshape(k, MLA_HEADS * LANES)

    p["wq_pad"] = bf(head_pad(f("mla_w_uq"), MLA_QK))
    p["wk_pad"] = bf(head_pad(f("mla_w_uk"), MLA_NOPE))
    scale = MLA_QK ** -0.5 * math.log2(math.e)
    zpad = jnp.zeros((LANES - MLA_QK,), F32)
    p["gq_vec"] = (jnp.concatenate([f("mla_qn_norm"), f("mla_qr_norm"), zpad]) * scale).reshape(1, LANES)
    p["gk_vec"] = jnp.concatenate([f("mla_kn_norm"), jnp.zeros((LANES - MLA_NOPE,), F32)]).reshape(1, LANES)
    p["gkr_vec"] = jnp.concatenate([jnp.zeros((MLA_NOPE,), F32), f("mla_kr_norm"), zpad]).reshape(1, LANES)
    p["mla_q_norm"] = f("mla_q_norm").reshape(1, MLA_Q_RANK)
    p["mla_kv_norm"] = f("mla_kv_norm").reshape(1, MLA_KV_RANK)
    p["mh"] = _block_diag_mean(LANES, [(0, MLA_NOPE), (MLA_NOPE, MLA_ROPE)])
    w_uv = f("mla_w_uv")
    eye = jnp.eye(MLA_HEADS, dtype=F32)
    p["wuv_bd"] = bf(jnp.einsum("chv,hg->hcgv", w_uv, eye).reshape(MLA_HEADS * LANES, MLA_HEADS * MLA_V))
    w_uk = f("mla_w_uk")
    wt = jnp.transpose(w_uk, (1, 2, 0)) * f("mla_kn_norm")[None, :, None]
    p["wukt_g"] = jnp.zeros((MLA_HEADS, LANES, LANES), F32).at[:, :MLA_NOPE, :].set(wt)
    p["wukt"] = bf(jnp.transpose(w_uk, (1, 2, 0)).reshape(MLA_HEADS * MLA_NOPE, MLA_KV_RANK))
    p["rope_sel"] = jnp.zeros((LANES, LANES), F32).at[MLA_NOPE + jnp.arange(MLA_ROPE), jnp.arange(MLA_ROPE)].set(1.0)

    p["ssd_conv_w"] = jnp.zeros((SUBLANES, SSD_CONV_DIM), F32).at[:SSD_CONV].set(f("ssd_conv_w"))
    p["ssd_conv_b"] = f("ssd_conv_b").reshape(1, SSD_CONV_DIM)
    lane_pad = lambda v: jnp.zeros((1, LANES), F32).at[0, :v.shape[0]].set(v)
    p["ssd_dt_bias"] = lane_pad(f("ssd_dt_bias"))
    p["ssd_a_log"] = lane_pad(f("ssd_a_log"))
    p["ssd_rep"] = bf(jnp.repeat(jnp.eye(LANES, SSD_HEADS, dtype=F32), LANES, axis=1))
    p["ssd_d"] = jnp.repeat(f("ssd_d"), SSD_HEADDIM).reshape(1, SSD_D)
    p["ssd_norm"] = f("ssd_norm").reshape(1, SSD_D)
    p["gla_w_gate"] = bf(jnp.zeros((LANES, GLA_KD), F32).at[:GLA_GATE_RANK].set(f("gla_w_gate")))
    p["gla_b_gate"] = f("gla_b_gate").reshape(1, GLA_KD)
    p["gla_norm"] = f("gla_norm").reshape(1, GLA_VD)
    p["m64"] = _block_diag_mean(MEM_D, [(i * 64, 64) for i in range(4)])
    w_o = f("w_o")
    p["wo_mla"] = bf(w_o[:MLA_HEADS * MLA_V])
    p["wo_ssd"] = bf(w_o[MLA_HEADS * MLA_V:MLA_HEADS * MLA_V + SSD_D])
    p["wo_gla"] = bf(w_o[MLA_HEADS * MLA_V + SSD_D:])
    p["norm_mem"] = f("norm_mem").reshape(1, D_MODEL)
    p["mem_wq"], p["mem_wk"], p["mem_wv"], p["mem_wo"] = bf(f("mem_wq")), bf(f("mem_wk")), bf(f("mem_wv")), bf(f("mem_wo"))
    p["mem_q_gain"] = (jnp.tile(f("mem_q_norm"), MEM_HEADS) * MEM_HEAD_DIM ** -0.5).reshape(1, MEM_D)
    p["mem_k_gain"] = jnp.tile(f("mem_k_norm"), MEM_HEADS).reshape(1, MEM_D)
    p["norm_ffn"] = f("norm_ffn").reshape(1, D_MODEL)
    w_up = f("ffn_w_up")
    p["ffn_wu"], p["ffn_wv"], p["ffn_wd"] = bf(w_up[:, :D_FF]), bf(w_up[:, D_FF:]), bf(f("ffn_w_down"))
    p["ffn_conv_w"] = jnp.zeros((SUBLANES, D_FF), F32).at[:FFN_CONV].set(f("ffn_conv_w"))
    p["ffn_conv_b"] = f("ffn_conv_b").reshape(1, D_FF)
    return p


def _pad_rows(state, seq):
    b, k, c = state.shape
    return jnp.pad(state, ((0, 0), (0, seq - k), (0, 0))).reshape(b * seq, c)


def _layer(x3, p, cos_t, sin_t, tm, sample, mem2d=None):
    b, L, d = x3.shape
    rows = b * L
    x = x3.reshape(rows, d)
    h_mla, z, xbc, dtr, h_gla = norm_matmul(
        x, p["norm_mix"], [p["w_mla"], p["w_z"], p["w_xbc"], p["w_dt"], p["w_gla"]], [F32] * 5, tm)
    q_cat, k_cat, c_kv, c_bf, kr_vec = mla_prep(h_mla, cos_t, sin_t, p, tm)
    kr = kr_vec[:, MLA_NOPE:MLA_QK]
    hw = MLA_HEADS * LANES
    if sample is None:
        o_mla = mla_prompt(q_cat.reshape(b, L, hw), k_cat.reshape(b, L, hw), c_bf.reshape(b, L, 2 * LANES),
                           p["wuv_bd"], math.gcd(L, 512), 256)
        o_mla = o_mla.reshape(rows, MLA_HEADS * MLA_V)
        g3 = lambda a: a.reshape(b, L, a.shape[-1])
        ns = math.gcd(b, 4)
        o_ssd, ssd_h = ssd(g3(z), g3(xbc), g3(dtr), None, None, p, 1, L, ns)
        o_gla, gla_st = gla(g3(h_gla), None, p, 1, L, GLA_CHUNK, 256, ns)
    else:
        qa, qrp = q_absorb(q_cat, p, tm)
        to_hq = lambda a, n: a.reshape(b, L, MLA_HEADS, LANES)[..., :n].transpose(0, 2, 1, 3).reshape(b, MLA_HEADS * L, n)
        krt_new = jnp.pad(jnp.swapaxes(kr.reshape(b, L, MLA_ROPE), 1, 2), ((0, 0), (0, 0), (0, LANES - L)))
        o_mla = mla_paged(sample["page_table"], p["wukt"], to_hq(qa, LANES), to_hq(qrp, MLA_ROPE),
                          c_kv.reshape(b, L, LANES), krt_new, p["wuv_bd"], sample["cache_lat"], sample["cache_krt"],
                          sample["layer"])
        o_mla = o_mla.reshape(rows, MLA_HEADS * MLA_V)
        nb = SSD_CHUNK // L
        ng = b // nb
        g3 = lambda a: a.reshape(ng, nb * L, a.shape[-1])
        ns = math.gcd(ng, 2)
        o_ssd, ssd_h = ssd(g3(z), g3(xbc), g3(dtr), g3(_pad_rows(sample["ssd_conv"], L)),
                           sample["ssd_all"].reshape(-1, ng, nb * SSD_D, SSD_STATE), p, nb, L, ns, sample["layer"])
        st0 = jnp.swapaxes(sample["gla"], 2, 3).reshape(ng, nb * GLA_VD, GLA_DK)
        o_gla, gla_st = gla(g3(h_gla), st0, p, nb, L, nb * L, nb * L, ns)
    o_ssd, o_gla = o_ssd.reshape(rows, SSD_D), o_gla.reshape(rows, GLA_VD)
    mix, w_mix = [o_mla, o_ssd, o_gla], [p["wo_mla"], p["wo_ssd"], p["wo_gla"]]
    if sample is None:
        mk, mv = mem_kv(mem2d, p, 512)
        x = mem_attend(x, mix, w_mix, mk, mv, p, math.gcd(L, 512), 1, L, False)
    else:
        nbm = 128 // L
        x = mem_attend(x, mix, w_mix, sample["mem_kt"], sample["mem_vt"], p, nbm * L, nbm, L, True,
                       sample["layer"])

    ft = min(rows, FFN_ROWS)
    if sample is None:
        x, u_tail = ffn(x, None, p, ft, True, L // ft, L, SUBLANES)
        ffn_conv = u_tail.reshape(b, L // ft, SUBLANES, D_FF)[:, -1, SUBLANES - (FFN_CONV - 1):, :]
    else:
        x, u_full = ffn(x, _pad_rows(sample["ffn_conv"], L), p, ft, False, 1, L, ft)
        ffn_conv = u_full.reshape(b, L, D_FF)[:, L - (FFN_CONV - 1):, :]

    xbc3 = xbc.reshape(b, L, SSD_CONV_DIM)
    out = dict(
        x=x.reshape(b, L, d), lat=c_kv.reshape(b, L, MLA_KV_RANK), kr=kr.reshape(b, L, MLA_ROPE),
        ssd_conv=xbc3[:, L - (SSD_CONV - 1):, :], ssd=ssd_h.reshape(b, SSD_HEADS, SSD_HEADDIM, SSD_STATE),
        gla=jnp.swapaxes(gla_st.reshape(b, GLA_HEADS, GLA_DV, GLA_DK), 2, 3),
        ffn_conv=ffn_conv)
    if sample is None:
        out["mem_k"] = mk.reshape(b, N_MEM, MEM_HEADS, MEM_HEAD_DIM)
        out["mem_v"] = mv.reshape(b, N_MEM, MEM_HEADS, MEM_HEAD_DIM)
    return out


def kernel(x_prompt, x_sample, cache_mla_latent, cache_mla_krope, cache_mem_k, cache_mem_v, state_ssd_conv, state_ssd, state_gla, state_ffn_conv, page_table, mem_prompt, norm_mix, w_in, mla_q_norm, mla_w_uq, mla_kv_norm, mla_w_uk, mla_w_uv, mla_qn_norm, mla_qr_norm, mla_kn_norm, mla_kr_norm, ssd_conv_w, ssd_conv_b, ssd_dt_bias, ssd_a_log, ssd_d, ssd_norm, gla_w_gate, gla_b_gate, gla_norm, w_o, norm_mem, mem_wq, mem_wk, mem_wv, mem_wo, mem_q_norm, mem_k_norm, norm_ffn, ffn_w_up, ffn_conv_w, ffn_conv_b, ffn_w_down):
    weights = dict(
        norm_mix=norm_mix, w_in=w_in, mla_q_norm=mla_q_norm, mla_w_uq=mla_w_uq, mla_kv_norm=mla_kv_norm,
        mla_w_uk=mla_w_uk, mla_w_uv=mla_w_uv, mla_qn_norm=mla_qn_norm, mla_qr_norm=mla_qr_norm,
        mla_kn_norm=mla_kn_norm, mla_kr_norm=mla_kr_norm, ssd_conv_w=ssd_conv_w, ssd_conv_b=ssd_conv_b,
        ssd_dt_bias=ssd_dt_bias, ssd_a_log=ssd_a_log, ssd_d=ssd_d, ssd_norm=ssd_norm, gla_w_gate=gla_w_gate,
        gla_b_gate=gla_b_gate, gla_norm=gla_norm, w_o=w_o, norm_mem=norm_mem, mem_wq=mem_wq, mem_wk=mem_wk,
        mem_wv=mem_wv, mem_wo=mem_wo, mem_q_norm=mem_q_norm, mem_k_norm=mem_k_norm, norm_ffn=norm_ffn,
        ffn_w_up=ffn_w_up, ffn_conv_w=ffn_conv_w, ffn_conv_b=ffn_conv_b, ffn_w_down=ffn_w_down)
    depth = w_in.shape[0]
    bp, lp, _ = x_prompt.shape
    bs, ls, _ = x_sample.shape
    assert ls >= SSD_CONV - 1 and SSD_CHUNK % ls == 0
    past_len = page_table.shape[1] * PAGE_SIZE
    tm_p, tm_s = 512, min(512, bs * ls)
    cos_p, sin_p = _rope_tables(jnp.arange(lp))
    cos_s, sin_s = _rope_tables(past_len + jnp.arange(ls))
    cos_s, sin_s = jnp.tile(cos_s, (tm_s // ls, 1)), jnp.tile(sin_s, (tm_s // ls, 1))
    mem2d = mem_prompt.reshape(bp * N_MEM, D_MODEL)
    cache_krt = jnp.swapaxes(cache_mla_krope, 2, 3)
    mem_kt = jnp.transpose(cache_mem_k, (0, 1, 3, 4, 2)).reshape(depth, bs, MEM_D, N_MEM)
    mem_vt = jnp.transpose(cache_mem_v, (0, 1, 3, 4, 2)).reshape(depth, bs, MEM_D, N_MEM)

    packed = jax.vmap(_pack_one)(weights)
    xp, xs = x_prompt, x_sample
    outs_p, outs_s = [], []
    for l in range(depth):
        p = {name: _Stacked(arr, l) for name, arr in packed.items()}
        op = _layer(xp, p, cos_p, sin_p, tm_p, None, mem2d)
        xp = op["x"]
        outs_p.append(op)
        sample = dict(layer=l, page_table=page_table, cache_lat=cache_mla_latent, cache_krt=cache_krt,
                      mem_kt=mem_kt, mem_vt=mem_vt, ssd_conv=state_ssd_conv[l], ssd_all=state_ssd,
                      gla=state_gla[l], ffn_conv=state_ffn_conv[l])
        os_ = _layer(xs, p, cos_s, sin_s, tm_s, sample)
        xs = os_["x"]
        outs_s.append(os_)

    st = lambda outs, key: jnp.stack([o[key] for o in outs])
    return (xp, xs,
            st(outs_p, "lat"), st(outs_p, "kr"), st(outs_p, "mem_k"), st(outs_p, "mem_v"),
            st(outs_p, "ssd_conv"), st(outs_p, "ssd"), st(outs_p, "gla"), st(outs_p, "ffn_conv"),
            st(outs_s, "lat"), st(outs_s, "kr"), st(outs_s, "ssd_conv"), st(outs_s, "ssd"), st(outs_s, "gla"),
            st(outs_s, "ffn_conv"))
```

```python
import functools
import math

import jax
import jax.numpy as jnp
from jax import lax
from jax.experimental import pallas as pl
from jax.experimental.pallas import tpu as pltpu

F32 = jnp.float32
BF16 = jnp.bfloat16
EPS = 1e-6
LANES = 128
SUBLANES = 8
VMEM_LIMIT = 56 * 1024 * 1024

D_MODEL = 1024
MLA_HEADS, MLA_NOPE, MLA_ROPE, MLA_V = 8, 64, 32, 64
MLA_QK = MLA_NOPE + MLA_ROPE
MLA_Q_RANK, MLA_KV_RANK = 256, 128
ROPE_THETA = 10000.0
PAGE_SIZE = 128
SSD_HEADS, SSD_HEADDIM, SSD_GROUPS, SSD_STATE, SSD_CONV = 4, 64, 2, 128, 4
SSD_D = SSD_HEADS * SSD_HEADDIM
SSD_CONV_DIM = SSD_D + 2 * SSD_GROUPS * SSD_STATE
SSD_CHUNK = 128
GLA_HEADS, GLA_DK, GLA_DV = 4, 32, 64
GLA_KD, GLA_VD = GLA_HEADS * GLA_DK, GLA_HEADS * GLA_DV
GLA_GATE_RANK, GLA_GATE_TAU, GLA_CHUNK = 16, 16.0, 64
N_MEM, MEM_HEADS, MEM_HEAD_DIM = 256, 4, 64
MEM_D = MEM_HEADS * MEM_HEAD_DIM
D_FF, FFN_CONV = 2816, 3
FFN_CHUNK = 256
FFN_ROWS = 256

HI = lax.Precision.HIGHEST


def _cparams(*sem):
    return pltpu.CompilerParams(dimension_semantics=sem, vmem_limit_bytes=VMEM_LIMIT)


def _dot(a, b, precision=None):
    return jnp.dot(a, b, preferred_element_type=F32, precision=precision)


def _dot_nt(a, b):
    return lax.dot_general(a, b, (((1,), (1,)), ((), ())), preferred_element_type=F32)


def _dot_tn(a, b):
    return lax.dot_general(a, b, (((0,), (0,)), ((), ())), preferred_element_type=F32)


def _split3(x):
    x1 = x.astype(BF16)
    r1 = x - x1.astype(F32)
    x2 = r1.astype(BF16)
    x3 = (r1 - x2.astype(F32)).astype(BF16)
    return x1, x2, x3


def _dot_sel(sel, x):
    return sum(_dot(sel, t) for t in _split3(x))


def _dot_rep(x, sel):
    return sum(_dot(t, sel) for t in _split3(x))


def _seg_total(cs, n_seg, seg_len):
    return jnp.concatenate(
        [jnp.broadcast_to(cs[(i + 1) * seg_len - 1:(i + 1) * seg_len, :], (seg_len, cs.shape[1]))
         for i in range(n_seg)], axis=0)


def _rms(x, g):
    return x * lax.rsqrt(jnp.mean(x * x, axis=-1, keepdims=True) + EPS) * g


def _sigmoid(x):
    return 1.0 / (1.0 + jnp.exp(-x))


def _silu(x):
    return x * _sigmoid(x)


def _softplus(x):
    return jnp.maximum(x, 0.0) + jnp.log1p(jnp.exp(-jnp.abs(x)))


def _log_sigmoid(x):
    return jnp.minimum(x, 0.0) - jnp.log1p(jnp.exp(-jnp.abs(x)))


class _Stacked:
    def __init__(self, arr, layer):
        self.arr, self.layer = arr, layer

    @property
    def shape(self):
        return self.arr.shape[1:]


def _wspec(w, resident=False):
    shp, layer = w.shape, w.layer
    kw = dict(pipeline_mode=pl.Buffered(1)) if resident else {}
    return pl.BlockSpec((None,) + shp, lambda *_: (layer,) + (0,) * len(shp), **kw)


def _norm_mm_body(x_ref, g_ref, *refs, n_out, has_norm):
    w_refs, o_refs = refs[:n_out], refs[n_out:]
    x = x_ref[...]
    if has_norm:
        x = _rms(x, g_ref[...])
    xb = x.astype(BF16)
    for w_ref, o_ref in zip(w_refs, o_refs):
        o_ref[...] = _dot(xb, w_ref[...]).astype(o_ref.dtype)


def norm_matmul(x, g, ws, out_dtypes, tm, has_norm=True):
    rows, k = x.shape
    assert rows % tm == 0
    in_specs = [pl.BlockSpec((tm, k), lambda i: (i, 0)), _wspec(g)]
    in_specs += [_wspec(w) for w in ws]
    out_specs = [pl.BlockSpec((tm, w.shape[1]), lambda i: (i, 0)) for w in ws]
    out_shape = [jax.ShapeDtypeStruct((rows, w.shape[1]), dt) for w, dt in zip(ws, out_dtypes)]
    return pl.pallas_call(
        functools.partial(_norm_mm_body, n_out=len(ws), has_norm=has_norm),
        grid=(rows // tm,), in_specs=in_specs, out_specs=out_specs, out_shape=out_shape,
        compiler_params=_cparams("parallel"), name="norm_matmul",
    )(x, g.arr, *[w.arr for w in ws])


def _mla_prep_body(h_ref, cos_ref, sin_ref, qng_ref, wq_ref, gq_ref, kvg_ref, wk_ref, gk_ref, krg_ref, mh_ref,
                   q_out, k_out, ckv_out, cbf_out, kr_out):
    tm = h_ref.shape[0]
    h = h_ref[...]
    cq, ckv, krr = h[:, 0:256], h[:, 256:384], h[:, 384:512]
    cos, sin = cos_ref[...], sin_ref[...]
    lane = lax.broadcasted_iota(jnp.int32, (tm, LANES), 1)
    mh = mh_ref[...]

    def rope(xv):
        rot = jnp.where(lane < MLA_NOPE + MLA_ROPE // 2,
                        pltpu.roll(xv, LANES - MLA_ROPE // 2, 1), pltpu.roll(xv, MLA_ROPE // 2, 1))
        return xv * cos + rot * sin

    def seg_norm(xv, g):
        msq = _dot((xv * xv).astype(BF16), mh)
        return xv * lax.rsqrt(msq + EPS) * g

    q = _dot(_rms(cq, qng_ref[...]).astype(BF16), wq_ref[...])
    c = _rms(ckv, kvg_ref[...])
    cb = c.astype(BF16)
    ckv_out[...] = c
    cbf_out[...] = jnp.concatenate([cb, jnp.ones_like(cb)], axis=1)
    kr = rope(seg_norm(krr, krg_ref[...]))
    kr_out[...] = kr[:, MLA_NOPE:MLA_QK]
    k = _dot(cb, wk_ref[...])
    gq, gk = gq_ref[...], gk_ref[...]
    for hd in range(MLA_HEADS):
        sl = slice(hd * LANES, (hd + 1) * LANES)
        q_out[:, sl] = rope(seg_norm(q[:, sl], gq)).astype(BF16)
        k_out[:, sl] = (seg_norm(k[:, sl], gk) + kr).astype(BF16)


def mla_prep(h_mla, cos_t, sin_t, p, tm):
    rows = h_mla.shape[0]
    nt = cos_t.shape[0] // tm
    row_spec = lambda n: pl.BlockSpec((tm, n), lambda i: (i, 0))
    tab_spec = pl.BlockSpec((tm, LANES), lambda i: (i % nt, 0))
    hw = MLA_HEADS * LANES
    names = ("mla_q_norm", "wq_pad", "gq_vec", "mla_kv_norm", "wk_pad", "gk_vec", "gkr_vec", "mh")
    return pl.pallas_call(
        _mla_prep_body, grid=(rows // tm,),
        in_specs=[row_spec(512), tab_spec, tab_spec] + [_wspec(p[n]) for n in names],
        out_specs=[row_spec(hw), row_spec(hw), row_spec(LANES), row_spec(2 * LANES), row_spec(MLA_ROPE)],
        out_shape=[jax.ShapeDtypeStruct((rows, hw), BF16), jax.ShapeDtypeStruct((rows, hw), BF16),
                   jax.ShapeDtypeStruct((rows, LANES), F32), jax.ShapeDtypeStruct((rows, 2 * LANES), BF16),
                   jax.ShapeDtypeStruct((rows, MLA_ROPE), F32)],
        compiler_params=_cparams("parallel"), name="mla_prep",
    )(h_mla, cos_t, sin_t, *[p[n].arr for n in names])


def _mla_prompt_body(q_ref, k_ref, c_ref, wuv_ref, o_ref, m_ref, l_ref, acc_ref, olat_ref, *, tq, tk):
    qi = pl.program_id(1)

    def causal(n_rows):
        return (lax.broadcasted_iota(jnp.int32, (n_rows, tk), 1)
                <= lax.broadcasted_iota(jnp.int32, (n_rows, tk), 0))

    m_ref[...] = jnp.full(m_ref.shape, -jnp.inf, F32)
    l_ref[...] = jnp.zeros(l_ref.shape, F32)
    acc_ref[...] = jnp.zeros(acc_ref.shape, F32)

    def step(j, r0, masked):
        start = pl.multiple_of(j * tk, tk)
        rows = slice(r0, tq)
        cblk = c_ref[0, pl.ds(start, tk), :]
        for hd in range(MLA_HEADS):
            sl = slice(hd * LANES, (hd + 1) * LANES)
            s = _dot_nt(q_ref[0, rows, sl], k_ref[0, pl.ds(start, tk), sl])
            if masked:
                s = jnp.where(causal(tq - r0), s, -jnp.inf)
            m_old = m_ref[hd, rows, :]
            m_new = jnp.maximum(m_old, jnp.max(s, axis=-1, keepdims=True))
            p = jnp.exp2(s - jnp.concatenate([m_new] * (tk // LANES), axis=1))
            corr = jnp.exp2(m_old - m_new)
            pv = _dot(p.astype(BF16), cblk)
            acc_ref[hd, rows, :] = acc_ref[hd, rows, :] * corr + pv[:, :LANES]
            l_ref[hd, rows, :] = l_ref[hd, rows, :] * corr + pv[:, LANES:]
            m_ref[hd, rows, :] = m_new

    def body(j, carry):
        step(j, 0, False)
        return carry

    n_full = qi * (tq // tk)
    lax.fori_loop(0, n_full, body, 0)
    for d in range(tq // tk):
        step(n_full + d, d * tk, True)
    for hd in range(MLA_HEADS):
        olat_ref[:, hd * LANES:(hd + 1) * LANES] = (acc_ref[hd] / l_ref[hd]).astype(BF16)
    o_ref[0] = _dot(olat_ref[...], wuv_ref[...]).astype(o_ref.dtype)


def mla_prompt(q_cat, k_cat, c_bf, wuv_bd, tq=256, tk=256):
    b, L, hw = q_cat.shape
    assert tq % tk == 0 and L % tq == 0
    return pl.pallas_call(
        functools.partial(_mla_prompt_body, tq=tq, tk=tk), grid=(b, L // tq),
        in_specs=[pl.BlockSpec((1, tq, hw), lambda bi, qi: (bi, qi, 0)),
                  pl.BlockSpec((1, L, hw), lambda bi, qi: (bi, 0, 0)),
                  pl.BlockSpec((1, L, 2 * LANES), lambda bi, qi: (bi, 0, 0)),
                  _wspec(wuv_bd)],
        out_specs=pl.BlockSpec((1, tq, MLA_HEADS * MLA_V), lambda bi, qi: (bi, qi, 0)),
        out_shape=jax.ShapeDtypeStruct((b, L, MLA_HEADS * MLA_V), BF16),
        scratch_shapes=[pltpu.VMEM((MLA_HEADS, tq, LANES), F32), pltpu.VMEM((MLA_HEADS, tq, LANES), F32),
                        pltpu.VMEM((MLA_HEADS, tq, LANES), F32), pltpu.VMEM((tq, hw), BF16)],
        compiler_params=_cparams("parallel", "arbitrary"), name="mla_prompt",
    )(q_cat, k_cat, c_bf, wuv_bd.arr)


def _mem_kv_body(x_ref, wk_ref, wv_ref, m_ref, g_ref, k_out, v_out):
    xb = x_ref[...].astype(BF16)
    k = _dot(xb, wk_ref[...])
    msq = _dot((k * k).astype(BF16), m_ref[...])
    k_out[...] = k * lax.rsqrt(msq + EPS) * g_ref[...]
    v_out[...] = _dot(xb, wv_ref[...])


def mem_kv(mem2d, p, tm):
    rows, d = mem2d.shape
    row = lambda n: pl.BlockSpec((tm, n), lambda i: (i, 0))
    names = ("mem_wk", "mem_wv", "m64", "mem_k_gain")
    return pl.pallas_call(
        _mem_kv_body, grid=(rows // tm,),
        in_specs=[row(d)] + [_wspec(p[n]) for n in names],
        out_specs=[row(MEM_D), row(MEM_D)],
        out_shape=[jax.ShapeDtypeStruct((rows, MEM_D), F32)] * 2,
        compiler_params=_cparams("parallel"), name="mem_kv",
    )(mem2d, *[p[n].arr for n in names])


def _mem_attend_body(x_ref, *refs, nb, kv_t, n_pre):
    a_refs, w_refs = refs[:n_pre], refs[n_pre:2 * n_pre]
    g_ref, wq_ref, m_ref, gq_ref, k_ref, v_ref, wo_ref, o_ref = refs[2 * n_pre:]
    r = x_ref.shape[0]
    x = x_ref[...]
    for a_ref, w_ref in zip(a_refs, w_refs):
        x = x + _dot(a_ref[...], w_ref[...])
    q = _dot(_rms(x, g_ref[...]).astype(BF16), wq_ref[...])
    msq = _dot((q * q).astype(BF16), m_ref[...])
    qn = (q * lax.rsqrt(msq + EPS) * gq_ref[...]).astype(BF16)
    if kv_t:
        kb = jnp.concatenate([k_ref[b] for b in range(nb)], axis=1).astype(BF16)
        vb = jnp.concatenate([v_ref[b] for b in range(nb)], axis=1).astype(BF16)
    else:
        kb = k_ref[...].astype(BF16)
        vb = v_ref[...].astype(BF16)
    lane = lax.broadcasted_iota(jnp.int32, (r, MEM_D), 1) // MEM_HEAD_DIM
    if nb > 1:
        rb = lax.broadcasted_iota(jnp.int32, (r, nb * N_MEM), 0) // (r // nb)
        cb = lax.broadcasted_iota(jnp.int32, (r, nb * N_MEM), 1) // N_MEM
        same = rb == cb
    o = jnp.zeros((r, MEM_D), F32)
    for hd in range(MEM_HEADS):
        qm = jnp.where(lane == hd, qn, jnp.zeros_like(qn))
        s = _dot(qm, kb) if kv_t else _dot_nt(qm, kb)
        if nb > 1:
            s = jnp.where(same, s, -jnp.inf)
        pr = jnp.exp(s - jnp.max(s, axis=-1, keepdims=True))
        prb = pr.astype(BF16)
        pv = (_dot_nt(prb, vb) if kv_t else _dot(prb, vb)) / jnp.sum(pr, axis=-1, keepdims=True)
        o = jnp.where(lane == hd, pv, o)
    o_ref[...] = x + _dot(o.astype(BF16), wo_ref[...])


def mem_attend(x, acts, ws, k, v, p, r, nb, rows_per_batch, kv_t, layer=0):
    rows, d = x.shape
    row = lambda n: pl.BlockSpec((r, n), lambda i: (i, 0))
    names = ("norm_mem", "mem_wq", "m64", "mem_q_gain")
    if kv_t:
        kv_spec = pl.BlockSpec((None, nb, MEM_D, N_MEM), lambda i: (layer, i, 0, 0))
    elif nb == 1:
        kv_spec = pl.BlockSpec((N_MEM, MEM_D), lambda i: (i // (rows_per_batch // r), 0))
    else:
        kv_spec = pl.BlockSpec((nb * N_MEM, MEM_D), lambda i: (i, 0))
    return pl.pallas_call(
        functools.partial(_mem_attend_body, nb=nb, kv_t=kv_t, n_pre=len(acts)), grid=(rows // r,),
        in_specs=[row(d)] + [row(a.shape[1]) for a in acts] + [_wspec(w) for w in ws]
        + [_wspec(p[n]) for n in names] + [kv_spec, kv_spec, _wspec(p["mem_wo"])],
        out_specs=row(d),
        out_shape=jax.ShapeDtypeStruct((rows, d), F32),
        compiler_params=_cparams("parallel"), name="mem_attend",
    )(x, *acts, *[w.arr for w in ws], *[p[n].arr for n in names], k, v, p["mem_wo"].arr)


def _ffn_body(x_ref, g_ref, *refs, carry, tiles_per_seq, seg, tail_rows):
    if carry:
        wup_ref, wd_ref, cw_ref, cb_ref, o_ref, tail_ref, xn_ref, carry_ref = refs
        halo_ref = None
    else:
        halo_ref, wup_ref, wd_ref, cw_ref, cb_ref, o_ref, tail_ref, xn_ref = refs
        carry_ref = None
    tm = x_ref.shape[0]
    x = x_ref[...]
    xn_ref[...] = _rms(x, g_ref[...]).astype(BF16)
    row = lax.broadcasted_iota(jnp.int32, (tm, FFN_CHUNK), 0)
    if carry:
        @pl.when(pl.program_id(0) % tiles_per_seq == 0)
        def _():
            carry_ref[...] = jnp.zeros_like(carry_ref)
    else:
        t = row % seg
    n_chunks = D_FF // FFN_CHUNK

    def up(j):
        sl = slice(j * FFN_CHUNK, (j + 1) * FFN_CHUNK)
        xn = xn_ref[...]
        return (_dot(xn, wup_ref[:, sl]),
                _dot(xn, wup_ref[:, D_FF + j * FFN_CHUNK:D_FF + (j + 1) * FFN_CHUNK]))

    acc = x
    uv = up(0)
    for j in range(n_chunks):
        sl = slice(j * FFN_CHUNK, (j + 1) * FFN_CHUNK)
        u, v = uv
        if j + 1 < n_chunks:
            uv = up(j + 1)
        um1 = pltpu.roll(u, 1, 0)
        um2 = pltpu.roll(u, 2, 0)
        if carry:
            c6 = carry_ref[6:7, sl]
            c7 = carry_ref[7:8, sl]
            um1 = jnp.where(row == 0, c7, um1)
            um2 = jnp.where(row == 0, c6, jnp.where(row == 1, c7, um2))
            carry_ref[:, sl] = u[tm - SUBLANES:, :]
        else:
            hal = halo_ref[:, sl]
            um1 = jnp.where(t >= 1, um1, pltpu.roll(hal, tm - 1, 0))
            um2 = jnp.where(t >= 2, um2, hal)
        tail_ref[:, sl] = u[tm - tail_rows:, :]
        uc = um2 * cw_ref[0:1, sl] + um1 * cw_ref[1:2, sl] + u * cw_ref[2:3, sl] + cb_ref[:, sl]
        a = (_silu(uc) * v).astype(BF16)
        acc = acc + _dot(a, wd_ref[sl, :])
    o_ref[...] = acc


def ffn(x, halo, p, tm, carry, tiles_per_seq, seg, tail_rows):
    rows, d = x.shape
    n_tiles = rows // tm
    row_spec = lambda n: pl.BlockSpec((tm, n), lambda i: (i, 0))
    in_specs = [row_spec(d), _wspec(p["norm_ffn"])]
    args = [x, p["norm_ffn"].arr]
    if not carry:
        in_specs.append(row_spec(D_FF))
        args.append(halo)
    in_specs += [_wspec(p[n], resident=True) for n in ("ffn_wup", "ffn_wd")]
    in_specs += [_wspec(p["ffn_conv_w"]), _wspec(p["ffn_conv_b"])]
    args += [p[n].arr for n in ("ffn_wup", "ffn_wd", "ffn_conv_w", "ffn_conv_b")]
    scratch = [pltpu.VMEM((tm, d), BF16)]
    if carry:
        scratch.append(pltpu.VMEM((SUBLANES, D_FF), F32))
    return pl.pallas_call(
        functools.partial(_ffn_body, carry=carry, tiles_per_seq=tiles_per_seq, seg=seg, tail_rows=tail_rows),
        grid=(n_tiles,), in_specs=in_specs,
        out_specs=[row_spec(d), pl.BlockSpec((tail_rows, D_FF), lambda i: (i, 0))],
        out_shape=[jax.ShapeDtypeStruct((rows, d), F32), jax.ShapeDtypeStruct((n_tiles * tail_rows, D_FF), F32)],
        scratch_shapes=scratch,
        compiler_params=_cparams("arbitrary"), name="ffn",
    )(*args)


def _ssd_body(*refs, ns, nb, seg, carry, has_state):
    z_ref, xbc_ref, dtr_ref = refs[0:3]
    i = 3
    halo_ref = None
    if not carry:
        halo_ref = refs[i]
        i += 1
    consts = refs[i:i + 7]
    i += 7
    h0_ref = None
    if has_state:
        h0_ref = refs[i]
        i += 1
    o_ref, h_ref = refs[i:i + 2]
    tail_ref = refs[i + 2] if carry else None

    @pl.when(pl.program_id(1) == 0)
    def _():
        h_ref[...] = h0_ref[...] if has_state else jnp.zeros_like(h_ref)
        if carry:
            tail_ref[...] = jnp.zeros_like(tail_ref)

    for s in range(ns):
        _ssd_stream(z_ref.at[s], xbc_ref.at[s], dtr_ref.at[s], None if carry else halo_ref.at[s], consts,
                    o_ref.at[s], h_ref.at[s], tail_ref.at[s] if carry else None, nb=nb, seg=seg, carry=carry)


def _ssd_stream(z_ref, xbc_ref, dtr_ref, halo_ref, consts, o_ref, h_ref, tail_ref, *, nb, seg, carry):
    cw_ref, cb_ref, dtb_ref, alog_ref, d_ref, ng_ref, rep_ref = consts
    r = z_ref.shape[0]
    hp = SSD_HEADS * SSD_HEADDIM
    gp = hp // SSD_GROUPS

    x = xbc_ref[...]
    row = lax.broadcasted_iota(jnp.int32, (r, SSD_CONV_DIM), 0)
    row8 = lax.broadcasted_iota(jnp.int32, (SUBLANES, SSD_CONV_DIM), 0)

    def prev(k):
        xs = pltpu.roll(x, k, 0)
        if carry:
            tl = pltpu.roll(tail_ref[...], k, 0)
            top = jnp.where(row8 < k, tl, xs[:SUBLANES])
            return jnp.concatenate([top, xs[SUBLANES:]], axis=0)
        sh = SSD_CONV - 1 - k
        hal = halo_ref[...]
        hs_ = hal if sh == 0 else pltpu.roll(hal, r - sh, 0)
        return jnp.where(row % seg >= k, xs, hs_)

    u = (cb_ref[...] + x * cw_ref[3:4, :] + prev(1) * cw_ref[2:3, :] + prev(2) * cw_ref[1:2, :]
         + prev(3) * cw_ref[0:1, :])
    if carry:
        tail_ref[...] = x[r - SUBLANES:, :]
    u = _silu(u)

    ri = lax.broadcasted_iota(jnp.int32, (r, r), 0)
    ci = lax.broadcasted_iota(jnp.int32, (r, r), 1)
    if nb > 1:
        tril = jnp.logical_and((ri // seg) == (ci // seg), ci <= ri)
    else:
        tril = ci <= ri
    tri = jnp.where(tril, 1.0, 0.0).astype(BF16)

    dtc = _softplus(dtr_ref[...] + dtb_ref[...])
    csc = _dot_sel(tri, dtc * (-jnp.exp(alog_ref[...])))
    dt = _dot_rep(dtc, rep_ref[...])
    cs = _dot_rep(csc, rep_ref[...])
    tot = _seg_total(cs, nb, seg) if nb > 1 else _seg_total(cs, 1, r)

    lane = lax.broadcasted_iota(jnp.int32, (r, LANES), 1)
    rb = lax.broadcasted_iota(jnp.int32, (r, LANES), 0) // seg
    lo = lane < SSD_HEADDIM

    def pair(a, b):
        return jnp.where(lo, a, b)

    def hs(a, hd):
        return a[:, hd * LANES:(hd + 1) * LANES]

    zz = z_ref[...]
    hall = h_ref[...].astype(BF16)
    for g in range(SSD_GROUPS):
        h0i, h1i = 2 * g, 2 * g + 1
        gs = slice(g * LANES, (g + 1) * LANES)
        xg = u[:, gs]
        bg = u[:, SSD_D + g * SSD_STATE:SSD_D + (g + 1) * SSD_STATE].astype(BF16)
        cg = u[:, SSD_D + SSD_GROUPS * SSD_STATE + g * SSD_STATE:
               SSD_D + SSD_GROUPS * SSD_STATE + (g + 1) * SSD_STATE].astype(BF16)
        cbm = _dot_nt(cg, bg)
        w0 = (cbm * jnp.where(tril, jnp.exp(hs(cs, h0i) - hs(cs, h0i).T), 0.0)).astype(BF16)
        w1 = (cbm * jnp.where(tril, jnp.exp(hs(cs, h1i) - hs(cs, h1i).T), 0.0)).astype(BF16)
        dtp = pair(hs(dt, h0i), hs(dt, h1i))
        csp = pair(hs(cs, h0i), hs(cs, h1i))
        totp = pair(hs(tot, h0i), hs(tot, h1i))
        xdt = (xg * dtp).astype(BF16)
        y = pair(_dot(w0, xdt), _dot(w1, xdt))
        zst = _dot_nt(cg, hall)
        if nb == 1:
            yst = zst[:, g * gp:(g + 1) * gp]
        else:
            yst = jnp.zeros((r, gp), F32)
            for b in range(nb):
                yst = jnp.where(rb == b, zst[:, b * hp + g * gp:b * hp + (g + 1) * gp], yst)
        y = y + yst * jnp.exp(csp) + d_ref[:, gs] * xg
        y = y * _silu(zz[:, gs])
        o_ref[:, gs] = _rms(y, ng_ref[:, gs]).astype(o_ref.dtype)

        xw = (xg * (jnp.exp(totp - csp) * dtp)).astype(BF16)
        if nb > 1:
            xw = jnp.concatenate([jnp.where(rb == b, xw, jnp.zeros_like(xw)) for b in range(nb)], axis=1)
        dh = _dot_tn(xw, bg)
        for b in range(nb):
            r0 = b * seg if nb > 1 else 0
            dec = jnp.concatenate(
                [jnp.broadcast_to(jnp.exp(hs(tot, h0i)[r0:r0 + 1, :]), (SSD_HEADDIM, LANES)),
                 jnp.broadcast_to(jnp.exp(hs(tot, h1i)[r0:r0 + 1, :]), (SSD_HEADDIM, LANES))], axis=0)
            sl = slice(b * hp + g * gp, b * hp + (g + 1) * gp)
            h_ref[sl, :] = h_ref[sl, :] * dec + dh[b * gp:(b + 1) * gp, :]


def ssd(z, xbc, dtr, halo, h0, p, nb, seg, ns, layer=0):
    n_groups, lg, _ = z.shape
    r = SSD_CHUNK
    carry = halo is None
    n_chunks = lg // r
    assert lg % r == 0 and n_groups % ns == 0 and (carry or n_chunks == 1)
    hp = SSD_HEADS * SSD_HEADDIM
    row_spec = lambda n: pl.BlockSpec((ns, r, n), lambda b, c: (b, c, 0))
    st_spec = pl.BlockSpec((ns, nb * hp, SSD_STATE), lambda b, c: (b, 0, 0))
    in_specs = [row_spec(SSD_D), row_spec(SSD_CONV_DIM), row_spec(LANES)]
    args = [z, xbc, dtr]
    if not carry:
        in_specs.append(row_spec(SSD_CONV_DIM))
        args.append(halo)
    names = ("ssd_conv_w", "ssd_conv_b", "ssd_dt_bias", "ssd_a_log", "ssd_d", "ssd_norm", "ssd_rep")
    in_specs += [_wspec(p[n]) for n in names]
    args += [p[n].arr for n in names]
    if h0 is not None:
        in_specs.append(pl.BlockSpec((None, ns, nb * hp, SSD_STATE), lambda b, c: (layer, b, 0, 0)))
        args.append(h0)
    scratch = [pltpu.VMEM((ns, SUBLANES, SSD_CONV_DIM), F32)] if carry else []
    return pl.pallas_call(
        functools.partial(_ssd_body, ns=ns, nb=nb, seg=seg, carry=carry, has_state=h0 is not None),
        grid=(n_groups // ns, n_chunks), in_specs=in_specs,
        out_specs=[row_spec(SSD_D), st_spec],
        out_shape=[jax.ShapeDtypeStruct((n_groups, lg, SSD_D), BF16),
                   jax.ShapeDtypeStruct((n_groups, nb * hp, SSD_STATE), F32)],
        scratch_shapes=scratch,
        compiler_params=_cparams("parallel", "arbitrary"), name="ssd",
    )(*args)


def _gla_body(*refs, ns, nb, seg, c, has_state):
    hg3_ref, wg_ref, bg_ref, m_ref, ng_ref = refs[0:5]
    i = 5
    s0_ref = None
    if has_state:
        s0_ref = refs[i]
        i += 1
    o3_ref, stc_ref, st3_ref = refs[i:i + 3]
    r = hg3_ref.shape[1]

    def head_blocks():
        for s in range(ns):
            for b in range(nb):
                for hd in range(GLA_HEADS):
                    yield s, slice(b * GLA_VD + hd * GLA_DV, b * GLA_VD + (hd + 1) * GLA_DV), \
                        slice(hd * GLA_DK, (hd + 1) * GLA_DK)

    @pl.when(pl.program_id(1) == 0)
    def _():
        st3_ref[...] = jnp.zeros_like(st3_ref)
        if has_state:
            for s, rows, lanes in head_blocks():
                st3_ref[s, rows, lanes] = s0_ref[s, rows, :]

    sl_ = seg if nb > 1 else c
    ri = lax.broadcasted_iota(jnp.int32, (r, r), 0)
    ci = lax.broadcasted_iota(jnp.int32, (r, r), 1)
    same = (ri // sl_) == (ci // sl_)
    tril = jnp.logical_and(same, ci <= ri)
    tri = jnp.where(tril, 1.0, 0.0).astype(BF16)
    klane = lax.broadcasted_iota(jnp.int32, (r, GLA_KD), 1) // GLA_DK
    vlane = lax.broadcasted_iota(jnp.int32, (r, GLA_VD), 1) // GLA_DV
    rbv = lax.broadcasted_iota(jnp.int32, (r, GLA_VD), 0) // seg
    blk = (lax.broadcasted_iota(jnp.int32, (GLA_VD, GLA_KD), 0) // GLA_DV
           == lax.broadcasted_iota(jnp.int32, (GLA_VD, GLA_KD), 1) // GLA_DK)

    for s in range(ns):
        hg_ref, o_ref, st_ref = hg3_ref.at[s], o3_ref.at[s], st3_ref.at[s]
        q = hg_ref[:, 0:GLA_KD] * (GLA_DK ** -0.5)
        k = hg_ref[:, GLA_KD:2 * GLA_KD]
        vb = hg_ref[:, 2 * GLA_KD:2 * GLA_KD + GLA_VD].astype(BF16)
        gg = hg_ref[:, 2 * GLA_KD + GLA_VD:2 * GLA_KD + 2 * GLA_VD]
        glr = hg_ref[:, 2 * GLA_KD + 2 * GLA_VD:2 * GLA_KD + 2 * GLA_VD + LANES].astype(BF16)
        gate = _log_sigmoid(_dot(glr, wg_ref[...]) + bg_ref[...]) * (1.0 / GLA_GATE_TAU)
        bc = _dot_sel(tri, gate)
        tot = _seg_total(bc, r // sl_, sl_)
        qt = (q * jnp.exp(bc)).astype(BF16)
        kt = (k * jnp.exp(-bc)).astype(BF16)
        kd = (k * jnp.exp(tot - bc)).astype(BF16)
        etot = jnp.exp(tot)
        o = jnp.zeros((r, GLA_VD), F32)
        for hd in range(GLA_HEADS):
            a = _dot_nt(jnp.where(klane == hd, qt, jnp.zeros_like(qt)), kt)
            a = jnp.where(tril, a, 0.0).astype(BF16)
            o = o + _dot(a, jnp.where(vlane == hd, vb, jnp.zeros_like(vb)))
        if nb > 1:
            zs = _dot_nt(qt, st_ref[...].astype(BF16))
            ost = jnp.zeros((r, GLA_VD), F32)
            for b in range(nb):
                ost = jnp.where(rbv == b, zs[:, b * GLA_VD:(b + 1) * GLA_VD], ost)
            o = o + ost
            vexp = jnp.concatenate([jnp.where(rbv == b, vb, jnp.zeros_like(vb)) for b in range(nb)], axis=1)
            ds = _dot_tn(vexp, kd)
            for b in range(nb):
                sl = slice(b * GLA_VD, (b + 1) * GLA_VD)
                st_ref[sl, :] = st_ref[sl, :] * etot[b * seg:b * seg + 1, :] + jnp.where(blk, ds[sl, :], 0.0)
        else:
            st = st_ref[...]
            parts = []
            for sub in range(r // c):
                rs = slice(sub * c, (sub + 1) * c)
                parts.append(_dot_nt(qt[rs, :], st.astype(BF16)))
                st = st * etot[sub * c:sub * c + 1, :] + jnp.where(blk, _dot_tn(vb[rs, :], kd[rs, :]), 0.0)
            st_ref[...] = st
            o = o + jnp.concatenate(parts, axis=0)
        msq = _dot((o * o).astype(BF16), m_ref[...])
        on = o * lax.rsqrt(msq + EPS) * ng_ref[...]
        o_ref[...] = (on * _silu(gg)).astype(o_ref.dtype)

    @pl.when(pl.program_id(1) == pl.num_programs(1) - 1)
    def _():
        for s, rows, lanes in head_blocks():
            stc_ref[s, rows, :] = st3_ref[s, rows, lanes]


def gla(hg, s0, p, nb, seg, c, r, ns):
    n_groups, lg, width = hg.shape
    n_steps = lg // r
    assert lg % r == 0 and r % c == 0 and n_groups % ns == 0
    row_spec = lambda n: pl.BlockSpec((ns, r, n), lambda b, s: (b, s, 0))
    st_spec = pl.BlockSpec((ns, nb * GLA_VD, GLA_DK), lambda b, s: (b, 0, 0))
    names = ("gla_w_gate", "gla_b_gate", "m64", "gla_norm")
    in_specs = [row_spec(width)] + [_wspec(p[n]) for n in names]
    args = [hg] + [p[n].arr for n in names]
    if s0 is not None:
        in_specs.append(st_spec)
        args.append(s0)
    return pl.pallas_call(
        functools.partial(_gla_body, ns=ns, nb=nb, seg=seg, c=c, has_state=s0 is not None),
        grid=(n_groups // ns, n_steps), in_specs=in_specs,
        out_specs=[row_spec(GLA_VD), st_spec],
        out_shape=[jax.ShapeDtypeStruct((n_groups, lg, GLA_VD), BF16),
                   jax.ShapeDtypeStruct((n_groups, nb * GLA_VD, GLA_DK), F32)],
        scratch_shapes=[pltpu.VMEM((ns, nb * GLA_VD, GLA_KD), F32)],
        compiler_params=_cparams("parallel", "arbitrary"), name="gla",
    )(*args)


def _q_absorb_body(q_ref, wt_ref, sel_ref, qa_out, qr_out):
    for hd in range(MLA_HEADS):
        qh = q_ref[:, hd * LANES:(hd + 1) * LANES].astype(F32)
        qa_out[:, hd * LANES:(hd + 1) * LANES] = _dot(qh, wt_ref[hd], HI).astype(BF16)
        qr_out[:, hd * LANES:(hd + 1) * LANES] = _dot(qh, sel_ref[...], HI).astype(BF16)


def q_absorb(q_cat, p, tm):
    rows, hw = q_cat.shape
    row_spec = pl.BlockSpec((tm, hw), lambda i: (i, 0))
    return pl.pallas_call(
        _q_absorb_body, grid=(rows // tm,),
        in_specs=[row_spec, _wspec(p["wukt_g"]), _wspec(p["rope_sel"])],
        out_specs=[row_spec, row_spec],
        out_shape=[jax.ShapeDtypeStruct((rows, hw), BF16)] * 2,
        compiler_params=_cparams("parallel"), name="q_absorb",
    )(q_cat, p["wukt_g"].arr, p["rope_sel"].arr)


PAGED_CP = 64
PAGED_SUB = 512


def _mla_paged_body(pt_ref, ptn_ref, wukt_ref, qa_ref, qr_ref, cnew_ref, krnew_ref, wuv_ref, lat_hbm, kr_hbm,
                    o_ref, lat_buf, kr_buf, sem, m_ref, l_ref, acc_ref, *, layer, n_pages, n_batch, seq, cp):
    bi = pl.program_id(0)
    nc = n_pages // cp
    nq = MLA_HEADS * seq
    nk = MLA_HEADS * MLA_NOPE

    def copies(table, chunk, slot):
        out = []
        for pg in range(cp):
            pid = table[0, 0, chunk * cp + pg]
            dst = pl.ds(pg * PAGE_SIZE, PAGE_SIZE)
            out.append(pltpu.make_async_copy(lat_hbm.at[layer, pid], lat_buf.at[slot, dst, :], sem.at[0, slot]))
            out.append(pltpu.make_async_copy(kr_hbm.at[layer, pid], kr_buf.at[slot, :, dst], sem.at[1, slot]))
        return out

    def start(table, chunk, slot):
        for cpy in copies(table, chunk, slot):
            cpy.start()

    def wait(slot):
        for cpy in copies(pt_ref, 0, slot):
            cpy.wait()

    @pl.when(bi == 0)
    def _():
        start(pt_ref, 0, 0)

    m_ref[...] = jnp.full(m_ref.shape, -jnp.inf, F32)
    l_ref[...] = jnp.zeros(l_ref.shape, F32)
    acc_ref[...] = jnp.zeros(acc_ref.shape, F32)
    lhs = jnp.concatenate([wukt_ref[...], qa_ref[0]], axis=0)
    qr = qr_ref[0]

    def scores(latb, krt):
        big = _dot_nt(lhs, latb)
        kt = big[0:nk, :]
        ss = jnp.sum((kt * kt).reshape(MLA_HEADS, MLA_NOPE, kt.shape[1]), axis=1) * (1.0 / MLA_NOPE)
        rinv = lax.rsqrt(ss + EPS)
        rexp = jnp.concatenate([jnp.broadcast_to(rinv[hd:hd + 1, :], (seq, rinv.shape[1]))
                                for hd in range(MLA_HEADS)], axis=0)
        return big[nk:, :] * rexp + _dot(qr, krt.astype(BF16))

    def attend(lat, krt, mask):
        keys = lat.shape[0]
        sub = min(PAGED_SUB, keys)
        latb = lat.astype(BF16)
        s = jnp.concatenate([scores(latb[i * sub:(i + 1) * sub, :], krt[:, i * sub:(i + 1) * sub])
                             for i in range(keys // sub)], axis=1)
        if mask is not None:
            s = jnp.where(mask, s, -jnp.inf)
        m_old = m_ref[...]
        m_new = jnp.maximum(m_old, jnp.max(s, axis=-1, keepdims=True))
        pr = jnp.exp2(s - jnp.concatenate([m_new] * (keys // LANES), axis=1))
        corr = jnp.exp2(m_old - m_new)
        l_ref[...] = l_ref[...] * corr + jnp.sum(pr, axis=-1, keepdims=True)
        acc_ref[...] = acc_ref[...] * corr + _dot(pr.astype(BF16), latb)
        m_ref[...] = m_new

    lat_new = jnp.concatenate([cnew_ref[0], jnp.zeros((LANES - seq, MLA_KV_RANK), F32)], axis=0)
    qtok = lax.broadcasted_iota(jnp.int32, (nq, LANES), 0) % seq
    key = lax.broadcasted_iota(jnp.int32, (nq, LANES), 1)
    attend(lat_new, krnew_ref[0], key <= qtok)

    def step(slot, prefetch):
        prefetch()
        wait(slot)
        attend(lat_buf[slot], kr_buf[slot], None)

    def pair_body(jj, carry_):
        step(0, lambda: start(pt_ref, 2 * jj + 1, 1))
        step(1, lambda: start(pt_ref, 2 * jj + 2, 0))
        return carry_

    lax.fori_loop(0, nc // 2 - 1, pair_body, 0)
    step(0, lambda: start(pt_ref, nc - 1, 1))

    def next_batch_prefetch():
        @pl.when(bi + 1 < n_batch)
        def _():
            start(ptn_ref, 0, 0)

    step(1, next_batch_prefetch)

    olat = (acc_ref[...] / l_ref[...]).astype(BF16)
    o = jnp.zeros((seq, MLA_HEADS * MLA_V), F32)
    for hd in range(MLA_HEADS):
        o = o + _dot(olat[hd * seq:(hd + 1) * seq, :], wuv_ref[hd * LANES:(hd + 1) * LANES, :])
    o_ref[0] = o.astype(o_ref.dtype)


def mla_paged(page_table, wukt, qa, qr, c_new, krt_new, wuv_bd, cache_lat, cache_krt, layer):
    n_batch, n_pages = page_table.shape
    seq = c_new.shape[1]
    cp = min(PAGED_CP, n_pages // 2)
    assert n_pages % (2 * cp) == 0
    nq = MLA_HEADS * seq
    pt3 = page_table.reshape(n_batch, 1, n_pages)
    smem_spec = lambda f: pl.BlockSpec((1, 1, n_pages), f, memory_space=pltpu.SMEM)
    per_b = lambda shp: pl.BlockSpec((1,) + shp, lambda b: (b, 0, 0))
    kc = cp * PAGE_SIZE
    kr_new, cache_kr = krt_new, cache_krt
    return pl.pallas_call(
        functools.partial(_mla_paged_body, layer=layer, n_pages=n_pages, n_batch=n_batch, seq=seq, cp=cp),
        grid=(n_batch,),
        in_specs=[smem_spec(lambda b: (b, 0, 0)),
                  smem_spec(lambda b: (jnp.minimum(b + 1, n_batch - 1), 0, 0)),
                  _wspec(wukt), per_b(qa.shape[1:]), per_b(qr.shape[1:]), per_b(c_new.shape[1:]),
                  per_b(kr_new.shape[1:]),
                  _wspec(wuv_bd),
                  pl.BlockSpec(memory_space=pl.ANY), pl.BlockSpec(memory_space=pl.ANY)],
        out_specs=per_b((seq, MLA_HEADS * MLA_V)),
        out_shape=jax.ShapeDtypeStruct((n_batch, seq, MLA_HEADS * MLA_V), BF16),
        scratch_shapes=[pltpu.VMEM((2, kc, MLA_KV_RANK), F32), pltpu.VMEM((2, MLA_ROPE, kc), F32),
                        pltpu.SemaphoreType.DMA((2, 2)),
                        pltpu.VMEM((nq, LANES), F32), pltpu.VMEM((nq, LANES), F32), pltpu.VMEM((nq, LANES), F32)],
        compiler_params=_cparams("arbitrary"), name="mla_paged",
    )(pt3, pt3, wukt.arr, qa, qr, c_new, kr_new, wuv_bd.arr, cache_lat, cache_kr)


def _block_diag_mean(n, blocks):
    idx = jnp.arange(n)
    m = jnp.zeros((n, n), F32)
    for start, size in blocks:
        inb = jnp.logical_and(idx >= start, idx < start + size)
        m = m + jnp.where(jnp.logical_and(inb[:, None], inb[None, :]), 1.0 / size, 0.0)
    return m.astype(BF16)


def _rope_tables(pos):
    half = MLA_ROPE // 2
    inv = ROPE_THETA ** (-jnp.arange(half, dtype=F32) / half)
    ang = pos.astype(F32)[:, None] * inv[None, :]
    cos, sin = jnp.cos(ang), jnp.sin(ang)
    n = pos.shape[0]
    one, zero = jnp.ones((n, MLA_NOPE), F32), jnp.zeros((n, MLA_NOPE), F32)
    tail1, tail0 = jnp.ones((n, LANES - MLA_QK), F32), jnp.zeros((n, LANES - MLA_QK), F32)
    return (jnp.concatenate([one, cos, cos, tail1], axis=1), jnp.concatenate([zero, -sin, sin, tail0], axis=1))


def _pack_one(w):
    f = lambda name: w[name]
    bf = lambda a: a.astype(BF16)
    p = {}
    w_in = f("w_in")
    o = 0
    cols = {}
    for name, size in (("cq", MLA_Q_RANK), ("ckv", MLA_KV_RANK), ("krr", MLA_ROPE), ("z", SSD_D), ("xbc", SSD_CONV_DIM),
                       ("dt", SSD_HEADS), ("gq", GLA_KD), ("gk", GLA_KD), ("gv", GLA_VD), ("glr", GLA_GATE_RANK),
                       ("gg", GLA_VD)):
        cols[name] = w_in[:, o:o + size]
        o += size
    d = w_in.shape[0]

    def place(a, axis, lo, total):
        cfg = [(0, 0)] * a.ndim
        cfg[axis] = (lo, total - lo - a.shape[axis])
        return jnp.pad(a, cfg)

    krr_pad = place(cols["krr"], 1, MLA_NOPE, LANES)
    glr_pad = place(cols["glr"], 1, 0, LANES)
    p["w_mla"] = bf(jnp.concatenate([cols["cq"], cols["ckv"], krr_pad], axis=1))
    p["w_z"] = bf(cols["z"])
    p["w_xbc"] = bf(cols["xbc"])
    p["w_dt"] = bf(place(cols["dt"], 1, 0, LANES))
    p["w_gla"] = bf(jnp.concatenate([cols["gq"], cols["gk"], cols["gv"], cols["gg"], glr_pad], axis=1))
    p["norm_mix"] = f("norm_mix").reshape(1, d)

    def head_pad(wm, n_real):
        return place(wm, 2, 0, LANES).reshape(wm.shape[0], MLA_HEADS * LANES)

    p["wq_pad"] = bf(head_pad(f("mla_w_uq"), MLA_QK))
    p["wk_pad"] = bf(head_pad(f("mla_w_uk"), MLA_NOPE))
    scale = MLA_QK ** -0.5 * math.log2(math.e)
    zpad = jnp.zeros((LANES - MLA_QK,), F32)
    p["gq_vec"] = (jnp.concatenate([f("mla_qn_norm"), f("mla_qr_norm"), zpad]) * scale).reshape(1, LANES)
    p["gk_vec"] = jnp.concatenate([f("mla_kn_norm"), jnp.zeros((LANES - MLA_NOPE,), F32)]).reshape(1, LANES)
    p["gkr_vec"] = jnp.concatenate([jnp.zeros((MLA_NOPE,), F32), f("mla_kr_norm"), zpad]).reshape(1, LANES)
    p["mla_q_norm"] = f("mla_q_norm").reshape(1, MLA_Q_RANK)
    p["mla_kv_norm"] = f("mla_kv_norm").reshape(1, MLA_KV_RANK)
    p["mh"] = _block_diag_mean(LANES, [(0, MLA_NOPE), (MLA_NOPE, MLA_ROPE)])
    w_uv = f("mla_w_uv")
    p["wuv_bd"] = bf(jnp.concatenate(
        [place(w_uv[:, hd, :], 1, hd * MLA_V, MLA_HEADS * MLA_V) for hd in range(MLA_HEADS)], axis=0))
    w_uk = f("mla_w_uk")
    wt = jnp.transpose(w_uk, (1, 2, 0)) * f("mla_kn_norm")[None, :, None]
    p["wukt_g"] = place(wt, 1, 0, LANES)
    p["wukt"] = bf(jnp.transpose(w_uk, (1, 2, 0)).reshape(MLA_HEADS * MLA_NOPE, MLA_KV_RANK))
    lane_i = jnp.arange(LANES)
    p["rope_sel"] = jnp.where(jnp.logical_and(lane_i[:, None] == lane_i[None, :] + MLA_NOPE,
                                              lane_i[None, :] < MLA_ROPE), 1.0, 0.0).astype(F32)

    p["ssd_conv_w"] = place(f("ssd_conv_w"), 0, 0, SUBLANES)
    p["ssd_conv_b"] = f("ssd_conv_b").reshape(1, SSD_CONV_DIM)
    lane_pad = lambda v: place(v, 0, 0, LANES).reshape(1, LANES)
    p["ssd_dt_bias"] = lane_pad(f("ssd_dt_bias"))
    p["ssd_a_log"] = lane_pad(f("ssd_a_log"))
    p["ssd_rep"] = bf(jnp.repeat(jnp.eye(LANES, SSD_HEADS, dtype=F32), LANES, axis=1))
    p["ssd_d"] = jnp.repeat(f("ssd_d"), SSD_HEADDIM).reshape(1, SSD_D)
    p["ssd_norm"] = f("ssd_norm").reshape(1, SSD_D)
    p["gla_w_gate"] = bf(place(f("gla_w_gate"), 0, 0, LANES))
    p["gla_b_gate"] = f("gla_b_gate").reshape(1, GLA_KD)
    p["gla_norm"] = f("gla_norm").reshape(1, GLA_VD)
    p["m64"] = _block_diag_mean(MEM_D, [(i * 64, 64) for i in range(4)])
    w_o = f("w_o")
    p["wo_mla"] = bf(w_o[:MLA_HEADS * MLA_V])
    p["wo_ssd"] = bf(w_o[MLA_HEADS * MLA_V:MLA_HEADS * MLA_V + SSD_D])
    p["wo_gla"] = bf(w_o[MLA_HEADS * MLA_V + SSD_D:])
    p["norm_mem"] = f("norm_mem").reshape(1, D_MODEL)
    p["mem_wq"], p["mem_wk"], p["mem_wv"], p["mem_wo"] = bf(f("mem_wq")), bf(f("mem_wk")), bf(f("mem_wv")), bf(f("mem_wo"))
    p["mem_q_gain"] = (jnp.tile(f("mem_q_norm"), MEM_HEADS) * MEM_HEAD_DIM ** -0.5).reshape(1, MEM_D)
    p["mem_k_gain"] = jnp.tile(f("mem_k_norm"), MEM_HEADS).reshape(1, MEM_D)
    p["norm_ffn"] = f("norm_ffn").reshape(1, D_MODEL)
    w_up = f("ffn_w_up")
    p["ffn_wup"], p["ffn_wd"] = bf(w_up), bf(f("ffn_w_down"))
    p["ffn_conv_w"] = place(f("ffn_conv_w"), 0, 0, SUBLANES)
    p["ffn_conv_b"] = f("ffn_conv_b").reshape(1, D_FF)
    return p


def _pad_rows(state, seq):
    b, k, c = state.shape
    return jnp.pad(state, ((0, 0), (0, seq - k), (0, 0))).reshape(b * seq, c)


def _layer(x3, p, cos_t, sin_t, tm, sample, mem2d=None):
    b, L, d = x3.shape
    rows = b * L
    x = x3.reshape(rows, d)
    h_mla, z, xbc, dtr, h_gla = norm_matmul(
        x, p["norm_mix"], [p["w_mla"], p["w_z"], p["w_xbc"], p["w_dt"], p["w_gla"]], [F32] * 5, tm)
    q_cat, k_cat, c_kv, c_bf, kr = mla_prep(h_mla, cos_t, sin_t, p, tm)
    hw = MLA_HEADS * LANES
    if sample is None:
        o_mla = mla_prompt(q_cat.reshape(b, L, hw), k_cat.reshape(b, L, hw), c_bf.reshape(b, L, 2 * LANES),
                           p["wuv_bd"], math.gcd(L, 1024), 256)
        o_mla = o_mla.reshape(rows, MLA_HEADS * MLA_V)
        g3 = lambda a: a.reshape(b, L, a.shape[-1])
        ns = math.gcd(b, 4)
        o_ssd, ssd_h = ssd(g3(z), g3(xbc), g3(dtr), None, None, p, 1, L, ns)
        o_gla, gla_st = gla(g3(h_gla), None, p, 1, L, GLA_CHUNK, 256, ns)
    else:
        qa, qrp = q_absorb(q_cat, p, tm)
        to_hq = lambda a, n: a.reshape(b, L, MLA_HEADS, LANES)[..., :n].transpose(0, 2, 1, 3).reshape(b, MLA_HEADS * L, n)
        krt_new = jnp.pad(jnp.swapaxes(kr.reshape(b, L, MLA_ROPE), 1, 2), ((0, 0), (0, 0), (0, LANES - L)))
        o_mla = mla_paged(sample["page_table"], p["wukt"], to_hq(qa, LANES), to_hq(qrp, MLA_ROPE),
                          c_kv.reshape(b, L, LANES), krt_new, p["wuv_bd"], sample["cache_lat"], sample["cache_krt"],
                          sample["layer"])
        o_mla = o_mla.reshape(rows, MLA_HEADS * MLA_V)
        nb = SSD_CHUNK // L
        ng = b // nb
        g3 = lambda a: a.reshape(ng, nb * L, a.shape[-1])
        ns = math.gcd(ng, 2)
        o_ssd, ssd_h = ssd(g3(z), g3(xbc), g3(dtr), g3(_pad_rows(sample["ssd_conv"], L)),
                           sample["ssd_all"].reshape(-1, ng, nb * SSD_D, SSD_STATE), p, nb, L, ns, sample["layer"])
        st0 = jnp.swapaxes(sample["gla"], 2, 3).reshape(ng, nb * GLA_VD, GLA_DK)
        o_gla, gla_st = gla(g3(h_gla), st0, p, nb, L, nb * L, nb * L, ns)
    o_ssd, o_gla = o_ssd.reshape(rows, SSD_D), o_gla.reshape(rows, GLA_VD)
    mix, w_mix = [o_mla, o_ssd, o_gla], [p["wo_mla"], p["wo_ssd"], p["wo_gla"]]
    if sample is None:
        mk, mv = mem_kv(mem2d, p, 512)
        x = mem_attend(x, mix, w_mix, mk, mv, p, math.gcd(L, 1024), 1, L, False)
    else:
        nbm = 128 // L
        x = mem_attend(x, mix, w_mix, sample["mem_kt"], sample["mem_vt"], p, nbm * L, nbm, L, True,
                       sample["layer"])

    ft = min(rows, FFN_ROWS)
    if sample is None:
        x, u_tail = ffn(x, None, p, ft, True, L // ft, L, SUBLANES)
        ffn_conv = u_tail.reshape(b, L // ft, SUBLANES, D_FF)[:, -1, SUBLANES - (FFN_CONV - 1):, :]
    else:
        x, u_full = ffn(x, _pad_rows(sample["ffn_conv"], L), p, ft, False, 1, L, ft)
        ffn_conv = u_full.reshape(b, L, D_FF)[:, L - (FFN_CONV - 1):, :]

    xbc3 = xbc.reshape(b, L, SSD_CONV_DIM)
    out = dict(
        x=x.reshape(b, L, d), lat=c_kv.reshape(b, L, MLA_KV_RANK), kr=kr.reshape(b, L, MLA_ROPE),
        ssd_conv=xbc3[:, L - (SSD_CONV - 1):, :], ssd=ssd_h.reshape(b, SSD_HEADS, SSD_HEADDIM, SSD_STATE),
        gla=jnp.swapaxes(gla_st.reshape(b, GLA_HEADS, GLA_DV, GLA_DK), 2, 3),
        ffn_conv=ffn_conv)
    if sample is None:
        out["mem_k"] = mk.reshape(b, N_MEM, MEM_HEADS, MEM_HEAD_DIM)
        out["mem_v"] = mv.reshape(b, N_MEM, MEM_HEADS, MEM_HEAD_DIM)
    return out


def kernel(x_prompt, x_sample, cache_mla_latent, cache_mla_krope, cache_mem_k, cache_mem_v, state_ssd_conv, state_ssd, state_gla, state_ffn_conv, page_table, mem_prompt, norm_mix, w_in, mla_q_norm, mla_w_uq, mla_kv_norm, mla_w_uk, mla_w_uv, mla_qn_norm, mla_qr_norm, mla_kn_norm, mla_kr_norm, ssd_conv_w, ssd_conv_b, ssd_dt_bias, ssd_a_log, ssd_d, ssd_norm, gla_w_gate, gla_b_gate, gla_norm, w_o, norm_mem, mem_wq, mem_wk, mem_wv, mem_wo, mem_q_norm, mem_k_norm, norm_ffn, ffn_w_up, ffn_conv_w, ffn_conv_b, ffn_w_down):
    weights = dict(
        norm_mix=norm_mix, w_in=w_in, mla_q_norm=mla_q_norm, mla_w_uq=mla_w_uq, mla_kv_norm=mla_kv_norm,
        mla_w_uk=mla_w_uk, mla_w_uv=mla_w_uv, mla_qn_norm=mla_qn_norm, mla_qr_norm=mla_qr_norm,
        mla_kn_norm=mla_kn_norm, mla_kr_norm=mla_kr_norm, ssd_conv_w=ssd_conv_w, ssd_conv_b=ssd_conv_b,
        ssd_dt_bias=ssd_dt_bias, ssd_a_log=ssd_a_log, ssd_d=ssd_d, ssd_norm=ssd_norm, gla_w_gate=gla_w_gate,
        gla_b_gate=gla_b_gate, gla_norm=gla_norm, w_o=w_o, norm_mem=norm_mem, mem_wq=mem_wq, mem_wk=mem_wk,
        mem_wv=mem_wv, mem_wo=mem_wo, mem_q_norm=mem_q_norm, mem_k_norm=mem_k_norm, norm_ffn=norm_ffn,
        ffn_w_up=ffn_w_up, ffn_conv_w=ffn_conv_w, ffn_conv_b=ffn_conv_b, ffn_w_down=ffn_w_down)
    depth = w_in.shape[0]
    bp, lp, _ = x_prompt.shape
    bs, ls, _ = x_sample.shape
    assert ls >= SSD_CONV - 1 and SSD_CHUNK % ls == 0
    past_len = page_table.shape[1] * PAGE_SIZE
    tm_p, tm_s = 512, min(512, bs * ls)
    cos_p, sin_p = _rope_tables(jnp.arange(lp))
    cos_s, sin_s = _rope_tables(past_len + jnp.arange(ls))
    cos_s, sin_s = jnp.tile(cos_s, (tm_s // ls, 1)), jnp.tile(sin_s, (tm_s // ls, 1))
    mem2d = mem_prompt.reshape(bp * N_MEM, D_MODEL)
    cache_krt = jnp.swapaxes(cache_mla_krope, 2, 3)
    mem_kt = jnp.transpose(cache_mem_k, (0, 1, 3, 4, 2)).reshape(depth, bs, MEM_D, N_MEM)
    mem_vt = jnp.transpose(cache_mem_v, (0, 1, 3, 4, 2)).reshape(depth, bs, MEM_D, N_MEM)

    packed = jax.vmap(_pack_one)(weights)
    xp, xs = x_prompt, x_sample
    outs_p, outs_s = [], []
    for l in range(depth):
        p = {name: _Stacked(arr, l) for name, arr in packed.items()}
        op = _layer(xp, p, cos_p, sin_p, tm_p, None, mem2d)
        xp = op["x"]
        outs_p.append(op)
        sample = dict(layer=l, page_table=page_table, cache_lat=cache_mla_latent, cache_krt=cache_krt,
                      mem_kt=mem_kt, mem_vt=mem_vt, ssd_conv=state_ssd_conv[l], ssd_all=state_ssd,
                      gla=state_gla[l], ffn_conv=state_ffn_conv[l])
        os_ = _layer(xs, p, cos_s, sin_s, tm_s, sample)
        xs = os_["x"]
        outs_s.append(os_)

    st = lambda outs, key: jnp.stack([o[key] for o in outs])
    return (xp, xs,
            st(outs_p, "lat"), st(outs_p, "kr"), st(outs_p, "mem_k"), st(outs_p, "mem_v"),
            st(outs_p, "ssd_conv"), st(outs_p, "ssd"), st(outs_p, "gla"), st(outs_p, "ffn_conv"),
            st(outs_s, "lat"), st(outs_s, "kr"), st(outs_s, "ssd_conv"), st(outs_s, "ssd"), st(outs_s, "gla"),
            st(outs_s, "ffn_conv"))
```

```python
import functools
import math

import jax
import jax.numpy as jnp
from jax import lax
from jax.experimental import pallas as pl
from jax.experimental.pallas import tpu as pltpu

F32 = jnp.float32
BF16 = jnp.bfloat16
EPS = 1e-6
LANES = 128
SUBLANES = 8
VMEM_LIMIT = 56 * 1024 * 1024

D_MODEL = 1024
MLA_HEADS, MLA_NOPE, MLA_ROPE, MLA_V = 8, 64, 32, 64
MLA_QK = MLA_NOPE + MLA_ROPE
MLA_Q_RANK, MLA_KV_RANK = 256, 128
ROPE_THETA = 10000.0
PAGE_SIZE = 128
SSD_HEADS, SSD_HEADDIM, SSD_GROUPS, SSD_STATE, SSD_CONV = 4, 64, 2, 128, 4
SSD_D = SSD_HEADS * SSD_HEADDIM
SSD_CONV_DIM = SSD_D + 2 * SSD_GROUPS * SSD_STATE
SSD_CHUNK = 128
GLA_HEADS, GLA_DK, GLA_DV = 4, 32, 64
GLA_KD, GLA_VD = GLA_HEADS * GLA_DK, GLA_HEADS * GLA_DV
GLA_GATE_RANK, GLA_GATE_TAU, GLA_CHUNK = 16, 16.0, 64
N_MEM, MEM_HEADS, MEM_HEAD_DIM = 256, 4, 64
MEM_D = MEM_HEADS * MEM_HEAD_DIM
D_FF, FFN_CONV = 2816, 3
FFN_CHUNK = 256
FFN_ROWS = 256

HI = lax.Precision.HIGHEST


def _cparams(*sem):
    return pltpu.CompilerParams(dimension_semantics=sem, vmem_limit_bytes=VMEM_LIMIT)


def _dot(a, b, precision=None):
    return jnp.dot(a, b, preferred_element_type=F32, precision=precision)


def _dot_nt(a, b):
    return lax.dot_general(a, b, (((1,), (1,)), ((), ())), preferred_element_type=F32)


def _dot_tn(a, b):
    return lax.dot_general(a, b, (((0,), (0,)), ((), ())), preferred_element_type=F32)


def _split3(x):
    x1 = x.astype(BF16)
    r1 = x - x1.astype(F32)
    x2 = r1.astype(BF16)
    x3 = (r1 - x2.astype(F32)).astype(BF16)
    return x1, x2, x3


def _dot_sel(sel, x):
    return sum(_dot(sel, t) for t in _split3(x))


def _dot_rep(x, sel):
    return sum(_dot(t, sel) for t in _split3(x))


def _seg_total(cs, n_seg, seg_len):
    return jnp.concatenate(
        [jnp.broadcast_to(cs[(i + 1) * seg_len - 1:(i + 1) * seg_len, :], (seg_len, cs.shape[1]))
         for i in range(n_seg)], axis=0)


def _rms(x, g):
    return x * lax.rsqrt(jnp.mean(x * x, axis=-1, keepdims=True) + EPS) * g


def _sigmoid(x):
    return 1.0 / (1.0 + jnp.exp(-x))


def _silu(x):
    return x * _sigmoid(x)


def _softplus(x):
    return jnp.maximum(x, 0.0) + jnp.log(1.0 + jnp.exp(-jnp.abs(x)))


def _log_sigmoid(x):
    return jnp.minimum(x, 0.0) - jnp.log(1.0 + jnp.exp(-jnp.abs(x)))


class _Stacked:
    def __init__(self, arr, layer):
        self.arr, self.layer = arr, layer

    @property
    def shape(self):
        return self.arr.shape[1:]


def _wspec(w, resident=False):
    shp, layer = w.shape, w.layer
    kw = dict(pipeline_mode=pl.Buffered(1)) if resident else {}
    return pl.BlockSpec((None,) + shp, lambda *_: (layer,) + (0,) * len(shp), **kw)


def _in_proj_body(x_ref, g_ref, wm_ref, wz_ref, wx_ref, wdt_ref, wg_ref, cos_ref, sin_ref,
                  qng_ref, wq_ref, gq_ref, kvg_ref, wk_ref, gk_ref, krg_ref, mh_ref,
                  z_out, xbc_out, dt_out, hg_out, q_out, k_out, ckv_out, cbf_out, kr_out):
    tm = x_ref.shape[0]
    xb = _rms(x_ref[...], g_ref[...]).astype(BF16)
    z_out[...] = _dot(xb, wz_ref[...])
    xbc_out[...] = _dot(xb, wx_ref[...])
    dt_out[...] = _dot(xb, wdt_ref[...])
    hg_out[...] = _dot(xb, wg_ref[...])
    h = _dot(xb, wm_ref[...])
    cq, ckv, krr = h[:, 0:256], h[:, 256:384], h[:, 384:512]
    cos, sin = cos_ref[...], sin_ref[...]
    lane = lax.broadcasted_iota(jnp.int32, (tm, LANES), 1)
    mh = mh_ref[...]

    def rope(xv):
        rot = jnp.where(lane < MLA_NOPE + MLA_ROPE // 2,
                        pltpu.roll(xv, LANES - MLA_ROPE // 2, 1), pltpu.roll(xv, MLA_ROPE // 2, 1))
        return xv * cos + rot * sin

    def seg_norm(xv, g):
        msq = _dot((xv * xv).astype(BF16), mh)
        return xv * lax.rsqrt(msq + EPS) * g

    q = _dot(_rms(cq, qng_ref[...]).astype(BF16), wq_ref[...])
    c = _rms(ckv, kvg_ref[...])
    cb = c.astype(BF16)
    ckv_out[...] = c
    cbf_out[...] = jnp.concatenate([cb, jnp.ones_like(cb)], axis=1)
    kr = rope(seg_norm(krr, krg_ref[...]))
    kr_out[...] = kr[:, MLA_NOPE:MLA_QK]
    k = _dot(cb, wk_ref[...])
    gq, gk = gq_ref[...], gk_ref[...]
    for hd in range(MLA_HEADS):
        sl = slice(hd * LANES, (hd + 1) * LANES)
        q_out[:, sl] = rope(seg_norm(q[:, sl], gq)).astype(BF16)
        k_out[:, sl] = (seg_norm(k[:, sl], gk) + kr).astype(BF16)


def in_proj(x, cos_t, sin_t, p, tm):
    rows, d = x.shape
    assert rows % tm == 0
    nt = cos_t.shape[0] // tm
    row_spec = lambda n: pl.BlockSpec((tm, n), lambda i: (i, 0))
    tab_spec = pl.BlockSpec((tm, LANES), lambda i: (i % nt, 0))
    hw = MLA_HEADS * LANES
    proj = ("norm_mix", "w_mla", "w_z", "w_xbc", "w_dt", "w_gla")
    prep = ("mla_q_norm", "wq_pad", "gq_vec", "mla_kv_norm", "wk_pad", "gk_vec", "gkr_vec", "mh")
    widths = [(p["w_z"].shape[1], F32), (p["w_xbc"].shape[1], F32), (p["w_dt"].shape[1], F32),
              (p["w_gla"].shape[1], F32), (hw, BF16), (hw, BF16), (LANES, F32), (2 * LANES, BF16), (MLA_ROPE, F32)]
    return pl.pallas_call(
        _in_proj_body, grid=(rows // tm,),
        in_specs=[row_spec(d)] + [_wspec(p[n]) for n in proj] + [tab_spec, tab_spec] + [_wspec(p[n]) for n in prep],
        out_specs=[row_spec(n) for n, _ in widths],
        out_shape=[jax.ShapeDtypeStruct((rows, n), dt) for n, dt in widths],
        compiler_params=_cparams("parallel"), name="in_proj",
    )(x, *[p[n].arr for n in proj], cos_t, sin_t, *[p[n].arr for n in prep])


def _mla_prompt_body(q_ref, k_ref, c_ref, wuv_ref, o_ref, m_ref, l_ref, acc_ref, olat_ref, *, tq, tk):
    qi = pl.program_id(1)

    def causal(n_rows):
        return (lax.broadcasted_iota(jnp.int32, (n_rows, tk), 1)
                <= lax.broadcasted_iota(jnp.int32, (n_rows, tk), 0))

    m_ref[...] = jnp.full(m_ref.shape, -jnp.inf, F32)
    l_ref[...] = jnp.zeros(l_ref.shape, F32)
    acc_ref[...] = jnp.zeros(acc_ref.shape, F32)

    def step(j, r0, masked):
        start = pl.multiple_of(j * tk, tk)
        rows = slice(r0, tq)
        cblk = c_ref[0, pl.ds(start, tk), :]
        for hd in range(MLA_HEADS):
            sl = slice(hd * LANES, (hd + 1) * LANES)
            s = _dot_nt(q_ref[0, rows, sl], k_ref[0, pl.ds(start, tk), sl])
            if masked:
                s = jnp.where(causal(tq - r0), s, -jnp.inf)
            m_old = m_ref[hd, rows, :]
            m_new = jnp.maximum(m_old, jnp.max(s, axis=-1, keepdims=True))
            p = jnp.exp2(s - jnp.concatenate([m_new] * (tk // LANES), axis=1))
            corr = jnp.exp2(m_old - m_new)
            pv = _dot(p.astype(BF16), cblk)
            acc_ref[hd, rows, :] = acc_ref[hd, rows, :] * corr + pv[:, :LANES]
            l_ref[hd, rows, :] = l_ref[hd, rows, :] * corr + pv[:, LANES:]
            m_ref[hd, rows, :] = m_new

    def body(j, carry):
        step(j, 0, False)
        return carry

    n_full = qi * (tq // tk)
    lax.fori_loop(0, n_full, body, 0)
    for d in range(tq // tk):
        step(n_full + d, d * tk, True)
    for hd in range(MLA_HEADS):
        olat_ref[:, hd * LANES:(hd + 1) * LANES] = (acc_ref[hd] / l_ref[hd]).astype(BF16)
    o_ref[0] = _dot(olat_ref[...], wuv_ref[...]).astype(o_ref.dtype)


def mla_prompt(q_cat, k_cat, c_bf, wuv_bd, tq=256, tk=256):
    b, L, hw = q_cat.shape
    assert tq % tk == 0 and L % tq == 0
    return pl.pallas_call(
        functools.partial(_mla_prompt_body, tq=tq, tk=tk), grid=(b, L // tq),
        in_specs=[pl.BlockSpec((1, tq, hw), lambda bi, qi: (bi, qi, 0)),
                  pl.BlockSpec((1, L, hw), lambda bi, qi: (bi, 0, 0)),
                  pl.BlockSpec((1, L, 2 * LANES), lambda bi, qi: (bi, 0, 0)),
                  _wspec(wuv_bd)],
        out_specs=pl.BlockSpec((1, tq, MLA_HEADS * MLA_V), lambda bi, qi: (bi, qi, 0)),
        out_shape=jax.ShapeDtypeStruct((b, L, MLA_HEADS * MLA_V), BF16),
        scratch_shapes=[pltpu.VMEM((MLA_HEADS, tq, LANES), F32), pltpu.VMEM((MLA_HEADS, tq, LANES), F32),
                        pltpu.VMEM((MLA_HEADS, tq, LANES), F32), pltpu.VMEM((tq, hw), BF16)],
        compiler_params=_cparams("parallel", "arbitrary"), name="mla_prompt",
    )(q_cat, k_cat, c_bf, wuv_bd.arr)


def _mem_kv_body(x_ref, wk_ref, wv_ref, m_ref, g_ref, k_out, v_out):
    xb = x_ref[...].astype(BF16)
    k = _dot(xb, wk_ref[...])
    msq = _dot((k * k).astype(BF16), m_ref[...])
    k_out[...] = k * lax.rsqrt(msq + EPS) * g_ref[...]
    v_out[...] = _dot(xb, wv_ref[...])


def mem_kv(mem2d, p, tm):
    rows, d = mem2d.shape
    row = lambda n: pl.BlockSpec((tm, n), lambda i: (i, 0))
    names = ("mem_wk", "mem_wv", "m64", "mem_k_gain")
    return pl.pallas_call(
        _mem_kv_body, grid=(rows // tm,),
        in_specs=[row(d)] + [_wspec(p[n]) for n in names],
        out_specs=[row(MEM_D), row(MEM_D)],
        out_shape=[jax.ShapeDtypeStruct((rows, MEM_D), F32)] * 2,
        compiler_params=_cparams("parallel"), name="mem_kv",
    )(mem2d, *[p[n].arr for n in names])


def _mem_attend_body(x_ref, *refs, nb, kv_t, n_pre):
    a_refs, w_refs = refs[:n_pre], refs[n_pre:2 * n_pre]
    g_ref, wq_ref, m_ref, gq_ref, k_ref, v_ref, wo_ref, o_ref = refs[2 * n_pre:]
    r = x_ref.shape[0]
    x = x_ref[...]
    for a_ref, w_ref in zip(a_refs, w_refs):
        x = x + _dot(a_ref[...], w_ref[...])
    q = _dot(_rms(x, g_ref[...]).astype(BF16), wq_ref[...])
    msq = _dot((q * q).astype(BF16), m_ref[...])
    qn = (q * lax.rsqrt(msq + EPS) * gq_ref[...]).astype(BF16)
    if kv_t:
        kb = jnp.concatenate([k_ref[b] for b in range(nb)], axis=1).astype(BF16)
        vb = jnp.concatenate([v_ref[b] for b in range(nb)], axis=1).astype(BF16)
    else:
        kb = k_ref[...].astype(BF16)
        vb = v_ref[...].astype(BF16)
    lane = lax.broadcasted_iota(jnp.int32, (r, MEM_D), 1) // MEM_HEAD_DIM
    if nb > 1:
        rb = lax.broadcasted_iota(jnp.int32, (r, nb * N_MEM), 0) // (r // nb)
        cb = lax.broadcasted_iota(jnp.int32, (r, nb * N_MEM), 1) // N_MEM
        same = rb == cb
    o = jnp.zeros((r, MEM_D), F32)
    for hd in range(MEM_HEADS):
        qm = jnp.where(lane == hd, qn, jnp.zeros_like(qn))
        s = _dot(qm, kb) if kv_t else _dot_nt(qm, kb)
        if nb > 1:
            s = jnp.where(same, s, -jnp.inf)
        pr = jnp.exp(s - jnp.max(s, axis=-1, keepdims=True))
        prb = pr.astype(BF16)
        pv = (_dot_nt(prb, vb) if kv_t else _dot(prb, vb)) / jnp.sum(pr, axis=-1, keepdims=True)
        o = jnp.where(lane == hd, pv, o)
    o_ref[...] = x + _dot(o.astype(BF16), wo_ref[...])


def mem_attend(x, acts, ws, k, v, p, r, nb, rows_per_batch, kv_t, layer=0):
    rows, d = x.shape
    row = lambda n: pl.BlockSpec((r, n), lambda i: (i, 0))
    names = ("norm_mem", "mem_wq", "m64", "mem_q_gain")
    if kv_t:
        kv_spec = pl.BlockSpec((None, nb, MEM_D, N_MEM), lambda i: (layer, i, 0, 0))
    elif nb == 1:
        kv_spec = pl.BlockSpec((N_MEM, MEM_D), lambda i: (i // (rows_per_batch // r), 0))
    else:
        kv_spec = pl.BlockSpec((nb * N_MEM, MEM_D), lambda i: (i, 0))
    return pl.pallas_call(
        functools.partial(_mem_attend_body, nb=nb, kv_t=kv_t, n_pre=len(acts)), grid=(rows // r,),
        in_specs=[row(d)] + [row(a.shape[1]) for a in acts] + [_wspec(w) for w in ws]
        + [_wspec(p[n]) for n in names] + [kv_spec, kv_spec, _wspec(p["mem_wo"])],
        out_specs=row(d),
        out_shape=jax.ShapeDtypeStruct((rows, d), F32),
        compiler_params=_cparams("parallel"), name="mem_attend",
    )(x, *acts, *[w.arr for w in ws], *[p[n].arr for n in names], k, v, p["mem_wo"].arr)


def _ffn_body(x_ref, g_ref, *refs, carry, tiles_per_seq, seg, tail_rows):
    if carry:
        wup_ref, wd_ref, cw_ref, cb_ref, o_ref, tail_ref, xn_ref, carry_ref = refs
        halo_ref = None
    else:
        halo_ref, wup_ref, wd_ref, cw_ref, cb_ref, o_ref, tail_ref, xn_ref = refs
        carry_ref = None
    tm = x_ref.shape[0]
    x = x_ref[...]
    xn_ref[...] = _rms(x, g_ref[...]).astype(BF16)
    row = lax.broadcasted_iota(jnp.int32, (tm, FFN_CHUNK), 0)
    if carry:
        @pl.when(pl.program_id(0) % tiles_per_seq == 0)
        def _():
            carry_ref[...] = jnp.zeros_like(carry_ref)
    else:
        t = row % seg
    n_chunks = D_FF // FFN_CHUNK

    def up(j):
        sl = slice(j * FFN_CHUNK, (j + 1) * FFN_CHUNK)
        xn = xn_ref[...]
        return (_dot(xn, wup_ref[:, sl]),
                _dot(xn, wup_ref[:, D_FF + j * FFN_CHUNK:D_FF + (j + 1) * FFN_CHUNK]))

    acc = x
    uv = up(0)
    for j in range(n_chunks):
        sl = slice(j * FFN_CHUNK, (j + 1) * FFN_CHUNK)
        u, v = uv
        if j + 1 < n_chunks:
            uv = up(j + 1)
        um1 = pltpu.roll(u, 1, 0)
        um2 = pltpu.roll(u, 2, 0)
        if carry:
            c6 = carry_ref[6:7, sl]
            c7 = carry_ref[7:8, sl]
            um1 = jnp.where(row == 0, c7, um1)
            um2 = jnp.where(row == 0, c6, jnp.where(row == 1, c7, um2))
            carry_ref[:, sl] = u[tm - SUBLANES:, :]
        else:
            hal = halo_ref[:, sl]
            um1 = jnp.where(t >= 1, um1, pltpu.roll(hal, tm - 1, 0))
            um2 = jnp.where(t >= 2, um2, hal)
        tail_ref[:, sl] = u[tm - tail_rows:, :]
        uc = um2 * cw_ref[0:1, sl] + um1 * cw_ref[1:2, sl] + u * cw_ref[2:3, sl] + cb_ref[:, sl]
        a = (_silu(uc) * v).astype(BF16)
        acc = acc + _dot(a, wd_ref[sl, :])
    o_ref[...] = acc


def ffn(x, halo, p, tm, carry, tiles_per_seq, seg, tail_rows):
    rows, d = x.shape
    n_tiles = rows // tm
    row_spec = lambda n: pl.BlockSpec((tm, n), lambda i: (i, 0))
    in_specs = [row_spec(d), _wspec(p["norm_ffn"])]
    args = [x, p["norm_ffn"].arr]
    if not carry:
        in_specs.append(row_spec(D_FF))
        args.append(halo)
    in_specs += [_wspec(p[n], resident=True) for n in ("ffn_wup", "ffn_wd")]
    in_specs += [_wspec(p["ffn_conv_w"]), _wspec(p["ffn_conv_b"])]
    args += [p[n].arr for n in ("ffn_wup", "ffn_wd", "ffn_conv_w", "ffn_conv_b")]
    scratch = [pltpu.VMEM((tm, d), BF16)]
    if carry:
        scratch.append(pltpu.VMEM((SUBLANES, D_FF), F32))
    return pl.pallas_call(
        functools.partial(_ffn_body, carry=carry, tiles_per_seq=tiles_per_seq, seg=seg, tail_rows=tail_rows),
        grid=(n_tiles,), in_specs=in_specs,
        out_specs=[row_spec(d), pl.BlockSpec((tail_rows, D_FF), lambda i: (i, 0))],
        out_shape=[jax.ShapeDtypeStruct((rows, d), F32), jax.ShapeDtypeStruct((n_tiles * tail_rows, D_FF), F32)],
        scratch_shapes=scratch,
        compiler_params=_cparams("arbitrary"), name="ffn",
    )(*args)


def _ssd_body(*refs, ns, nb, seg, carry, has_state):
    z_ref, xbc_ref, dtr_ref = refs[0:3]
    i = 3
    halo_ref = None
    if not carry:
        halo_ref = refs[i]
        i += 1
    consts = refs[i:i + 7]
    i += 7
    h0_ref = None
    if has_state:
        h0_ref = refs[i]
        i += 1
    o_ref, h_ref = refs[i:i + 2]
    tail_ref = refs[i + 2] if carry else None

    @pl.when(pl.program_id(1) == 0)
    def _():
        h_ref[...] = h0_ref[...] if has_state else jnp.zeros_like(h_ref)
        if carry:
            tail_ref[...] = jnp.zeros_like(tail_ref)

    for s in range(ns):
        _ssd_stream(z_ref.at[s], xbc_ref.at[s], dtr_ref.at[s], None if carry else halo_ref.at[s], consts,
                    o_ref.at[s], h_ref.at[s], tail_ref.at[s] if carry else None, nb=nb, seg=seg, carry=carry)


def _ssd_stream(z_ref, xbc_ref, dtr_ref, halo_ref, consts, o_ref, h_ref, tail_ref, *, nb, seg, carry):
    cw_ref, cb_ref, dtb_ref, alog_ref, d_ref, ng_ref, rep_ref = consts
    r = z_ref.shape[0]
    hp = SSD_HEADS * SSD_HEADDIM
    gp = hp // SSD_GROUPS

    x = xbc_ref[...]
    row = lax.broadcasted_iota(jnp.int32, (r, SSD_CONV_DIM), 0)
    row8 = lax.broadcasted_iota(jnp.int32, (SUBLANES, SSD_CONV_DIM), 0)

    def prev(k):
        xs = pltpu.roll(x, k, 0)
        if carry:
            tl = pltpu.roll(tail_ref[...], k, 0)
            top = jnp.where(row8 < k, tl, xs[:SUBLANES])
            return jnp.concatenate([top, xs[SUBLANES:]], axis=0)
        sh = SSD_CONV - 1 - k
        hal = halo_ref[...]
        hs_ = hal if sh == 0 else pltpu.roll(hal, r - sh, 0)
        return jnp.where(row % seg >= k, xs, hs_)

    u = (cb_ref[...] + x * cw_ref[3:4, :] + prev(1) * cw_ref[2:3, :] + prev(2) * cw_ref[1:2, :]
         + prev(3) * cw_ref[0:1, :])
    if carry:
        tail_ref[...] = x[r - SUBLANES:, :]
    u = _silu(u)

    ri = lax.broadcasted_iota(jnp.int32, (r, r), 0)
    ci = lax.broadcasted_iota(jnp.int32, (r, r), 1)
    if nb > 1:
        tril = jnp.logical_and((ri // seg) == (ci // seg), ci <= ri)
    else:
        tril = ci <= ri
    tri = jnp.where(tril, 1.0, 0.0).astype(BF16)

    dtc = _softplus(dtr_ref[...] + dtb_ref[...])
    csc = _dot_sel(tri, dtc * (-jnp.exp(alog_ref[...])))
    dt = _dot_rep(dtc, rep_ref[...])
    cs = _dot_rep(csc, rep_ref[...])
    tot = _seg_total(cs, nb, seg) if nb > 1 else _seg_total(cs, 1, r)

    lane = lax.broadcasted_iota(jnp.int32, (r, LANES), 1)
    rb = lax.broadcasted_iota(jnp.int32, (r, LANES), 0) // seg
    lo = lane < SSD_HEADDIM

    def pair(a, b):
        return jnp.where(lo, a, b)

    def hs(a, hd):
        return a[:, hd * LANES:(hd + 1) * LANES]

    zz = z_ref[...]
    hall = h_ref[...].astype(BF16)
    for g in range(SSD_GROUPS):
        h0i, h1i = 2 * g, 2 * g + 1
        gs = slice(g * LANES, (g + 1) * LANES)
        xg = u[:, gs]
        bg = u[:, SSD_D + g * SSD_STATE:SSD_D + (g + 1) * SSD_STATE].astype(BF16)
        cg = u[:, SSD_D + SSD_GROUPS * SSD_STATE + g * SSD_STATE:
               SSD_D + SSD_GROUPS * SSD_STATE + (g + 1) * SSD_STATE].astype(BF16)
        cbm = _dot_nt(cg, bg)
        w0 = (cbm * jnp.where(tril, jnp.exp(hs(cs, h0i) - hs(cs, h0i).T), 0.0)).astype(BF16)
        w1 = (cbm * jnp.where(tril, jnp.exp(hs(cs, h1i) - hs(cs, h1i).T), 0.0)).astype(BF16)
        dtp = pair(hs(dt, h0i), hs(dt, h1i))
        csp = pair(hs(cs, h0i), hs(cs, h1i))
        totp = pair(hs(tot, h0i), hs(tot, h1i))
        xdt = (xg * dtp).astype(BF16)
        y = pair(_dot(w0, xdt), _dot(w1, xdt))
        zst = _dot_nt(cg, hall)
        if nb == 1:
            yst = zst[:, g * gp:(g + 1) * gp]
        else:
            yst = jnp.zeros((r, gp), F32)
            for b in range(nb):
                yst = jnp.where(rb == b, zst[:, b * hp + g * gp:b * hp + (g + 1) * gp], yst)
        y = y + yst * jnp.exp(csp) + d_ref[:, gs] * xg
        y = y * _silu(zz[:, gs])
        o_ref[:, gs] = _rms(y, ng_ref[:, gs]).astype(o_ref.dtype)

        xw = (xg * (jnp.exp(totp - csp) * dtp)).astype(BF16)
        if nb > 1:
            xw = jnp.concatenate([jnp.where(rb == b, xw, jnp.zeros_like(xw)) for b in range(nb)], axis=1)
        dh = _dot_tn(xw, bg)
        for b in range(nb):
            r0 = b * seg if nb > 1 else 0
            dec = jnp.concatenate(
                [jnp.broadcast_to(jnp.exp(hs(tot, h0i)[r0:r0 + 1, :]), (SSD_HEADDIM, LANES)),
                 jnp.broadcast_to(jnp.exp(hs(tot, h1i)[r0:r0 + 1, :]), (SSD_HEADDIM, LANES))], axis=0)
            sl = slice(b * hp + g * gp, b * hp + (g + 1) * gp)
            h_ref[sl, :] = h_ref[sl, :] * dec + dh[b * gp:(b + 1) * gp, :]


def ssd(z, xbc, dtr, halo, h0, p, nb, seg, ns, layer=0):
    n_groups, lg, _ = z.shape
    r = SSD_CHUNK
    carry = halo is None
    n_chunks = lg // r
    assert lg % r == 0 and n_groups % ns == 0 and (carry or n_chunks == 1)
    hp = SSD_HEADS * SSD_HEADDIM
    row_spec = lambda n: pl.BlockSpec((ns, r, n), lambda b, c: (b, c, 0))
    st_spec = pl.BlockSpec((ns, nb * hp, SSD_STATE), lambda b, c: (b, 0, 0))
    in_specs = [row_spec(SSD_D), row_spec(SSD_CONV_DIM), row_spec(LANES)]
    args = [z, xbc, dtr]
    if not carry:
        in_specs.append(row_spec(SSD_CONV_DIM))
        args.append(halo)
    names = ("ssd_conv_w", "ssd_conv_b", "ssd_dt_bias", "ssd_a_log", "ssd_d", "ssd_norm", "ssd_rep")
    in_specs += [_wspec(p[n]) for n in names]
    args += [p[n].arr for n in names]
    if h0 is not None:
        in_specs.append(pl.BlockSpec((None, ns, nb * hp, SSD_STATE), lambda b, c: (layer, b, 0, 0)))
        args.append(h0)
    scratch = [pltpu.VMEM((ns, SUBLANES, SSD_CONV_DIM), F32)] if carry else []
    return pl.pallas_call(
        functools.partial(_ssd_body, ns=ns, nb=nb, seg=seg, carry=carry, has_state=h0 is not None),
        grid=(n_groups // ns, n_chunks), in_specs=in_specs,
        out_specs=[row_spec(SSD_D), st_spec],
        out_shape=[jax.ShapeDtypeStruct((n_groups, lg, SSD_D), BF16),
                   jax.ShapeDtypeStruct((n_groups, nb * hp, SSD_STATE), F32)],
        scratch_shapes=scratch,
        compiler_params=_cparams("parallel", "arbitrary"), name="ssd",
    )(*args)


def _gla_body(*refs, ns, nb, seg, c, has_state):
    hg3_ref, wg_ref, bg_ref, m_ref, ng_ref = refs[0:5]
    i = 5
    s0_ref = None
    if has_state:
        s0_ref = refs[i]
        i += 1
    o3_ref, stc_ref, st3_ref = refs[i:i + 3]
    r = hg3_ref.shape[1]

    def head_blocks():
        for s in range(ns):
            for b in range(nb):
                for hd in range(GLA_HEADS):
                    yield s, slice(b * GLA_VD + hd * GLA_DV, b * GLA_VD + (hd + 1) * GLA_DV), \
                        slice(hd * GLA_DK, (hd + 1) * GLA_DK)

    @pl.when(pl.program_id(1) == 0)
    def _():
        st3_ref[...] = jnp.zeros_like(st3_ref)
        if has_state:
            for s, rows, lanes in head_blocks():
                st3_ref[s, rows, lanes] = s0_ref[s, rows, :]

    sl_ = seg if nb > 1 else c
    ri = lax.broadcasted_iota(jnp.int32, (r, r), 0)
    ci = lax.broadcasted_iota(jnp.int32, (r, r), 1)
    same = (ri // sl_) == (ci // sl_)
    tril = jnp.logical_and(same, ci <= ri)
    tri = jnp.where(tril, 1.0, 0.0).astype(BF16)
    klane = lax.broadcasted_iota(jnp.int32, (r, GLA_KD), 1) // GLA_DK
    vlane = lax.broadcasted_iota(jnp.int32, (r, GLA_VD), 1) // GLA_DV
    rbv = lax.broadcasted_iota(jnp.int32, (r, GLA_VD), 0) // seg
    blk = (lax.broadcasted_iota(jnp.int32, (GLA_VD, GLA_KD), 0) // GLA_DV
           == lax.broadcasted_iota(jnp.int32, (GLA_VD, GLA_KD), 1) // GLA_DK)

    for s in range(ns):
        hg_ref, o_ref, st_ref = hg3_ref.at[s], o3_ref.at[s], st3_ref.at[s]
        q = hg_ref[:, 0:GLA_KD] * (GLA_DK ** -0.5)
        k = hg_ref[:, GLA_KD:2 * GLA_KD]
        vb = hg_ref[:, 2 * GLA_KD:2 * GLA_KD + GLA_VD].astype(BF16)
        gg = hg_ref[:, 2 * GLA_KD + GLA_VD:2 * GLA_KD + 2 * GLA_VD]
        glr = hg_ref[:, 2 * GLA_KD + 2 * GLA_VD:2 * GLA_KD + 2 * GLA_VD + LANES].astype(BF16)
        gate = _log_sigmoid(_dot(glr, wg_ref[...]) + bg_ref[...]) * (1.0 / GLA_GATE_TAU)
        bc = _dot_sel(tri, gate)
        tot = _seg_total(bc, r // sl_, sl_)
        qt = (q * jnp.exp(bc)).astype(BF16)
        kt = (k * jnp.exp(-bc)).astype(BF16)
        kd = (k * jnp.exp(tot - bc)).astype(BF16)
        etot = jnp.exp(tot)
        o = jnp.zeros((r, GLA_VD), F32)
        for hd in range(GLA_HEADS):
            a = _dot_nt(jnp.where(klane == hd, qt, jnp.zeros_like(qt)), kt)
            a = jnp.where(tril, a, 0.0).astype(BF16)
            o = o + _dot(a, jnp.where(vlane == hd, vb, jnp.zeros_like(vb)))
        if nb > 1:
            zs = _dot_nt(qt, st_ref[...].astype(BF16))
            ost = jnp.zeros((r, GLA_VD), F32)
            for b in range(nb):
                ost = jnp.where(rbv == b, zs[:, b * GLA_VD:(b + 1) * GLA_VD], ost)
            o = o + ost
            vexp = jnp.concatenate([jnp.where(rbv == b, vb, jnp.zeros_like(vb)) for b in range(nb)], axis=1)
            ds = _dot_tn(vexp, kd)
            for b in range(nb):
                sl = slice(b * GLA_VD, (b + 1) * GLA_VD)
                st_ref[sl, :] = st_ref[sl, :] * etot[b * seg:b * seg + 1, :] + jnp.where(blk, ds[sl, :], 0.0)
        else:
            st = st_ref[...]
            parts = []
            for sub in range(r // c):
                rs = slice(sub * c, (sub + 1) * c)
                parts.append(_dot_nt(qt[rs, :], st.astype(BF16)))
                st = st * etot[sub * c:sub * c + 1, :] + jnp.where(blk, _dot_tn(vb[rs, :], kd[rs, :]), 0.0)
            st_ref[...] = st
            o = o + jnp.concatenate(parts, axis=0)
        msq = _dot((o * o).astype(BF16), m_ref[...])
        on = o * lax.rsqrt(msq + EPS) * ng_ref[...]
        o_ref[...] = (on * _silu(gg)).astype(o_ref.dtype)

    @pl.when(pl.program_id(1) == pl.num_programs(1) - 1)
    def _():
        for s, rows, lanes in head_blocks():
            stc_ref[s, rows, :] = st3_ref[s, rows, lanes]


def gla(hg, s0, p, nb, seg, c, r, ns):
    n_groups, lg, width = hg.shape
    n_steps = lg // r
    assert lg % r == 0 and r % c == 0 and n_groups % ns == 0
    row_spec = lambda n: pl.BlockSpec((ns, r, n), lambda b, s: (b, s, 0))
    st_spec = pl.BlockSpec((ns, nb * GLA_VD, GLA_DK), lambda b, s: (b, 0, 0))
    names = ("gla_w_gate", "gla_b_gate", "m64", "gla_norm")
    in_specs = [row_spec(width)] + [_wspec(p[n]) for n in names]
    args = [hg] + [p[n].arr for n in names]
    if s0 is not None:
        in_specs.append(st_spec)
        args.append(s0)
    return pl.pallas_call(
        functools.partial(_gla_body, ns=ns, nb=nb, seg=seg, c=c, has_state=s0 is not None),
        grid=(n_groups // ns, n_steps), in_specs=in_specs,
        out_specs=[row_spec(GLA_VD), st_spec],
        out_shape=[jax.ShapeDtypeStruct((n_groups, lg, GLA_VD), BF16),
                   jax.ShapeDtypeStruct((n_groups, nb * GLA_VD, GLA_DK), F32)],
        scratch_shapes=[pltpu.VMEM((ns, nb * GLA_VD, GLA_KD), F32)],
        compiler_params=_cparams("parallel", "arbitrary"), name="gla",
    )(*args)


def _q_absorb_body(q_ref, wt_ref, sel_ref, qa_out, qr_out):
    for hd in range(MLA_HEADS):
        qh = q_ref[:, hd * LANES:(hd + 1) * LANES].astype(F32)
        qa_out[:, hd * LANES:(hd + 1) * LANES] = _dot(qh, wt_ref[hd], HI).astype(BF16)
        qr_out[:, hd * LANES:(hd + 1) * LANES] = _dot(qh, sel_ref[...], HI).astype(BF16)


def q_absorb(q_cat, p, tm):
    rows, hw = q_cat.shape
    row_spec = pl.BlockSpec((tm, hw), lambda i: (i, 0))
    return pl.pallas_call(
        _q_absorb_body, grid=(rows // tm,),
        in_specs=[row_spec, _wspec(p["wukt_g"]), _wspec(p["rope_sel"])],
        out_specs=[row_spec, row_spec],
        out_shape=[jax.ShapeDtypeStruct((rows, hw), BF16)] * 2,
        compiler_params=_cparams("parallel"), name="q_absorb",
    )(q_cat, p["wukt_g"].arr, p["rope_sel"].arr)


PAGED_CP = 64
PAGED_SUB = 512


def _mla_paged_body(pt_ref, ptn_ref, wukt_ref, qa_ref, qr_ref, cnew_ref, krnew_ref, wuv_ref, lat_hbm, kr_hbm,
                    o_ref, lat_buf, kr_buf, sem, m_ref, l_ref, acc_ref, *, layer, n_pages, n_batch, seq, cp):
    bi = pl.program_id(0)
    nc = n_pages // cp
    nq = MLA_HEADS * seq
    nk = MLA_HEADS * MLA_NOPE

    def copies(table, chunk, slot):
        out = []
        for pg in range(cp):
            pid = table[0, 0, chunk * cp + pg]
            dst = pl.ds(pg * PAGE_SIZE, PAGE_SIZE)
            out.append(pltpu.make_async_copy(lat_hbm.at[layer, pid], lat_buf.at[slot, dst, :], sem.at[0, slot]))
            out.append(pltpu.make_async_copy(kr_hbm.at[layer, pid], kr_buf.at[slot, :, dst], sem.at[1, slot]))
        return out

    def start(table, chunk, slot):
        for cpy in copies(table, chunk, slot):
            cpy.start()

    def wait(slot):
        for cpy in copies(pt_ref, 0, slot):
            cpy.wait()

    @pl.when(bi == 0)
    def _():
        start(pt_ref, 0, 0)

    m_ref[...] = jnp.full(m_ref.shape, -jnp.inf, F32)
    l_ref[...] = jnp.zeros(l_ref.shape, F32)
    acc_ref[...] = jnp.zeros(acc_ref.shape, F32)
    lhs = jnp.concatenate([wukt_ref[...], qa_ref[0]], axis=0)
    qr = qr_ref[0]

    def scores(latb, krt):
        big = _dot_nt(lhs, latb)
        kt = big[0:nk, :]
        ss = jnp.sum((kt * kt).reshape(MLA_HEADS, MLA_NOPE, kt.shape[1]), axis=1) * (1.0 / MLA_NOPE)
        rinv = lax.rsqrt(ss + EPS)
        rexp = jnp.concatenate([jnp.broadcast_to(rinv[hd:hd + 1, :], (seq, rinv.shape[1]))
                                for hd in range(MLA_HEADS)], axis=0)
        return big[nk:, :] * rexp + _dot(qr, krt.astype(BF16))

    def attend(lat, krt, mask):
        keys = lat.shape[0]
        sub = min(PAGED_SUB, keys)
        latb = lat.astype(BF16)
        s = jnp.concatenate([scores(latb[i * sub:(i + 1) * sub, :], krt[:, i * sub:(i + 1) * sub])
                             for i in range(keys // sub)], axis=1)
        if mask is not None:
            s = jnp.where(mask, s, -jnp.inf)
        m_old = m_ref[...]
        m_new = jnp.maximum(m_old, jnp.max(s, axis=-1, keepdims=True))
        pr = jnp.exp2(s - jnp.concatenate([m_new] * (keys // LANES), axis=1))
        corr = jnp.exp2(m_old - m_new)
        l_ref[...] = l_ref[...] * corr + jnp.sum(pr, axis=-1, keepdims=True)
        acc_ref[...] = acc_ref[...] * corr + _dot(pr.astype(BF16), latb)
        m_ref[...] = m_new

    lat_new = jnp.concatenate([cnew_ref[0], jnp.zeros((LANES - seq, MLA_KV_RANK), F32)], axis=0)
    qtok = lax.broadcasted_iota(jnp.int32, (nq, LANES), 0) % seq
    key = lax.broadcasted_iota(jnp.int32, (nq, LANES), 1)
    attend(lat_new, krnew_ref[0], key <= qtok)

    def step(slot, prefetch):
        prefetch()
        wait(slot)
        attend(lat_buf[slot], kr_buf[slot], None)

    def pair_body(jj, carry_):
        step(0, lambda: start(pt_ref, 2 * jj + 1, 1))
        step(1, lambda: start(pt_ref, 2 * jj + 2, 0))
        return carry_

    lax.fori_loop(0, nc // 2 - 1, pair_body, 0)
    step(0, lambda: start(pt_ref, nc - 1, 1))

    def next_batch_prefetch():
        @pl.when(bi + 1 < n_batch)
        def _():
            start(ptn_ref, 0, 0)

    step(1, next_batch_prefetch)

    olat = (acc_ref[...] / l_ref[...]).astype(BF16)
    o = jnp.zeros((seq, MLA_HEADS * MLA_V), F32)
    for hd in range(MLA_HEADS):
        o = o + _dot(olat[hd * seq:(hd + 1) * seq, :], wuv_ref[hd * LANES:(hd + 1) * LANES, :])
    o_ref[0] = o.astype(o_ref.dtype)


def mla_paged(page_table, wukt, qa, qr, c_new, krt_new, wuv_bd, cache_lat, cache_krt, layer):
    n_batch, n_pages = page_table.shape
    seq = c_new.shape[1]
    cp = min(PAGED_CP, n_pages // 2)
    assert n_pages % (2 * cp) == 0
    nq = MLA_HEADS * seq
    pt3 = page_table.reshape(n_batch, 1, n_pages)
    smem_spec = lambda f: pl.BlockSpec((1, 1, n_pages), f, memory_space=pltpu.SMEM)
    per_b = lambda shp: pl.BlockSpec((1,) + shp, lambda b: (b, 0, 0))
    kc = cp * PAGE_SIZE
    kr_new, cache_kr = krt_new, cache_krt
    return pl.pallas_call(
        functools.partial(_mla_paged_body, layer=layer, n_pages=n_pages, n_batch=n_batch, seq=seq, cp=cp),
        grid=(n_batch,),
        in_specs=[smem_spec(lambda b: (b, 0, 0)),
                  smem_spec(lambda b: (jnp.minimum(b + 1, n_batch - 1), 0, 0)),
                  _wspec(wukt), per_b(qa.shape[1:]), per_b(qr.shape[1:]), per_b(c_new.shape[1:]),
                  per_b(kr_new.shape[1:]),
                  _wspec(wuv_bd),
                  pl.BlockSpec(memory_space=pl.ANY), pl.BlockSpec(memory_space=pl.ANY)],
        out_specs=per_b((seq, MLA_HEADS * MLA_V)),
        out_shape=jax.ShapeDtypeStruct((n_batch, seq, MLA_HEADS * MLA_V), BF16),
        scratch_shapes=[pltpu.VMEM((2, kc, MLA_KV_RANK), F32), pltpu.VMEM((2, MLA_ROPE, kc), F32),
                        pltpu.SemaphoreType.DMA((2, 2)),
                        pltpu.VMEM((nq, LANES), F32), pltpu.VMEM((nq, LANES), F32), pltpu.VMEM((nq, LANES), F32)],
        compiler_params=_cparams("arbitrary"), name="mla_paged",
    )(pt3, pt3, wukt.arr, qa, qr, c_new, kr_new, wuv_bd.arr, cache_lat, cache_kr)


def _block_diag_mean(n, blocks):
    idx = jnp.arange(n)
    m = jnp.zeros((n, n), F32)
    for start, size in blocks:
        inb = jnp.logical_and(idx >= start, idx < start + size)
        m = m + jnp.where(jnp.logical_and(inb[:, None], inb[None, :]), 1.0 / size, 0.0)
    return m.astype(BF16)


def _rope_tables(pos):
    half = MLA_ROPE // 2
    inv = ROPE_THETA ** (-jnp.arange(half, dtype=F32) / half)
    ang = pos.astype(F32)[:, None] * inv[None, :]
    cos, sin = jnp.cos(ang), jnp.sin(ang)
    n = pos.shape[0]
    one, zero = jnp.ones((n, MLA_NOPE), F32), jnp.zeros((n, MLA_NOPE), F32)
    tail1, tail0 = jnp.ones((n, LANES - MLA_QK), F32), jnp.zeros((n, LANES - MLA_QK), F32)
    return (jnp.concatenate([one, cos, cos, tail1], axis=1), jnp.concatenate([zero, -sin, sin, tail0], axis=1))


def _pack_one(w):
    f = lambda name: w[name]
    bf = lambda a: a.astype(BF16)
    p = {}
    w_in = f("w_in")
    o = 0
    cols = {}
    for name, size in (("cq", MLA_Q_RANK), ("ckv", MLA_KV_RANK), ("krr", MLA_ROPE), ("z", SSD_D), ("xbc", SSD_CONV_DIM),
                       ("dt", SSD_HEADS), ("gq", GLA_KD), ("gk", GLA_KD), ("gv", GLA_VD), ("glr", GLA_GATE_RANK),
                       ("gg", GLA_VD)):
        cols[name] = w_in[:, o:o + size]
        o += size
    d = w_in.shape[0]

    def place(a, axis, lo, total):
        cfg = [(0, 0)] * a.ndim
        cfg[axis] = (lo, total - lo - a.shape[axis])
        return jnp.pad(a, cfg)

    krr_pad = place(cols["krr"], 1, MLA_NOPE, LANES)
    glr_pad = place(cols["glr"], 1, 0, LANES)
    p["w_mla"] = bf(jnp.concatenate([cols["cq"], cols["ckv"], krr_pad], axis=1))
    p["w_z"] = bf(cols["z"])
    p["w_xbc"] = bf(cols["xbc"])
    p["w_dt"] = bf(place(cols["dt"], 1, 0, LANES))
    p["w_gla"] = bf(jnp.concatenate([cols["gq"], cols["gk"], cols["gv"], cols["gg"], glr_pad], axis=1))
    p["norm_mix"] = f("norm_mix").reshape(1, d)

    def head_pad(wm, n_real):
        return place(wm, 2, 0, LANES).reshape(wm.shape[0], MLA_HEADS * LANES)

    p["wq_pad"] = bf(head_pad(f("mla_w_uq"), MLA_QK))
    p["wk_pad"] = bf(head_pad(f("mla_w_uk"), MLA_NOPE))
    scale = MLA_QK ** -0.5 * math.log2(math.e)
    zpad = jnp.zeros((LANES - MLA_QK,), F32)
    p["gq_vec"] = (jnp.concatenate([f("mla_qn_norm"), f("mla_qr_norm"), zpad]) * scale).reshape(1, LANES)
    p["gk_vec"] = jnp.concatenate([f("mla_kn_norm"), jnp.zeros((LANES - MLA_NOPE,), F32)]).reshape(1, LANES)
    p["gkr_vec"] = jnp.concatenate([jnp.zeros((MLA_NOPE,), F32), f("mla_kr_norm"), zpad]).reshape(1, LANES)
    p["mla_q_norm"] = f("mla_q_norm").reshape(1, MLA_Q_RANK)
    p["mla_kv_norm"] = f("mla_kv_norm").reshape(1, MLA_KV_RANK)
    p["mh"] = _block_diag_mean(LANES, [(0, MLA_NOPE), (MLA_NOPE, MLA_ROPE)])
    w_uv = f("mla_w_uv")
    p["wuv_bd"] = bf(jnp.concatenate(
        [place(w_uv[:, hd, :], 1, hd * MLA_V, MLA_HEADS * MLA_V) for hd in range(MLA_HEADS)], axis=0))
    w_uk = f("mla_w_uk")
    wt = jnp.transpose(w_uk, (1, 2, 0)) * f("mla_kn_norm")[None, :, None]
    p["wukt_g"] = place(wt, 1, 0, LANES)
    p["wukt"] = bf(jnp.transpose(w_uk, (1, 2, 0)).reshape(MLA_HEADS * MLA_NOPE, MLA_KV_RANK))
    lane_i = jnp.arange(LANES)
    p["rope_sel"] = jnp.where(jnp.logical_and(lane_i[:, None] == lane_i[None, :] + MLA_NOPE,
                                              lane_i[None, :] < MLA_ROPE), 1.0, 0.0).astype(F32)

    p["ssd_conv_w"] = place(f("ssd_conv_w"), 0, 0, SUBLANES)
    p["ssd_conv_b"] = f("ssd_conv_b").reshape(1, SSD_CONV_DIM)
    lane_pad = lambda v: place(v, 0, 0, LANES).reshape(1, LANES)
    p["ssd_dt_bias"] = lane_pad(f("ssd_dt_bias"))
    p["ssd_a_log"] = lane_pad(f("ssd_a_log"))
    p["ssd_rep"] = bf(jnp.repeat(jnp.eye(LANES, SSD_HEADS, dtype=F32), LANES, axis=1))
    p["ssd_d"] = jnp.repeat(f("ssd_d"), SSD_HEADDIM).reshape(1, SSD_D)
    p["ssd_norm"] = f("ssd_norm").reshape(1, SSD_D)
    p["gla_w_gate"] = bf(place(f("gla_w_gate"), 0, 0, LANES))
    p["gla_b_gate"] = f("gla_b_gate").reshape(1, GLA_KD)
    p["gla_norm"] = f("gla_norm").reshape(1, GLA_VD)
    p["m64"] = _block_diag_mean(MEM_D, [(i * 64, 64) for i in range(4)])
    w_o = f("w_o")
    p["wo_mla"] = bf(w_o[:MLA_HEADS * MLA_V])
    p["wo_ssd"] = bf(w_o[MLA_HEADS * MLA_V:MLA_HEADS * MLA_V + SSD_D])
    p["wo_gla"] = bf(w_o[MLA_HEADS * MLA_V + SSD_D:])
    p["norm_mem"] = f("norm_mem").reshape(1, D_MODEL)
    p["mem_wq"], p["mem_wk"], p["mem_wv"], p["mem_wo"] = bf(f("mem_wq")), bf(f("mem_wk")), bf(f("mem_wv")), bf(f("mem_wo"))
    p["mem_q_gain"] = (jnp.tile(f("mem_q_norm"), MEM_HEADS) * MEM_HEAD_DIM ** -0.5).reshape(1, MEM_D)
    p["mem_k_gain"] = jnp.tile(f("mem_k_norm"), MEM_HEADS).reshape(1, MEM_D)
    p["norm_ffn"] = f("norm_ffn").reshape(1, D_MODEL)
    w_up = f("ffn_w_up")
    p["ffn_wup"], p["ffn_wd"] = bf(w_up), bf(f("ffn_w_down"))
    p["ffn_conv_w"] = place(f("ffn_conv_w"), 0, 0, SUBLANES)
    p["ffn_conv_b"] = f("ffn_conv_b").reshape(1, D_FF)
    return p


def _pad_rows(state, seq):
    b, k, c = state.shape
    return jnp.pad(state, ((0, 0), (0, seq - k), (0, 0))).reshape(b * seq, c)


def _layer(x3, p, cos_t, sin_t, tm, sample, mem2d=None):
    b, L, d = x3.shape
    rows = b * L
    x = x3.reshape(rows, d)
    z, xbc, dtr, h_gla, q_cat, k_cat, c_kv, c_bf, kr = in_proj(x, cos_t, sin_t, p, tm)
    hw = MLA_HEADS * LANES
    if sample is None:
        o_mla = mla_prompt(q_cat.reshape(b, L, hw), k_cat.reshape(b, L, hw), c_bf.reshape(b, L, 2 * LANES),
                           p["wuv_bd"], math.gcd(L, 1024), 256)
        o_mla = o_mla.reshape(rows, MLA_HEADS * MLA_V)
        g3 = lambda a: a.reshape(b, L, a.shape[-1])
        ns = math.gcd(b, 4)
        o_ssd, ssd_h = ssd(g3(z), g3(xbc), g3(dtr), None, None, p, 1, L, ns)
        o_gla, gla_st = gla(g3(h_gla), None, p, 1, L, GLA_CHUNK, 256, ns)
    else:
        qa, qrp = q_absorb(q_cat, p, tm)
        to_hq = lambda a, n: a.reshape(b, L, MLA_HEADS, LANES)[..., :n].transpose(0, 2, 1, 3).reshape(b, MLA_HEADS * L, n)
        krt_new = jnp.pad(jnp.swapaxes(kr.reshape(b, L, MLA_ROPE), 1, 2), ((0, 0), (0, 0), (0, LANES - L)))
        o_mla = mla_paged(sample["page_table"], p["wukt"], to_hq(qa, LANES), to_hq(qrp, MLA_ROPE),
                          c_kv.reshape(b, L, LANES), krt_new, p["wuv_bd"], sample["cache_lat"], sample["cache_krt"],
                          sample["layer"])
        o_mla = o_mla.reshape(rows, MLA_HEADS * MLA_V)
        nb = SSD_CHUNK // L
        ng = b // nb
        g3 = lambda a: a.reshape(ng, nb * L, a.shape[-1])
        ns = math.gcd(ng, 2)
        o_ssd, ssd_h = ssd(g3(z), g3(xbc), g3(dtr), g3(_pad_rows(sample["ssd_conv"], L)),
                           sample["ssd_all"].reshape(-1, ng, nb * SSD_D, SSD_STATE), p, nb, L, ns, sample["layer"])
        st0 = jnp.swapaxes(sample["gla"], 2, 3).reshape(ng, nb * GLA_VD, GLA_DK)
        o_gla, gla_st = gla(g3(h_gla), st0, p, nb, L, nb * L, nb * L, ns)
    o_ssd, o_gla = o_ssd.reshape(rows, SSD_D), o_gla.reshape(rows, GLA_VD)
    mix, w_mix = [o_mla, o_ssd, o_gla], [p["wo_mla"], p["wo_ssd"], p["wo_gla"]]
    if sample is None:
        mk, mv = mem_kv(mem2d, p, 512)
        x = mem_attend(x, mix, w_mix, mk, mv, p, math.gcd(L, 1024), 1, L, False)
    else:
        nbm = 128 // L
        x = mem_attend(x, mix, w_mix, sample["mem_kt"], sample["mem_vt"], p, nbm * L, nbm, L, True,
                       sample["layer"])

    ft = min(rows, FFN_ROWS)
    if sample is None:
        x, u_tail = ffn(x, None, p, ft, True, L // ft, L, SUBLANES)
        ffn_conv = u_tail.reshape(b, L // ft, SUBLANES, D_FF)[:, -1, SUBLANES - (FFN_CONV - 1):, :]
    else:
        x, u_full = ffn(x, _pad_rows(sample["ffn_conv"], L), p, ft, False, 1, L, ft)
        ffn_conv = u_full.reshape(b, L, D_FF)[:, L - (FFN_CONV - 1):, :]

    xbc3 = xbc.reshape(b, L, SSD_CONV_DIM)
    out = dict(
        x=x.reshape(b, L, d), lat=c_kv.reshape(b, L, MLA_KV_RANK), kr=kr.reshape(b, L, MLA_ROPE),
        ssd_conv=xbc3[:, L - (SSD_CONV - 1):, :], ssd=ssd_h.reshape(b, SSD_HEADS, SSD_HEADDIM, SSD_STATE),
        gla=jnp.swapaxes(gla_st.reshape(b, GLA_HEADS, GLA_DV, GLA_DK), 2, 3),
        ffn_conv=ffn_conv)
    if sample is None:
        out["mem_k"] = mk.reshape(b, N_MEM, MEM_HEADS, MEM_HEAD_DIM)
        out["mem_v"] = mv.reshape(b, N_MEM, MEM_HEADS, MEM_HEAD_DIM)
    return out


def kernel(x_prompt, x_sample, cache_mla_latent, cache_mla_krope, cache_mem_k, cache_mem_v, state_ssd_conv, state_ssd, state_gla, state_ffn_conv, page_table, mem_prompt, norm_mix, w_in, mla_q_norm, mla_w_uq, mla_kv_norm, mla_w_uk, mla_w_uv, mla_qn_norm, mla_qr_norm, mla_kn_norm, mla_kr_norm, ssd_conv_w, ssd_conv_b, ssd_dt_bias, ssd_a_log, ssd_d, ssd_norm, gla_w_gate, gla_b_gate, gla_norm, w_o, norm_mem, mem_wq, mem_wk, mem_wv, mem_wo, mem_q_norm, mem_k_norm, norm_ffn, ffn_w_up, ffn_conv_w, ffn_conv_b, ffn_w_down):
    weights = dict(
        norm_mix=norm_mix, w_in=w_in, mla_q_norm=mla_q_norm, mla_w_uq=mla_w_uq, mla_kv_norm=mla_kv_norm,
        mla_w_uk=mla_w_uk, mla_w_uv=mla_w_uv, mla_qn_norm=mla_qn_norm, mla_qr_norm=mla_qr_norm,
        mla_kn_norm=mla_kn_norm, mla_kr_norm=mla_kr_norm, ssd_conv_w=ssd_conv_w, ssd_conv_b=ssd_conv_b,
        ssd_dt_bias=ssd_dt_bias, ssd_a_log=ssd_a_log, ssd_d=ssd_d, ssd_norm=ssd_norm, gla_w_gate=gla_w_gate,
        gla_b_gate=gla_b_gate, gla_norm=gla_norm, w_o=w_o, norm_mem=norm_mem, mem_wq=mem_wq, mem_wk=mem_wk,
        mem_wv=mem_wv, mem_wo=mem_wo, mem_q_norm=mem_q_norm, mem_k_norm=mem_k_norm, norm_ffn=norm_ffn,
        ffn_w_up=ffn_w_up, ffn_conv_w=ffn_conv_w, ffn_conv_b=ffn_conv_b, ffn_w_down=ffn_w_down)
    depth = w_in.shape[0]
    bp, lp, _ = x_prompt.shape
    bs, ls, _ = x_sample.shape
    assert ls >= SSD_CONV - 1 and SSD_CHUNK % ls == 0
    past_len = page_table.shape[1] * PAGE_SIZE
    tm_p, tm_s = 512, min(512, bs * ls)
    cos_p, sin_p = _rope_tables(jnp.arange(lp))
    cos_s, sin_s = _rope_tables(past_len + jnp.arange(ls))
    cos_s, sin_s = jnp.tile(cos_s, (tm_s // ls, 1)), jnp.tile(sin_s, (tm_s // ls, 1))
    mem2d = mem_prompt.reshape(bp * N_MEM, D_MODEL)
    cache_krt = jnp.swapaxes(cache_mla_krope, 2, 3)
    mem_kt = jnp.transpose(cache_mem_k, (0, 1, 3, 4, 2)).reshape(depth, bs, MEM_D, N_MEM)
    mem_vt = jnp.transpose(cache_mem_v, (0, 1, 3, 4, 2)).reshape(depth, bs, MEM_D, N_MEM)

    packed = jax.vmap(_pack_one)(weights)
    xp, xs = x_prompt, x_sample
    outs_p, outs_s = [], []
    for l in range(depth):
        p = {name: _Stacked(arr, l) for name, arr in packed.items()}
        op = _layer(xp, p, cos_p, sin_p, tm_p, None, mem2d)
        xp = op["x"]
        outs_p.append(op)
        sample = dict(layer=l, page_table=page_table, cache_lat=cache_mla_latent, cache_krt=cache_krt,
                      mem_kt=mem_kt, mem_vt=mem_vt, ssd_conv=state_ssd_conv[l], ssd_all=state_ssd,
                      gla=state_gla[l], ffn_conv=state_ffn_conv[l])
        os_ = _layer(xs, p, cos_s, sin_s, tm_s, sample)
        xs = os_["x"]
        outs_s.append(os_)

    st = lambda outs, key: jnp.stack([o[key] for o in outs])
    return (xp, xs,
            st(outs_p, "lat"), st(outs_p, "kr"), st(outs_p, "mem_k"), st(outs_p, "mem_v"),
            st(outs_p, "ssd_conv"), st(outs_p, "ssd"), st(outs_p, "gla"), st(outs_p, "ffn_conv"),
            st(outs_s, "lat"), st(outs_s, "kr"), st(outs_s, "ssd_conv"), st(outs_s, "ssd"), st(outs_s, "gla"),
            st(outs_s, "ffn_conv"))
```

```python
import functools
import math

import jax
import jax.numpy as jnp
from jax import lax
from jax.experimental import pallas as pl
from jax.experimental.pallas import tpu as pltpu

F32 = jnp.float32
BF16 = jnp.bfloat16
EPS = 1e-6
LANES = 128
SUBLANES = 8
VMEM_LIMIT = 56 * 1024 * 1024

D_MODEL = 1024
MLA_HEADS, MLA_NOPE, MLA_ROPE, MLA_V = 8, 64, 32, 64
MLA_QK = MLA_NOPE + MLA_ROPE
MLA_Q_RANK, MLA_KV_RANK = 256, 128
ROPE_THETA = 10000.0
PAGE_SIZE = 128
SSD_HEADS, SSD_HEADDIM, SSD_GROUPS, SSD_STATE, SSD_CONV = 4, 64, 2, 128, 4
SSD_D = SSD_HEADS * SSD_HEADDIM
SSD_CONV_DIM = SSD_D + 2 * SSD_GROUPS * SSD_STATE
SSD_CHUNK = 128
GLA_HEADS, GLA_DK, GLA_DV = 4, 32, 64
GLA_KD, GLA_VD = GLA_HEADS * GLA_DK, GLA_HEADS * GLA_DV
GLA_GATE_RANK, GLA_GATE_TAU, GLA_CHUNK = 16, 16.0, 64
N_MEM, MEM_HEADS, MEM_HEAD_DIM = 256, 4, 64
MEM_D = MEM_HEADS * MEM_HEAD_DIM
D_FF, FFN_CONV = 2816, 3
FFN_CHUNK = 256
FFN_ROWS = 256
PROJ_ROWS = 512
MEM_ROWS = 1024
FLASH_Q, FLASH_K = 1024, 256
GLA_ROWS = 256
SCAN_STREAMS_PROMPT, SCAN_STREAMS_SAMPLE = 4, 2

HI = lax.Precision.HIGHEST


def _cparams(*sem):
    return pltpu.CompilerParams(dimension_semantics=sem, vmem_limit_bytes=VMEM_LIMIT)


def _dot(a, b, precision=None):
    return jnp.dot(a, b, preferred_element_type=F32, precision=precision)


def _dot_nt(a, b):
    return lax.dot_general(a, b, (((1,), (1,)), ((), ())), preferred_element_type=F32)


def _dot_tn(a, b):
    return lax.dot_general(a, b, (((0,), (0,)), ((), ())), preferred_element_type=F32)


def _split3(x):
    x1 = x.astype(BF16)
    r1 = x - x1.astype(F32)
    x2 = r1.astype(BF16)
    x3 = (r1 - x2.astype(F32)).astype(BF16)
    return x1, x2, x3


def _dot_sel(sel, x):
    return sum(_dot(sel, t) for t in _split3(x))


def _dot_rep(x, sel):
    return sum(_dot(t, sel) for t in _split3(x))


def _seg_total(cs, n_seg, seg_len):
    return jnp.concatenate(
        [jnp.broadcast_to(cs[(i + 1) * seg_len - 1:(i + 1) * seg_len, :], (seg_len, cs.shape[1]))
         for i in range(n_seg)], axis=0)


def _rms(x, g):
    return x * lax.rsqrt(jnp.mean(x * x, axis=-1, keepdims=True) + EPS) * g


def _sigmoid(x):
    return 1.0 / (1.0 + jnp.exp(-x))


def _silu(x):
    return x * _sigmoid(x)


def _softplus(x):
    return jnp.maximum(x, 0.0) + jnp.log(1.0 + jnp.exp(-jnp.abs(x)))


def _log_sigmoid(x):
    return jnp.minimum(x, 0.0) - jnp.log(1.0 + jnp.exp(-jnp.abs(x)))


class _Stacked:
    def __init__(self, arr, layer):
        self.arr, self.layer = arr, layer

    @property
    def shape(self):
        return self.arr.shape[1:]


def _wspec(w, resident=False):
    shp, layer = w.shape, w.layer
    kw = dict(pipeline_mode=pl.Buffered(1)) if resident else {}
    return pl.BlockSpec((None,) + shp, lambda *_: (layer,) + (0,) * len(shp), **kw)


def _in_proj_body(x_ref, g_ref, wm_ref, wz_ref, wx_ref, wdt_ref, wg_ref, cos_ref, sin_ref,
                  qng_ref, wq_ref, gq_ref, kvg_ref, wk_ref, gk_ref, krg_ref, mh_ref,
                  z_out, xbc_out, dt_out, hg_out, q_out, k_out, ckv_out, cbf_out, kr_out):
    tm = x_ref.shape[0]
    xb = _rms(x_ref[...], g_ref[...]).astype(BF16)
    z_out[...] = _dot(xb, wz_ref[...])
    xbc_out[...] = _dot(xb, wx_ref[...])
    dt_out[...] = _dot(xb, wdt_ref[...])
    hg_out[...] = _dot(xb, wg_ref[...])
    h = _dot(xb, wm_ref[...])
    cq, ckv, krr = h[:, 0:256], h[:, 256:384], h[:, 384:512]
    cos, sin = cos_ref[...], sin_ref[...]
    lane = lax.broadcasted_iota(jnp.int32, (tm, LANES), 1)
    mh = mh_ref[...]

    def rope(xv):
        rot = jnp.where(lane < MLA_NOPE + MLA_ROPE // 2,
                        pltpu.roll(xv, LANES - MLA_ROPE // 2, 1), pltpu.roll(xv, MLA_ROPE // 2, 1))
        return xv * cos + rot * sin

    def seg_norm(xv, g):
        msq = _dot((xv * xv).astype(BF16), mh)
        return xv * lax.rsqrt(msq + EPS) * g

    q = _dot(_rms(cq, qng_ref[...]).astype(BF16), wq_ref[...])
    c = _rms(ckv, kvg_ref[...])
    cb = c.astype(BF16)
    ckv_out[...] = c
    cbf_out[...] = jnp.concatenate([cb, jnp.ones_like(cb)], axis=1)
    kr = rope(seg_norm(krr, krg_ref[...]))
    kr_out[...] = kr[:, MLA_NOPE:MLA_QK]
    k = _dot(cb, wk_ref[...])
    gq, gk = gq_ref[...], gk_ref[...]
    for hd in range(MLA_HEADS):
        sl = slice(hd * LANES, (hd + 1) * LANES)
        q_out[:, sl] = rope(seg_norm(q[:, sl], gq)).astype(BF16)
        k_out[:, sl] = (seg_norm(k[:, sl], gk) + kr).astype(BF16)


def in_proj(x, cos_t, sin_t, p, tm):
    rows, d = x.shape
    assert rows % tm == 0
    nt = cos_t.shape[0] // tm
    row_spec = lambda n: pl.BlockSpec((tm, n), lambda i: (i, 0))
    tab_spec = pl.BlockSpec((tm, LANES), lambda i: (i % nt, 0))
    hw = MLA_HEADS * LANES
    proj = ("norm_mix", "w_mla", "w_z", "w_xbc", "w_dt", "w_gla")
    prep = ("mla_q_norm", "wq_pad", "gq_vec", "mla_kv_norm", "wk_pad", "gk_vec", "gkr_vec", "mh")
    widths = [(p["w_z"].shape[1], F32), (p["w_xbc"].shape[1], F32), (p["w_dt"].shape[1], F32),
              (p["w_gla"].shape[1], F32), (hw, BF16), (hw, BF16), (LANES, F32), (2 * LANES, BF16), (MLA_ROPE, F32)]
    return pl.pallas_call(
        _in_proj_body, grid=(rows // tm,),
        in_specs=[row_spec(d)] + [_wspec(p[n]) for n in proj] + [tab_spec, tab_spec] + [_wspec(p[n]) for n in prep],
        out_specs=[row_spec(n) for n, _ in widths],
        out_shape=[jax.ShapeDtypeStruct((rows, n), dt) for n, dt in widths],
        compiler_params=_cparams("parallel"), name="in_proj",
    )(x, *[p[n].arr for n in proj], cos_t, sin_t, *[p[n].arr for n in prep])


def _mla_prompt_body(q_ref, k_ref, c_ref, wuv_ref, o_ref, m_ref, l_ref, acc_ref, olat_ref, *, tq, tk):
    qi = pl.program_id(1)

    def causal(n_rows):
        return (lax.broadcasted_iota(jnp.int32, (n_rows, tk), 1)
                <= lax.broadcasted_iota(jnp.int32, (n_rows, tk), 0))

    m_ref[...] = jnp.full(m_ref.shape, -jnp.inf, F32)
    l_ref[...] = jnp.zeros(l_ref.shape, F32)
    acc_ref[...] = jnp.zeros(acc_ref.shape, F32)

    def step(j, r0, masked):
        start = pl.multiple_of(j * tk, tk)
        rows = slice(r0, tq)
        cblk = c_ref[0, pl.ds(start, tk), :]
        for hd in range(MLA_HEADS):
            sl = slice(hd * LANES, (hd + 1) * LANES)
            s = _dot_nt(q_ref[0, rows, sl], k_ref[0, pl.ds(start, tk), sl])
            if masked:
                s = jnp.where(causal(tq - r0), s, -jnp.inf)
            m_old = m_ref[hd, rows, :]
            m_new = jnp.maximum(m_old, jnp.max(s, axis=-1, keepdims=True))
            p = jnp.exp2(s - jnp.concatenate([m_new] * (tk // LANES), axis=1))
            corr = jnp.exp2(m_old - m_new)
            pv = _dot(p.astype(BF16), cblk)
            acc_ref[hd, rows, :] = acc_ref[hd, rows, :] * corr + pv[:, :LANES]
            l_ref[hd, rows, :] = l_ref[hd, rows, :] * corr + pv[:, LANES:]
            m_ref[hd, rows, :] = m_new

    def body(j, carry):
        step(j, 0, False)
        return carry

    n_full = qi * (tq // tk)
    lax.fori_loop(0, n_full, body, 0)
    for d in range(tq // tk):
        step(n_full + d, d * tk, True)
    for hd in range(MLA_HEADS):
        olat_ref[:, hd * LANES:(hd + 1) * LANES] = (acc_ref[hd] / l_ref[hd]).astype(BF16)
    o_ref[0] = _dot(olat_ref[...], wuv_ref[...]).astype(o_ref.dtype)


def mla_prompt(q_cat, k_cat, c_bf, wuv_bd, tq, tk):
    b, L, hw = q_cat.shape
    assert tq % tk == 0 and L % tq == 0
    return pl.pallas_call(
        functools.partial(_mla_prompt_body, tq=tq, tk=tk), grid=(b, L // tq),
        in_specs=[pl.BlockSpec((1, tq, hw), lambda bi, qi: (bi, qi, 0)),
                  pl.BlockSpec((1, L, hw), lambda bi, qi: (bi, 0, 0)),
                  pl.BlockSpec((1, L, 2 * LANES), lambda bi, qi: (bi, 0, 0)),
                  _wspec(wuv_bd)],
        out_specs=pl.BlockSpec((1, tq, MLA_HEADS * MLA_V), lambda bi, qi: (bi, qi, 0)),
        out_shape=jax.ShapeDtypeStruct((b, L, MLA_HEADS * MLA_V), BF16),
        scratch_shapes=[pltpu.VMEM((MLA_HEADS, tq, LANES), F32), pltpu.VMEM((MLA_HEADS, tq, LANES), F32),
                        pltpu.VMEM((MLA_HEADS, tq, LANES), F32), pltpu.VMEM((tq, hw), BF16)],
        compiler_params=_cparams("parallel", "arbitrary"), name="mla_prompt",
    )(q_cat, k_cat, c_bf, wuv_bd.arr)


def _mem_kv_body(x_ref, wk_ref, wv_ref, m_ref, g_ref, k_out, v_out):
    xb = x_ref[...].astype(BF16)
    k = _dot(xb, wk_ref[...])
    msq = _dot((k * k).astype(BF16), m_ref[...])
    k_out[...] = k * lax.rsqrt(msq + EPS) * g_ref[...]
    v_out[...] = _dot(xb, wv_ref[...])


def mem_kv(mem2d, p, tm):
    rows, d = mem2d.shape
    row = lambda n: pl.BlockSpec((tm, n), lambda i: (i, 0))
    names = ("mem_wk", "mem_wv", "m64", "mem_k_gain")
    return pl.pallas_call(
        _mem_kv_body, grid=(rows // tm,),
        in_specs=[row(d)] + [_wspec(p[n]) for n in names],
        out_specs=[row(MEM_D), row(MEM_D)],
        out_shape=[jax.ShapeDtypeStruct((rows, MEM_D), F32)] * 2,
        compiler_params=_cparams("parallel"), name="mem_kv",
    )(mem2d, *[p[n].arr for n in names])


def _mem_attend_body(x_ref, *refs, nb, kv_t, n_pre):
    a_refs, w_refs = refs[:n_pre], refs[n_pre:2 * n_pre]
    g_ref, wq_ref, m_ref, gq_ref, k_ref, v_ref, wo_ref, o_ref = refs[2 * n_pre:]
    r = x_ref.shape[0]
    x = x_ref[...]
    for a_ref, w_ref in zip(a_refs, w_refs):
        x = x + _dot(a_ref[...], w_ref[...])
    q = _dot(_rms(x, g_ref[...]).astype(BF16), wq_ref[...])
    msq = _dot((q * q).astype(BF16), m_ref[...])
    qn = (q * lax.rsqrt(msq + EPS) * gq_ref[...]).astype(BF16)
    if kv_t:
        kb = jnp.concatenate([k_ref[b] for b in range(nb)], axis=1).astype(BF16)
        vb = jnp.concatenate([v_ref[b] for b in range(nb)], axis=1).astype(BF16)
    else:
        kb = k_ref[...].astype(BF16)
        vb = v_ref[...].astype(BF16)
    lane = lax.broadcasted_iota(jnp.int32, (r, MEM_D), 1) // MEM_HEAD_DIM
    if nb > 1:
        rb = lax.broadcasted_iota(jnp.int32, (r, nb * N_MEM), 0) // (r // nb)
        cb = lax.broadcasted_iota(jnp.int32, (r, nb * N_MEM), 1) // N_MEM
        same = rb == cb
    o = jnp.zeros((r, MEM_D), F32)
    for hd in range(MEM_HEADS):
        qm = jnp.where(lane == hd, qn, jnp.zeros_like(qn))
        s = _dot(qm, kb) if kv_t else _dot_nt(qm, kb)
        if nb > 1:
            s = jnp.where(same, s, -jnp.inf)
        pr = jnp.exp(s - jnp.max(s, axis=-1, keepdims=True))
        prb = pr.astype(BF16)
        pv = (_dot_nt(prb, vb) if kv_t else _dot(prb, vb)) / jnp.sum(pr, axis=-1, keepdims=True)
        o = jnp.where(lane == hd, pv, o)
    o_ref[...] = x + _dot(o.astype(BF16), wo_ref[...])


def mem_attend(x, acts, ws, k, v, p, r, nb, rows_per_batch, kv_t, layer=0):
    rows, d = x.shape
    row = lambda n: pl.BlockSpec((r, n), lambda i: (i, 0))
    names = ("norm_mem", "mem_wq", "m64", "mem_q_gain")
    if kv_t:
        kv_spec = pl.BlockSpec((None, nb, MEM_D, N_MEM), lambda i: (layer, i, 0, 0))
    elif nb == 1:
        kv_spec = pl.BlockSpec((N_MEM, MEM_D), lambda i: (i // (rows_per_batch // r), 0))
    else:
        kv_spec = pl.BlockSpec((nb * N_MEM, MEM_D), lambda i: (i, 0))
    return pl.pallas_call(
        functools.partial(_mem_attend_body, nb=nb, kv_t=kv_t, n_pre=len(acts)), grid=(rows // r,),
        in_specs=[row(d)] + [row(a.shape[1]) for a in acts] + [_wspec(w) for w in ws]
        + [_wspec(p[n]) for n in names] + [kv_spec, kv_spec, _wspec(p["mem_wo"])],
        out_specs=row(d),
        out_shape=jax.ShapeDtypeStruct((rows, d), F32),
        compiler_params=_cparams("parallel"), name="mem_attend",
    )(x, *acts, *[w.arr for w in ws], *[p[n].arr for n in names], k, v, p["mem_wo"].arr)


def _ffn_body(x_ref, g_ref, *refs, carry, tiles_per_seq, seg, tail_rows):
    if carry:
        wup_ref, wd_ref, cw_ref, cb_ref, o_ref, tail_ref, xn_ref, carry_ref = refs
        halo_ref = None
    else:
        halo_ref, wup_ref, wd_ref, cw_ref, cb_ref, o_ref, tail_ref, xn_ref = refs
        carry_ref = None
    tm = x_ref.shape[0]
    x = x_ref[...]
    xn_ref[...] = _rms(x, g_ref[...]).astype(BF16)
    row = lax.broadcasted_iota(jnp.int32, (tm, FFN_CHUNK), 0)
    if carry:
        @pl.when(pl.program_id(0) % tiles_per_seq == 0)
        def _():
            carry_ref[...] = jnp.zeros_like(carry_ref)
    else:
        t = row % seg
    n_chunks = D_FF // FFN_CHUNK

    def up(j):
        sl = slice(j * FFN_CHUNK, (j + 1) * FFN_CHUNK)
        xn = xn_ref[...]
        return (_dot(xn, wup_ref[:, sl]),
                _dot(xn, wup_ref[:, D_FF + j * FFN_CHUNK:D_FF + (j + 1) * FFN_CHUNK]))

    acc = x
    uv = up(0)
    for j in range(n_chunks):
        sl = slice(j * FFN_CHUNK, (j + 1) * FFN_CHUNK)
        u, v = uv
        if j + 1 < n_chunks:
            uv = up(j + 1)
        um1 = pltpu.roll(u, 1, 0)
        um2 = pltpu.roll(u, 2, 0)
        if carry:
            c6 = carry_ref[6:7, sl]
            c7 = carry_ref[7:8, sl]
            um1 = jnp.where(row == 0, c7, um1)
            um2 = jnp.where(row == 0, c6, jnp.where(row == 1, c7, um2))
            carry_ref[:, sl] = u[tm - SUBLANES:, :]
        else:
            hal = halo_ref[:, sl]
            um1 = jnp.where(t >= 1, um1, pltpu.roll(hal, tm - 1, 0))
            um2 = jnp.where(t >= 2, um2, hal)
        tail_ref[:, sl] = u[tm - tail_rows:, :]
        uc = um2 * cw_ref[0:1, sl] + um1 * cw_ref[1:2, sl] + u * cw_ref[2:3, sl] + cb_ref[:, sl]
        a = (_silu(uc) * v).astype(BF16)
        acc = acc + _dot(a, wd_ref[sl, :])
    o_ref[...] = acc


def ffn(x, halo, p, tm, carry, tiles_per_seq, seg, tail_rows):
    rows, d = x.shape
    n_tiles = rows // tm
    row_spec = lambda n: pl.BlockSpec((tm, n), lambda i: (i, 0))
    in_specs = [row_spec(d), _wspec(p["norm_ffn"])]
    args = [x, p["norm_ffn"].arr]
    if not carry:
        in_specs.append(row_spec(D_FF))
        args.append(halo)
    in_specs += [_wspec(p[n], resident=True) for n in ("ffn_wup", "ffn_wd")]
    in_specs += [_wspec(p["ffn_conv_w"]), _wspec(p["ffn_conv_b"])]
    args += [p[n].arr for n in ("ffn_wup", "ffn_wd", "ffn_conv_w", "ffn_conv_b")]
    scratch = [pltpu.VMEM((tm, d), BF16)]
    if carry:
        scratch.append(pltpu.VMEM((SUBLANES, D_FF), F32))
    return pl.pallas_call(
        functools.partial(_ffn_body, carry=carry, tiles_per_seq=tiles_per_seq, seg=seg, tail_rows=tail_rows),
        grid=(n_tiles,), in_specs=in_specs,
        out_specs=[row_spec(d), pl.BlockSpec((tail_rows, D_FF), lambda i: (i, 0))],
        out_shape=[jax.ShapeDtypeStruct((rows, d), F32), jax.ShapeDtypeStruct((n_tiles * tail_rows, D_FF), F32)],
        scratch_shapes=scratch,
        compiler_params=_cparams("arbitrary"), name="ffn",
    )(*args)


def _ssd_body(*refs, ns, nb, seg, carry, has_state):
    z_ref, xbc_ref, dtr_ref = refs[0:3]
    i = 3
    halo_ref = None
    if not carry:
        halo_ref = refs[i]
        i += 1
    consts = refs[i:i + 7]
    i += 7
    h0_ref = None
    if has_state:
        h0_ref = refs[i]
        i += 1
    o_ref, h_ref = refs[i:i + 2]
    tail_ref = refs[i + 2] if carry else None

    @pl.when(pl.program_id(1) == 0)
    def _():
        h_ref[...] = h0_ref[...] if has_state else jnp.zeros_like(h_ref)
        if carry:
            tail_ref[...] = jnp.zeros_like(tail_ref)

    for s in range(ns):
        _ssd_stream(z_ref.at[s], xbc_ref.at[s], dtr_ref.at[s], None if carry else halo_ref.at[s], consts,
                    o_ref.at[s], h_ref.at[s], tail_ref.at[s] if carry else None, nb=nb, seg=seg, carry=carry)


def _ssd_stream(z_ref, xbc_ref, dtr_ref, halo_ref, consts, o_ref, h_ref, tail_ref, *, nb, seg, carry):
    cw_ref, cb_ref, dtb_ref, alog_ref, d_ref, ng_ref, rep_ref = consts
    r = z_ref.shape[0]
    hp = SSD_HEADS * SSD_HEADDIM
    gp = hp // SSD_GROUPS

    x = xbc_ref[...]
    row = lax.broadcasted_iota(jnp.int32, (r, SSD_CONV_DIM), 0)
    row8 = lax.broadcasted_iota(jnp.int32, (SUBLANES, SSD_CONV_DIM), 0)

    def prev(k):
        xs = pltpu.roll(x, k, 0)
        if carry:
            tl = pltpu.roll(tail_ref[...], k, 0)
            top = jnp.where(row8 < k, tl, xs[:SUBLANES])
            return jnp.concatenate([top, xs[SUBLANES:]], axis=0)
        sh = SSD_CONV - 1 - k
        hal = halo_ref[...]
        hs_ = hal if sh == 0 else pltpu.roll(hal, r - sh, 0)
        return jnp.where(row % seg >= k, xs, hs_)

    u = (cb_ref[...] + x * cw_ref[3:4, :] + prev(1) * cw_ref[2:3, :] + prev(2) * cw_ref[1:2, :]
         + prev(3) * cw_ref[0:1, :])
    if carry:
        tail_ref[...] = x[r - SUBLANES:, :]
    u = _silu(u)

    ri = lax.broadcasted_iota(jnp.int32, (r, r), 0)
    ci = lax.broadcasted_iota(jnp.int32, (r, r), 1)
    if nb > 1:
        tril = jnp.logical_and((ri // seg) == (ci // seg), ci <= ri)
    else:
        tril = ci <= ri
    tri = jnp.where(tril, 1.0, 0.0).astype(BF16)

    dtc = _softplus(dtr_ref[...] + dtb_ref[...])
    csc = _dot_sel(tri, dtc * (-jnp.exp(alog_ref[...])))
    dt = _dot_rep(dtc, rep_ref[...])
    cs = _dot_rep(csc, rep_ref[...])
    tot = _seg_total(cs, nb, seg) if nb > 1 else _seg_total(cs, 1, r)

    lane = lax.broadcasted_iota(jnp.int32, (r, LANES), 1)
    rb = lax.broadcasted_iota(jnp.int32, (r, LANES), 0) // seg
    lo = lane < SSD_HEADDIM

    def pair(a, b):
        return jnp.where(lo, a, b)

    def hs(a, hd):
        return a[:, hd * LANES:(hd + 1) * LANES]

    zz = z_ref[...]
    hall = h_ref[...].astype(BF16)
    for g in range(SSD_GROUPS):
        h0i, h1i = 2 * g, 2 * g + 1
        gs = slice(g * LANES, (g + 1) * LANES)
        xg = u[:, gs]
        bg = u[:, SSD_D + g * SSD_STATE:SSD_D + (g + 1) * SSD_STATE].astype(BF16)
        cg = u[:, SSD_D + SSD_GROUPS * SSD_STATE + g * SSD_STATE:
               SSD_D + SSD_GROUPS * SSD_STATE + (g + 1) * SSD_STATE].astype(BF16)
        cbm = _dot_nt(cg, bg)
        w0 = (cbm * jnp.where(tril, jnp.exp(hs(cs, h0i) - hs(cs, h0i).T), 0.0)).astype(BF16)
        w1 = (cbm * jnp.where(tril, jnp.exp(hs(cs, h1i) - hs(cs, h1i).T), 0.0)).astype(BF16)
        dtp = pair(hs(dt, h0i), hs(dt, h1i))
        csp = pair(hs(cs, h0i), hs(cs, h1i))
        totp = pair(hs(tot, h0i), hs(tot, h1i))
        xdt = (xg * dtp).astype(BF16)
        y = pair(_dot(w0, xdt), _dot(w1, xdt))
        zst = _dot_nt(cg, hall)
        if nb == 1:
            yst = zst[:, g * gp:(g + 1) * gp]
        else:
            yst = jnp.zeros((r, gp), F32)
            for b in range(nb):
                yst = jnp.where(rb == b, zst[:, b * hp + g * gp:b * hp + (g + 1) * gp], yst)
        y = y + yst * jnp.exp(csp) + d_ref[:, gs] * xg
        y = y * _silu(zz[:, gs])
        o_ref[:, gs] = _rms(y, ng_ref[:, gs]).astype(o_ref.dtype)

        xw = (xg * (jnp.exp(totp - csp) * dtp)).astype(BF16)
        if nb > 1:
            xw = jnp.concatenate([jnp.where(rb == b, xw, jnp.zeros_like(xw)) for b in range(nb)], axis=1)
        dh = _dot_tn(xw, bg)
        for b in range(nb):
            r0 = b * seg if nb > 1 else 0
            dec = jnp.concatenate(
                [jnp.broadcast_to(jnp.exp(hs(tot, h0i)[r0:r0 + 1, :]), (SSD_HEADDIM, LANES)),
                 jnp.broadcast_to(jnp.exp(hs(tot, h1i)[r0:r0 + 1, :]), (SSD_HEADDIM, LANES))], axis=0)
            sl = slice(b * hp + g * gp, b * hp + (g + 1) * gp)
            h_ref[sl, :] = h_ref[sl, :] * dec + dh[b * gp:(b + 1) * gp, :]


def ssd(z, xbc, dtr, halo, h0, p, nb, seg, ns, layer=0):
    n_groups, lg, _ = z.shape
    r = SSD_CHUNK
    carry = halo is None
    n_chunks = lg // r
    assert lg % r == 0 and n_groups % ns == 0 and (carry or n_chunks == 1)
    hp = SSD_HEADS * SSD_HEADDIM
    row_spec = lambda n: pl.BlockSpec((ns, r, n), lambda b, c: (b, c, 0))
    st_spec = pl.BlockSpec((ns, nb * hp, SSD_STATE), lambda b, c: (b, 0, 0))
    in_specs = [row_spec(SSD_D), row_spec(SSD_CONV_DIM), row_spec(LANES)]
    args = [z, xbc, dtr]
    if not carry:
        in_specs.append(row_spec(SSD_CONV_DIM))
        args.append(halo)
    names = ("ssd_conv_w", "ssd_conv_b", "ssd_dt_bias", "ssd_a_log", "ssd_d", "ssd_norm", "ssd_rep")
    in_specs += [_wspec(p[n]) for n in names]
    args += [p[n].arr for n in names]
    if h0 is not None:
        in_specs.append(pl.BlockSpec((None, ns, nb * hp, SSD_STATE), lambda b, c: (layer, b, 0, 0)))
        args.append(h0)
    scratch = [pltpu.VMEM((ns, SUBLANES, SSD_CONV_DIM), F32)] if carry else []
    return pl.pallas_call(
        functools.partial(_ssd_body, ns=ns, nb=nb, seg=seg, carry=carry, has_state=h0 is not None),
        grid=(n_groups // ns, n_chunks), in_specs=in_specs,
        out_specs=[row_spec(SSD_D), st_spec],
        out_shape=[jax.ShapeDtypeStruct((n_groups, lg, SSD_D), BF16),
                   jax.ShapeDtypeStruct((n_groups, nb * hp, SSD_STATE), F32)],
        scratch_shapes=scratch,
        compiler_params=_cparams("parallel", "arbitrary"), name="ssd",
    )(*args)


def _gla_body(*refs, ns, nb, seg, c, has_state):
    hg3_ref, wg_ref, bg_ref, m_ref, ng_ref = refs[0:5]
    i = 5
    s0_ref = None
    if has_state:
        s0_ref = refs[i]
        i += 1
    o3_ref, stc_ref, st3_ref = refs[i:i + 3]
    r = hg3_ref.shape[1]

    def head_blocks():
        for s in range(ns):
            for b in range(nb):
                for hd in range(GLA_HEADS):
                    yield s, slice(b * GLA_VD + hd * GLA_DV, b * GLA_VD + (hd + 1) * GLA_DV), \
                        slice(hd * GLA_DK, (hd + 1) * GLA_DK)

    @pl.when(pl.program_id(1) == 0)
    def _():
        st3_ref[...] = jnp.zeros_like(st3_ref)
        if has_state:
            for s, rows, lanes in head_blocks():
                st3_ref[s, rows, lanes] = s0_ref[s, rows, :]

    sl_ = seg if nb > 1 else c
    ri = lax.broadcasted_iota(jnp.int32, (r, r), 0)
    ci = lax.broadcasted_iota(jnp.int32, (r, r), 1)
    same = (ri // sl_) == (ci // sl_)
    tril = jnp.logical_and(same, ci <= ri)
    tri = jnp.where(tril, 1.0, 0.0).astype(BF16)
    klane = lax.broadcasted_iota(jnp.int32, (r, GLA_KD), 1) // GLA_DK
    vlane = lax.broadcasted_iota(jnp.int32, (r, GLA_VD), 1) // GLA_DV
    rbv = lax.broadcasted_iota(jnp.int32, (r, GLA_VD), 0) // seg
    blk = (lax.broadcasted_iota(jnp.int32, (GLA_VD, GLA_KD), 0) // GLA_DV
           == lax.broadcasted_iota(jnp.int32, (GLA_VD, GLA_KD), 1) // GLA_DK)

    for s in range(ns):
        hg_ref, o_ref, st_ref = hg3_ref.at[s], o3_ref.at[s], st3_ref.at[s]
        q = hg_ref[:, 0:GLA_KD] * (GLA_DK ** -0.5)
        k = hg_ref[:, GLA_KD:2 * GLA_KD]
        vb = hg_ref[:, 2 * GLA_KD:2 * GLA_KD + GLA_VD].astype(BF16)
        gg = hg_ref[:, 2 * GLA_KD + GLA_VD:2 * GLA_KD + 2 * GLA_VD]
        glr = hg_ref[:, 2 * GLA_KD + 2 * GLA_VD:2 * GLA_KD + 2 * GLA_VD + LANES].astype(BF16)
        gate = _log_sigmoid(_dot(glr, wg_ref[...]) + bg_ref[...]) * (1.0 / GLA_GATE_TAU)
        bc = _dot_sel(tri, gate)
        tot = _seg_total(bc, r // sl_, sl_)
        qt = (q * jnp.exp(bc)).astype(BF16)
        kt = (k * jnp.exp(-bc)).astype(BF16)
        kd = (k * jnp.exp(tot - bc)).astype(BF16)
        etot = jnp.exp(tot)
        o = jnp.zeros((r, GLA_VD), F32)
        for hd in range(GLA_HEADS):
            a = _dot_nt(jnp.where(klane == hd, qt, jnp.zeros_like(qt)), kt)
            a = jnp.where(tril, a, 0.0).astype(BF16)
            o = o + _dot(a, jnp.where(vlane == hd, vb, jnp.zeros_like(vb)))
        if nb > 1:
            zs = _dot_nt(qt, st_ref[...].astype(BF16))
            ost = jnp.zeros((r, GLA_VD), F32)
            for b in range(nb):
                ost = jnp.where(rbv == b, zs[:, b * GLA_VD:(b + 1) * GLA_VD], ost)
            o = o + ost
            vexp = jnp.concatenate([jnp.where(rbv == b, vb, jnp.zeros_like(vb)) for b in range(nb)], axis=1)
            ds = _dot_tn(vexp, kd)
            for b in range(nb):
                sl = slice(b * GLA_VD, (b + 1) * GLA_VD)
                st_ref[sl, :] = st_ref[sl, :] * etot[b * seg:b * seg + 1, :] + jnp.where(blk, ds[sl, :], 0.0)
        else:
            st = st_ref[...]
            parts = []
            for sub in range(r // c):
                rs = slice(sub * c, (sub + 1) * c)
                parts.append(_dot_nt(qt[rs, :], st.astype(BF16)))
                st = st * etot[sub * c:sub * c + 1, :] + jnp.where(blk, _dot_tn(vb[rs, :], kd[rs, :]), 0.0)
            st_ref[...] = st
            o = o + jnp.concatenate(parts, axis=0)
        msq = _dot((o * o).astype(BF16), m_ref[...])
        on = o * lax.rsqrt(msq + EPS) * ng_ref[...]
        o_ref[...] = (on * _silu(gg)).astype(o_ref.dtype)

    @pl.when(pl.program_id(1) == pl.num_programs(1) - 1)
    def _():
        for s, rows, lanes in head_blocks():
            stc_ref[s, rows, :] = st3_ref[s, rows, lanes]


def gla(hg, s0, p, nb, seg, c, r, ns):
    n_groups, lg, width = hg.shape
    n_steps = lg // r
    assert lg % r == 0 and r % c == 0 and n_groups % ns == 0
    row_spec = lambda n: pl.BlockSpec((ns, r, n), lambda b, s: (b, s, 0))
    st_spec = pl.BlockSpec((ns, nb * GLA_VD, GLA_DK), lambda b, s: (b, 0, 0))
    names = ("gla_w_gate", "gla_b_gate", "m64", "gla_norm")
    in_specs = [row_spec(width)] + [_wspec(p[n]) for n in names]
    args = [hg] + [p[n].arr for n in names]
    if s0 is not None:
        in_specs.append(st_spec)
        args.append(s0)
    return pl.pallas_call(
        functools.partial(_gla_body, ns=ns, nb=nb, seg=seg, c=c, has_state=s0 is not None),
        grid=(n_groups // ns, n_steps), in_specs=in_specs,
        out_specs=[row_spec(GLA_VD), st_spec],
        out_shape=[jax.ShapeDtypeStruct((n_groups, lg, GLA_VD), BF16),
                   jax.ShapeDtypeStruct((n_groups, nb * GLA_VD, GLA_DK), F32)],
        scratch_shapes=[pltpu.VMEM((ns, nb * GLA_VD, GLA_KD), F32)],
        compiler_params=_cparams("parallel", "arbitrary"), name="gla",
    )(*args)


def _q_absorb_body(q_ref, wt_ref, sel_ref, qa_out, qr_out):
    for hd in range(MLA_HEADS):
        qh = q_ref[:, hd * LANES:(hd + 1) * LANES].astype(F32)
        qa_out[:, hd * LANES:(hd + 1) * LANES] = _dot(qh, wt_ref[hd], HI).astype(BF16)
        qr_out[:, hd * LANES:(hd + 1) * LANES] = _dot(qh, sel_ref[...], HI).astype(BF16)


def q_absorb(q_cat, p, tm):
    rows, hw = q_cat.shape
    row_spec = pl.BlockSpec((tm, hw), lambda i: (i, 0))
    return pl.pallas_call(
        _q_absorb_body, grid=(rows // tm,),
        in_specs=[row_spec, _wspec(p["wukt_g"]), _wspec(p["rope_sel"])],
        out_specs=[row_spec, row_spec],
        out_shape=[jax.ShapeDtypeStruct((rows, hw), BF16)] * 2,
        compiler_params=_cparams("parallel"), name="q_absorb",
    )(q_cat, p["wukt_g"].arr, p["rope_sel"].arr)


PAGED_CP = 64
PAGED_SUB = 512


def _mla_paged_body(pt_ref, ptn_ref, wukt_ref, qa_ref, qr_ref, cnew_ref, krnew_ref, wuv_ref, lat_hbm, kr_hbm,
                    o_ref, lat_buf, kr_buf, sem, m_ref, l_ref, acc_ref, *, layer, n_pages, n_batch, seq, cp):
    bi = pl.program_id(0)
    nc = n_pages // cp
    nq = MLA_HEADS * seq
    nk = MLA_HEADS * MLA_NOPE

    def copies(table, chunk, slot):
        out = []
        for pg in range(cp):
            pid = table[0, 0, chunk * cp + pg]
            dst = pl.ds(pg * PAGE_SIZE, PAGE_SIZE)
            out.append(pltpu.make_async_copy(lat_hbm.at[layer, pid], lat_buf.at[slot, dst, :], sem.at[0, slot]))
            out.append(pltpu.make_async_copy(kr_hbm.at[layer, pid], kr_buf.at[slot, :, dst], sem.at[1, slot]))
        return out

    def start(table, chunk, slot):
        for cpy in copies(table, chunk, slot):
            cpy.start()

    def wait(slot):
        for cpy in copies(pt_ref, 0, slot):
            cpy.wait()

    @pl.when(bi == 0)
    def _():
        start(pt_ref, 0, 0)

    m_ref[...] = jnp.full(m_ref.shape, -jnp.inf, F32)
    l_ref[...] = jnp.zeros(l_ref.shape, F32)
    acc_ref[...] = jnp.zeros(acc_ref.shape, F32)
    lhs = jnp.concatenate([wukt_ref[...], qa_ref[0]], axis=0)
    qr = qr_ref[0]

    def scores(latb, krt):
        big = _dot_nt(lhs, latb)
        kt = big[0:nk, :]
        ss = jnp.sum((kt * kt).reshape(MLA_HEADS, MLA_NOPE, kt.shape[1]), axis=1) * (1.0 / MLA_NOPE)
        rinv = lax.rsqrt(ss + EPS)
        rexp = jnp.concatenate([jnp.broadcast_to(rinv[hd:hd + 1, :], (seq, rinv.shape[1]))
                                for hd in range(MLA_HEADS)], axis=0)
        return big[nk:, :] * rexp + _dot(qr, krt.astype(BF16))

    def attend(lat, krt, mask):
        keys = lat.shape[0]
        sub = min(PAGED_SUB, keys)
        latb = lat.astype(BF16)
        s = jnp.concatenate([scores(latb[i * sub:(i + 1) * sub, :], krt[:, i * sub:(i + 1) * sub])
                             for i in range(keys // sub)], axis=1)
        if mask is not None:
            s = jnp.where(mask, s, -jnp.inf)
        m_old = m_ref[...]
        m_new = jnp.maximum(m_old, jnp.max(s, axis=-1, keepdims=True))
        pr = jnp.exp2(s - jnp.concatenate([m_new] * (keys // LANES), axis=1))
        corr = jnp.exp2(m_old - m_new)
        l_ref[...] = l_ref[...] * corr + jnp.sum(pr, axis=-1, keepdims=True)
        acc_ref[...] = acc_ref[...] * corr + _dot(pr.astype(BF16), latb)
        m_ref[...] = m_new

    lat_new = jnp.concatenate([cnew_ref[0], jnp.zeros((LANES - seq, MLA_KV_RANK), F32)], axis=0)
    qtok = lax.broadcasted_iota(jnp.int32, (nq, LANES), 0) % seq
    key = lax.broadcasted_iota(jnp.int32, (nq, LANES), 1)
    attend(lat_new, krnew_ref[0], key <= qtok)

    def step(slot, prefetch):
        prefetch()
        wait(slot)
        attend(lat_buf[slot], kr_buf[slot], None)

    def pair_body(jj, carry_):
        step(0, lambda: start(pt_ref, 2 * jj + 1, 1))
        step(1, lambda: start(pt_ref, 2 * jj + 2, 0))
        return carry_

    lax.fori_loop(0, nc // 2 - 1, pair_body, 0)
    step(0, lambda: start(pt_ref, nc - 1, 1))

    def next_batch_prefetch():
        @pl.when(bi + 1 < n_batch)
        def _():
            start(ptn_ref, 0, 0)

    step(1, next_batch_prefetch)

    olat = (acc_ref[...] / l_ref[...]).astype(BF16)
    o = jnp.zeros((seq, MLA_HEADS * MLA_V), F32)
    for hd in range(MLA_HEADS):
        o = o + _dot(olat[hd * seq:(hd + 1) * seq, :], wuv_ref[hd * LANES:(hd + 1) * LANES, :])
    o_ref[0] = o.astype(o_ref.dtype)


def mla_paged(page_table, wukt, qa, qr, c_new, krt_new, wuv_bd, cache_lat, cache_krt, layer):
    n_batch, n_pages = page_table.shape
    seq = c_new.shape[1]
    cp = min(PAGED_CP, n_pages // 2)
    assert n_pages % (2 * cp) == 0
    nq = MLA_HEADS * seq
    pt3 = page_table.reshape(n_batch, 1, n_pages)
    smem_spec = lambda f: pl.BlockSpec((1, 1, n_pages), f, memory_space=pltpu.SMEM)
    per_b = lambda shp: pl.BlockSpec((1,) + shp, lambda b: (b, 0, 0))
    kc = cp * PAGE_SIZE
    return pl.pallas_call(
        functools.partial(_mla_paged_body, layer=layer, n_pages=n_pages, n_batch=n_batch, seq=seq, cp=cp),
        grid=(n_batch,),
        in_specs=[smem_spec(lambda b: (b, 0, 0)),
                  smem_spec(lambda b: (jnp.minimum(b + 1, n_batch - 1), 0, 0)),
                  _wspec(wukt), per_b(qa.shape[1:]), per_b(qr.shape[1:]), per_b(c_new.shape[1:]),
                  per_b(krt_new.shape[1:]),
                  _wspec(wuv_bd),
                  pl.BlockSpec(memory_space=pl.ANY), pl.BlockSpec(memory_space=pl.ANY)],
        out_specs=per_b((seq, MLA_HEADS * MLA_V)),
        out_shape=jax.ShapeDtypeStruct((n_batch, seq, MLA_HEADS * MLA_V), BF16),
        scratch_shapes=[pltpu.VMEM((2, kc, MLA_KV_RANK), F32), pltpu.VMEM((2, MLA_ROPE, kc), F32),
                        pltpu.SemaphoreType.DMA((2, 2)),
                        pltpu.VMEM((nq, LANES), F32), pltpu.VMEM((nq, LANES), F32), pltpu.VMEM((nq, LANES), F32)],
        compiler_params=_cparams("arbitrary"), name="mla_paged",
    )(pt3, pt3, wukt.arr, qa, qr, c_new, krt_new, wuv_bd.arr, cache_lat, cache_krt)


def _block_diag_mean(n, blocks):
    idx = jnp.arange(n)
    m = jnp.zeros((n, n), F32)
    for start, size in blocks:
        inb = jnp.logical_and(idx >= start, idx < start + size)
        m = m + jnp.where(jnp.logical_and(inb[:, None], inb[None, :]), 1.0 / size, 0.0)
    return m.astype(BF16)


def _rope_tables(pos):
    half = MLA_ROPE // 2
    inv = ROPE_THETA ** (-jnp.arange(half, dtype=F32) / half)
    ang = pos.astype(F32)[:, None] * inv[None, :]
    cos, sin = jnp.cos(ang), jnp.sin(ang)
    n = pos.shape[0]
    one, zero = jnp.ones((n, MLA_NOPE), F32), jnp.zeros((n, MLA_NOPE), F32)
    tail1, tail0 = jnp.ones((n, LANES - MLA_QK), F32), jnp.zeros((n, LANES - MLA_QK), F32)
    return (jnp.concatenate([one, cos, cos, tail1], axis=1), jnp.concatenate([zero, -sin, sin, tail0], axis=1))


def _pack_one(w):
    f = lambda name: w[name]
    bf = lambda a: a.astype(BF16)
    p = {}
    w_in = f("w_in")
    o = 0
    cols = {}
    for name, size in (("cq", MLA_Q_RANK), ("ckv", MLA_KV_RANK), ("krr", MLA_ROPE), ("z", SSD_D), ("xbc", SSD_CONV_DIM),
                       ("dt", SSD_HEADS), ("gq", GLA_KD), ("gk", GLA_KD), ("gv", GLA_VD), ("glr", GLA_GATE_RANK),
                       ("gg", GLA_VD)):
        cols[name] = w_in[:, o:o + size]
        o += size
    d = w_in.shape[0]

    def place(a, axis, lo, total):
        cfg = [(0, 0)] * a.ndim
        cfg[axis] = (lo, total - lo - a.shape[axis])
        return jnp.pad(a, cfg)

    krr_pad = place(cols["krr"], 1, MLA_NOPE, LANES)
    glr_pad = place(cols["glr"], 1, 0, LANES)
    p["w_mla"] = bf(jnp.concatenate([cols["cq"], cols["ckv"], krr_pad], axis=1))
    p["w_z"] = bf(cols["z"])
    p["w_xbc"] = bf(cols["xbc"])
    p["w_dt"] = bf(place(cols["dt"], 1, 0, LANES))
    p["w_gla"] = bf(jnp.concatenate([cols["gq"], cols["gk"], cols["gv"], cols["gg"], glr_pad], axis=1))
    p["norm_mix"] = f("norm_mix").reshape(1, d)

    def head_pad(wm, n_real):
        return place(wm, 2, 0, LANES).reshape(wm.shape[0], MLA_HEADS * LANES)

    p["wq_pad"] = bf(head_pad(f("mla_w_uq"), MLA_QK))
    p["wk_pad"] = bf(head_pad(f("mla_w_uk"), MLA_NOPE))
    scale = MLA_QK ** -0.5 * math.log2(math.e)
    zpad = jnp.zeros((LANES - MLA_QK,), F32)
    p["gq_vec"] = (jnp.concatenate([f("mla_qn_norm"), f("mla_qr_norm"), zpad]) * scale).reshape(1, LANES)
    p["gk_vec"] = jnp.concatenate([f("mla_kn_norm"), jnp.zeros((LANES - MLA_NOPE,), F32)]).reshape(1, LANES)
    p["gkr_vec"] = jnp.concatenate([jnp.zeros((MLA_NOPE,), F32), f("mla_kr_norm"), zpad]).reshape(1, LANES)
    p["mla_q_norm"] = f("mla_q_norm").reshape(1, MLA_Q_RANK)
    p["mla_kv_norm"] = f("mla_kv_norm").reshape(1, MLA_KV_RANK)
    p["mh"] = _block_diag_mean(LANES, [(0, MLA_NOPE), (MLA_NOPE, MLA_ROPE)])
    w_uv = f("mla_w_uv")
    p["wuv_bd"] = bf(jnp.concatenate(
        [place(w_uv[:, hd, :], 1, hd * MLA_V, MLA_HEADS * MLA_V) for hd in range(MLA_HEADS)], axis=0))
    w_uk = f("mla_w_uk")
    wt = jnp.transpose(w_uk, (1, 2, 0)) * f("mla_kn_norm")[None, :, None]
    p["wukt_g"] = place(wt, 1, 0, LANES)
    p["wukt"] = bf(jnp.transpose(w_uk, (1, 2, 0)).reshape(MLA_HEADS * MLA_NOPE, MLA_KV_RANK))
    lane_i = jnp.arange(LANES)
    p["rope_sel"] = jnp.where(jnp.logical_and(lane_i[:, None] == lane_i[None, :] + MLA_NOPE,
                                              lane_i[None, :] < MLA_ROPE), 1.0, 0.0).astype(F32)

    p["ssd_conv_w"] = place(f("ssd_conv_w"), 0, 0, SUBLANES)
    p["ssd_conv_b"] = f("ssd_conv_b").reshape(1, SSD_CONV_DIM)
    lane_pad = lambda v: place(v, 0, 0, LANES).reshape(1, LANES)
    p["ssd_dt_bias"] = lane_pad(f("ssd_dt_bias"))
    p["ssd_a_log"] = lane_pad(f("ssd_a_log"))
    p["ssd_rep"] = bf(jnp.repeat(jnp.eye(LANES, SSD_HEADS, dtype=F32), LANES, axis=1))
    p["ssd_d"] = jnp.repeat(f("ssd_d"), SSD_HEADDIM).reshape(1, SSD_D)
    p["ssd_norm"] = f("ssd_norm").reshape(1, SSD_D)
    p["gla_w_gate"] = bf(place(f("gla_w_gate"), 0, 0, LANES))
    p["gla_b_gate"] = f("gla_b_gate").reshape(1, GLA_KD)
    p["gla_norm"] = f("gla_norm").reshape(1, GLA_VD)
    p["m64"] = _block_diag_mean(MEM_D, [(i * 64, 64) for i in range(4)])
    w_o = f("w_o")
    p["wo_mla"] = bf(w_o[:MLA_HEADS * MLA_V])
    p["wo_ssd"] = bf(w_o[MLA_HEADS * MLA_V:MLA_HEADS * MLA_V + SSD_D])
    p["wo_gla"] = bf(w_o[MLA_HEADS * MLA_V + SSD_D:])
    p["norm_mem"] = f("norm_mem").reshape(1, D_MODEL)
    p["mem_wq"], p["mem_wk"], p["mem_wv"], p["mem_wo"] = bf(f("mem_wq")), bf(f("mem_wk")), bf(f("mem_wv")), bf(f("mem_wo"))
    p["mem_q_gain"] = (jnp.tile(f("mem_q_norm"), MEM_HEADS) * MEM_HEAD_DIM ** -0.5).reshape(1, MEM_D)
    p["mem_k_gain"] = jnp.tile(f("mem_k_norm"), MEM_HEADS).reshape(1, MEM_D)
    p["norm_ffn"] = f("norm_ffn").reshape(1, D_MODEL)
    w_up = f("ffn_w_up")
    p["ffn_wup"], p["ffn_wd"] = bf(w_up), bf(f("ffn_w_down"))
    p["ffn_conv_w"] = place(f("ffn_conv_w"), 0, 0, SUBLANES)
    p["ffn_conv_b"] = f("ffn_conv_b").reshape(1, D_FF)
    return p


def _pad_rows(state, seq):
    b, k, c = state.shape
    return jnp.pad(state, ((0, 0), (0, seq - k), (0, 0))).reshape(b * seq, c)


def _layer(x3, p, cos_t, sin_t, tm, sample, mem2d=None):
    b, L, d = x3.shape
    rows = b * L
    x = x3.reshape(rows, d)
    z, xbc, dtr, h_gla, q_cat, k_cat, c_kv, c_bf, kr = in_proj(x, cos_t, sin_t, p, tm)
    hw = MLA_HEADS * LANES
    if sample is None:
        o_mla = mla_prompt(q_cat.reshape(b, L, hw), k_cat.reshape(b, L, hw), c_bf.reshape(b, L, 2 * LANES),
                           p["wuv_bd"], math.gcd(L, FLASH_Q), FLASH_K)
        o_mla = o_mla.reshape(rows, MLA_HEADS * MLA_V)
        g3 = lambda a: a.reshape(b, L, a.shape[-1])
        ns = math.gcd(b, SCAN_STREAMS_PROMPT)
        o_ssd, ssd_h = ssd(g3(z), g3(xbc), g3(dtr), None, None, p, 1, L, ns)
        o_gla, gla_st = gla(g3(h_gla), None, p, 1, L, GLA_CHUNK, GLA_ROWS, ns)
    else:
        qa, qrp = q_absorb(q_cat, p, tm)
        to_hq = lambda a, n: a.reshape(b, L, MLA_HEADS, LANES)[..., :n].transpose(0, 2, 1, 3).reshape(b, MLA_HEADS * L, n)
        krt_new = jnp.pad(jnp.swapaxes(kr.reshape(b, L, MLA_ROPE), 1, 2), ((0, 0), (0, 0), (0, LANES - L)))
        o_mla = mla_paged(sample["page_table"], p["wukt"], to_hq(qa, LANES), to_hq(qrp, MLA_ROPE),
                          c_kv.reshape(b, L, LANES), krt_new, p["wuv_bd"], sample["cache_lat"], sample["cache_krt"],
                          sample["layer"])
        o_mla = o_mla.reshape(rows, MLA_HEADS * MLA_V)
        nb = SSD_CHUNK // L
        ng = b // nb
        g3 = lambda a: a.reshape(ng, nb * L, a.shape[-1])
        ns = math.gcd(ng, SCAN_STREAMS_SAMPLE)
        o_ssd, ssd_h = ssd(g3(z), g3(xbc), g3(dtr), g3(_pad_rows(sample["ssd_conv"], L)),
                           sample["ssd_all"].reshape(-1, ng, nb * SSD_D, SSD_STATE), p, nb, L, ns, sample["layer"])
        st0 = jnp.swapaxes(sample["gla"], 2, 3).reshape(ng, nb * GLA_VD, GLA_DK)
        o_gla, gla_st = gla(g3(h_gla), st0, p, nb, L, nb * L, nb * L, ns)
    o_ssd, o_gla = o_ssd.reshape(rows, SSD_D), o_gla.reshape(rows, GLA_VD)
    mix, w_mix = [o_mla, o_ssd, o_gla], [p["wo_mla"], p["wo_ssd"], p["wo_gla"]]
    if sample is None:
        mk, mv = mem_kv(mem2d, p, PROJ_ROWS)
        x = mem_attend(x, mix, w_mix, mk, mv, p, math.gcd(L, MEM_ROWS), 1, L, False)
    else:
        nbm = SSD_CHUNK // L
        x = mem_attend(x, mix, w_mix, sample["mem_kt"], sample["mem_vt"], p, nbm * L, nbm, L, True,
                       sample["layer"])

    ft = min(rows, FFN_ROWS)
    if sample is None:
        x, u_tail = ffn(x, None, p, ft, True, L // ft, L, SUBLANES)
        ffn_conv = u_tail.reshape(b, L // ft, SUBLANES, D_FF)[:, -1, SUBLANES - (FFN_CONV - 1):, :]
    else:
        x, u_full = ffn(x, _pad_rows(sample["ffn_conv"], L), p, ft, False, 1, L, ft)
        ffn_conv = u_full.reshape(b, L, D_FF)[:, L - (FFN_CONV - 1):, :]

    xbc3 = xbc.reshape(b, L, SSD_CONV_DIM)
    out = dict(
        x=x.reshape(b, L, d), lat=c_kv.reshape(b, L, MLA_KV_RANK), kr=kr.reshape(b, L, MLA_ROPE),
        ssd_conv=xbc3[:, L - (SSD_CONV - 1):, :], ssd=ssd_h.reshape(b, SSD_HEADS, SSD_HEADDIM, SSD_STATE),
        gla=jnp.swapaxes(gla_st.reshape(b, GLA_HEADS, GLA_DV, GLA_DK), 2, 3),
        ffn_conv=ffn_conv)
    if sample is None:
        out["mem_k"] = mk.reshape(b, N_MEM, MEM_HEADS, MEM_HEAD_DIM)
        out["mem_v"] = mv.reshape(b, N_MEM, MEM_HEADS, MEM_HEAD_DIM)
    return out


def kernel(x_prompt, x_sample, cache_mla_latent, cache_mla_krope, cache_mem_k, cache_mem_v, state_ssd_conv, state_ssd, state_gla, state_ffn_conv, page_table, mem_prompt, norm_mix, w_in, mla_q_norm, mla_w_uq, mla_kv_norm, mla_w_uk, mla_w_uv, mla_qn_norm, mla_qr_norm, mla_kn_norm, mla_kr_norm, ssd_conv_w, ssd_conv_b, ssd_dt_bias, ssd_a_log, ssd_d, ssd_norm, gla_w_gate, gla_b_gate, gla_norm, w_o, norm_mem, mem_wq, mem_wk, mem_wv, mem_wo, mem_q_norm, mem_k_norm, norm_ffn, ffn_w_up, ffn_conv_w, ffn_conv_b, ffn_w_down):
    weights = dict(
        norm_mix=norm_mix, w_in=w_in, mla_q_norm=mla_q_norm, mla_w_uq=mla_w_uq, mla_kv_norm=mla_kv_norm,
        mla_w_uk=mla_w_uk, mla_w_uv=mla_w_uv, mla_qn_norm=mla_qn_norm, mla_qr_norm=mla_qr_norm,
        mla_kn_norm=mla_kn_norm, mla_kr_norm=mla_kr_norm, ssd_conv_w=ssd_conv_w, ssd_conv_b=ssd_conv_b,
        ssd_dt_bias=ssd_dt_bias, ssd_a_log=ssd_a_log, ssd_d=ssd_d, ssd_norm=ssd_norm, gla_w_gate=gla_w_gate,
        gla_b_gate=gla_b_gate, gla_norm=gla_norm, w_o=w_o, norm_mem=norm_mem, mem_wq=mem_wq, mem_wk=mem_wk,
        mem_wv=mem_wv, mem_wo=mem_wo, mem_q_norm=mem_q_norm, mem_k_norm=mem_k_norm, norm_ffn=norm_ffn,
        ffn_w_up=ffn_w_up, ffn_conv_w=ffn_conv_w, ffn_conv_b=ffn_conv_b, ffn_w_down=ffn_w_down)
    depth = w_in.shape[0]
    bp, lp, _ = x_prompt.shape
    bs, ls, _ = x_sample.shape
    assert ls >= SSD_CONV - 1 and SSD_CHUNK % ls == 0
    past_len = page_table.shape[1] * PAGE_SIZE
    tm_p, tm_s = PROJ_ROWS, min(PROJ_ROWS, bs * ls)
    cos_p, sin_p = _rope_tables(jnp.arange(lp))
    cos_s, sin_s = _rope_tables(past_len + jnp.arange(ls))
    cos_s, sin_s = jnp.tile(cos_s, (tm_s // ls, 1)), jnp.tile(sin_s, (tm_s // ls, 1))
    mem2d = mem_prompt.reshape(bp * N_MEM, D_MODEL)
    cache_krt = jnp.swapaxes(cache_mla_krope, 2, 3)
    mem_kt = jnp.transpose(cache_mem_k, (0, 1, 3, 4, 2)).reshape(depth, bs, MEM_D, N_MEM)
    mem_vt = jnp.transpose(cache_mem_v, (0, 1, 3, 4, 2)).reshape(depth, bs, MEM_D, N_MEM)

    packed = jax.vmap(_pack_one)(weights)
    xp, xs = x_prompt, x_sample
    outs_p, outs_s = [], []
    for l in range(depth):
        p = {name: _Stacked(arr, l) for name, arr in packed.items()}
        op = _layer(xp, p, cos_p, sin_p, tm_p, None, mem2d)
        xp = op["x"]
        outs_p.append(op)
        sample = dict(layer=l, page_table=page_table, cache_lat=cache_mla_latent, cache_krt=cache_krt,
                      mem_kt=mem_kt, mem_vt=mem_vt, ssd_conv=state_ssd_conv[l], ssd_all=state_ssd,
                      gla=state_gla[l], ffn_conv=state_ffn_conv[l])
        os_ = _layer(xs, p, cos_s, sin_s, tm_s, sample)
        xs = os_["x"]
        outs_s.append(os_)

    st = lambda outs, key: jnp.stack([o[key] for o in outs])
    return (xp, xs,
            st(outs_p, "lat"), st(outs_p, "kr"), st(outs_p, "mem_k"), st(outs_p, "mem_v"),
            st(outs_p, "ssd_conv"), st(outs_p, "ssd"), st(outs_p, "gla"), st(outs_p, "ffn_conv"),
            st(outs_s, "lat"), st(outs_s, "kr"), st(outs_s, "ssd_conv"), st(outs_s, "ssd"), st(outs_s, "gla"),
            st(outs_s, "ffn_conv"))
```

```python
import functools
import math

import jax
import jax.numpy as jnp
from jax import lax
from jax.experimental import pallas as pl
from jax.experimental.pallas import tpu as pltpu

F32 = jnp.float32
BF16 = jnp.bfloat16
EPS = 1e-6
LANES = 128
SUBLANES = 8
VMEM_LIMIT = 56 * 1024 * 1024

D_MODEL = 1024
MLA_HEADS, MLA_NOPE, MLA_ROPE, MLA_V = 8, 64, 32, 64
MLA_QK = MLA_NOPE + MLA_ROPE
MLA_Q_RANK, MLA_KV_RANK = 256, 128
ROPE_THETA = 10000.0
PAGE_SIZE = 128
SSD_HEADS, SSD_HEADDIM, SSD_GROUPS, SSD_STATE, SSD_CONV = 4, 64, 2, 128, 4
SSD_D = SSD_HEADS * SSD_HEADDIM
SSD_CONV_DIM = SSD_D + 2 * SSD_GROUPS * SSD_STATE
SSD_CHUNK = 128
GLA_HEADS, GLA_DK, GLA_DV = 4, 32, 64
GLA_KD, GLA_VD = GLA_HEADS * GLA_DK, GLA_HEADS * GLA_DV
GLA_GATE_RANK, GLA_GATE_TAU, GLA_CHUNK = 16, 16.0, 64
N_MEM, MEM_HEADS, MEM_HEAD_DIM = 256, 4, 64
MEM_D = MEM_HEADS * MEM_HEAD_DIM
D_FF, FFN_CONV = 2816, 3
FFN_CHUNK = 256
FFN_ROWS = 256
PROJ_ROWS = 512
MEM_ROWS = 1024
FLASH_Q, FLASH_K = 1024, 256
GLA_ROWS = 256
SCAN_STREAMS_PROMPT, SCAN_STREAMS_SAMPLE = 4, 2

HI = lax.Precision.HIGHEST


def _cparams(*sem):
    return pltpu.CompilerParams(dimension_semantics=sem, vmem_limit_bytes=VMEM_LIMIT)


def _dot(a, b, precision=None):
    return jnp.dot(a, b, preferred_element_type=F32, precision=precision)


def _dot_nt(a, b):
    return lax.dot_general(a, b, (((1,), (1,)), ((), ())), preferred_element_type=F32)


def _dot_tn(a, b):
    return lax.dot_general(a, b, (((0,), (0,)), ((), ())), preferred_element_type=F32)


def _split3(x):
    x1 = x.astype(BF16)
    r1 = x - x1.astype(F32)
    x2 = r1.astype(BF16)
    x3 = (r1 - x2.astype(F32)).astype(BF16)
    return x1, x2, x3


def _dot_sel(sel, x):
    return sum(_dot(sel, t) for t in _split3(x))


def _dot_rep(x, sel):
    return sum(_dot(t, sel) for t in _split3(x))


def _seg_total(cs, n_seg, seg_len):
    return jnp.concatenate(
        [jnp.broadcast_to(cs[(i + 1) * seg_len - 1:(i + 1) * seg_len, :], (seg_len, cs.shape[1]))
         for i in range(n_seg)], axis=0)


def _rms(x, g):
    return x * lax.rsqrt(jnp.mean(x * x, axis=-1, keepdims=True) + EPS) * g


def _sigmoid(x):
    return 1.0 / (1.0 + jnp.exp(-x))


def _silu(x):
    return x * _sigmoid(x)


def _softplus(x):
    return jnp.maximum(x, 0.0) + jnp.log(1.0 + jnp.exp(-jnp.abs(x)))


def _log_sigmoid(x):
    return jnp.minimum(x, 0.0) - jnp.log(1.0 + jnp.exp(-jnp.abs(x)))


class _Stacked:
    def __init__(self, arr, layer):
        self.arr, self.layer = arr, layer

    @property
    def shape(self):
        return self.arr.shape[1:]


def _wspec(w, resident=False):
    shp, layer = w.shape, w.layer
    kw = dict(pipeline_mode=pl.Buffered(1)) if resident else {}
    return pl.BlockSpec((None,) + shp, lambda *_: (layer,) + (0,) * len(shp), **kw)


def _in_proj_body(x_ref, g_ref, wm_ref, wz_ref, wx_ref, wdt_ref, wg_ref, cos_ref, sin_ref,
                  qng_ref, wq_ref, gq_ref, kvg_ref, wk_ref, gk_ref, krg_ref, mh_ref,
                  z_out, xbc_out, dt_out, hg_out, q_out, k_out, ckv_out, cbf_out, kr_out):
    tm = x_ref.shape[0]
    xb = _rms(x_ref[...], g_ref[...]).astype(BF16)
    z_out[...] = _dot(xb, wz_ref[...])
    xbc_out[...] = _dot(xb, wx_ref[...])
    dt_out[...] = _dot(xb, wdt_ref[...])
    hg_out[...] = _dot(xb, wg_ref[...])
    h = _dot(xb, wm_ref[...])
    cq, ckv, krr = h[:, 0:256], h[:, 256:384], h[:, 384:512]
    cos, sin = cos_ref[...], sin_ref[...]
    lane = lax.broadcasted_iota(jnp.int32, (tm, LANES), 1)
    mh = mh_ref[...]

    def rope(xv):
        rot = jnp.where(lane < MLA_NOPE + MLA_ROPE // 2,
                        pltpu.roll(xv, LANES - MLA_ROPE // 2, 1), pltpu.roll(xv, MLA_ROPE // 2, 1))
        return xv * cos + rot * sin

    def seg_norm(xv, g):
        msq = _dot((xv * xv).astype(BF16), mh)
        return xv * lax.rsqrt(msq + EPS) * g

    q = _dot(_rms(cq, qng_ref[...]).astype(BF16), wq_ref[...])
    c = _rms(ckv, kvg_ref[...])
    cb = c.astype(BF16)
    ckv_out[...] = c
    cbf_out[...] = jnp.concatenate([cb, jnp.ones_like(cb)], axis=1)
    kr = rope(seg_norm(krr, krg_ref[...]))
    kr_out[...] = kr[:, MLA_NOPE:MLA_QK]
    k = _dot(cb, wk_ref[...])
    gq, gk = gq_ref[...], gk_ref[...]
    for hd in range(MLA_HEADS):
        sl = slice(hd * LANES, (hd + 1) * LANES)
        q_out[:, sl] = rope(seg_norm(q[:, sl], gq)).astype(BF16)
        k_out[:, sl] = (seg_norm(k[:, sl], gk) + kr).astype(BF16)


def in_proj(x, cos_t, sin_t, p, tm):
    rows, d = x.shape
    assert rows % tm == 0
    nt = cos_t.shape[0] // tm
    row_spec = lambda n: pl.BlockSpec((tm, n), lambda i: (i, 0))
    tab_spec = pl.BlockSpec((tm, LANES), lambda i: (i % nt, 0))
    hw = MLA_HEADS * LANES
    proj = ("norm_mix", "w_mla", "w_z", "w_xbc", "w_dt", "w_gla")
    prep = ("mla_q_norm", "wq_pad", "gq_vec", "mla_kv_norm", "wk_pad", "gk_vec", "gkr_vec", "mh")
    widths = [(p["w_z"].shape[1], F32), (p["w_xbc"].shape[1], F32), (p["w_dt"].shape[1], F32),
              (p["w_gla"].shape[1], F32), (hw, BF16), (hw, BF16), (LANES, F32), (2 * LANES, BF16), (MLA_ROPE, F32)]
    return pl.pallas_call(
        _in_proj_body, grid=(rows // tm,),
        in_specs=[row_spec(d)] + [_wspec(p[n]) for n in proj] + [tab_spec, tab_spec] + [_wspec(p[n]) for n in prep],
        out_specs=[row_spec(n) for n, _ in widths],
        out_shape=[jax.ShapeDtypeStruct((rows, n), dt) for n, dt in widths],
        compiler_params=_cparams("parallel"), name="in_proj",
    )(x, *[p[n].arr for n in proj], cos_t, sin_t, *[p[n].arr for n in prep])


def _mla_prompt_body(q_ref, k_ref, c_ref, wuv_ref, o_ref, m_ref, l_ref, acc_ref, olat_ref, *, tq, tk):
    qi = pl.program_id(1)

    def causal(n_rows):
        return (lax.broadcasted_iota(jnp.int32, (n_rows, tk), 1)
                <= lax.broadcasted_iota(jnp.int32, (n_rows, tk), 0))

    m_ref[...] = jnp.full(m_ref.shape, -jnp.inf, F32)
    l_ref[...] = jnp.zeros(l_ref.shape, F32)
    acc_ref[...] = jnp.zeros(acc_ref.shape, F32)

    def step(j, r0, masked):
        start = pl.multiple_of(j * tk, tk)
        rows = slice(r0, tq)
        cblk = c_ref[0, pl.ds(start, tk), :]
        for hd in range(MLA_HEADS):
            sl = slice(hd * LANES, (hd + 1) * LANES)
            s = _dot_nt(q_ref[0, rows, sl], k_ref[0, pl.ds(start, tk), sl])
            if masked:
                s = jnp.where(causal(tq - r0), s, -jnp.inf)
            m_old = m_ref[hd, rows, :]
            m_new = jnp.maximum(m_old, jnp.max(s, axis=-1, keepdims=True))
            p = jnp.exp2(s - jnp.concatenate([m_new] * (tk // LANES), axis=1))
            corr = jnp.exp2(m_old - m_new)
            pv = _dot(p.astype(BF16), cblk)
            acc_ref[hd, rows, :] = acc_ref[hd, rows, :] * corr + pv[:, :LANES]
            l_ref[hd, rows, :] = l_ref[hd, rows, :] * corr + pv[:, LANES:]
            m_ref[hd, rows, :] = m_new

    def body(j, carry):
        step(j, 0, False)
        return carry

    n_full = qi * (tq // tk)
    lax.fori_loop(0, n_full, body, 0)
    for d in range(tq // tk):
        step(n_full + d, d * tk, True)
    for hd in range(MLA_HEADS):
        olat_ref[:, hd * LANES:(hd + 1) * LANES] = (acc_ref[hd] / l_ref[hd]).astype(BF16)
    o_ref[0] = _dot(olat_ref[...], wuv_ref[...]).astype(o_ref.dtype)


def mla_prompt(q_cat, k_cat, c_bf, wuv_bd, tq, tk):
    b, L, hw = q_cat.shape
    assert tq % tk == 0 and L % tq == 0
    return pl.pallas_call(
        functools.partial(_mla_prompt_body, tq=tq, tk=tk), grid=(b, L // tq),
        in_specs=[pl.BlockSpec((1, tq, hw), lambda bi, qi: (bi, qi, 0)),
                  pl.BlockSpec((1, L, hw), lambda bi, qi: (bi, 0, 0)),
                  pl.BlockSpec((1, L, 2 * LANES), lambda bi, qi: (bi, 0, 0)),
                  _wspec(wuv_bd)],
        out_specs=pl.BlockSpec((1, tq, MLA_HEADS * MLA_V), lambda bi, qi: (bi, qi, 0)),
        out_shape=jax.ShapeDtypeStruct((b, L, MLA_HEADS * MLA_V), BF16),
        scratch_shapes=[pltpu.VMEM((MLA_HEADS, tq, LANES), F32), pltpu.VMEM((MLA_HEADS, tq, LANES), F32),
                        pltpu.VMEM((MLA_HEADS, tq, LANES), F32), pltpu.VMEM((tq, hw), BF16)],
        compiler_params=_cparams("parallel", "arbitrary"), name="mla_prompt",
    )(q_cat, k_cat, c_bf, wuv_bd.arr)


def _mem_kv_body(x_ref, wk_ref, wv_ref, m_ref, g_ref, k_out, v_out):
    xb = x_ref[...].astype(BF16)
    k = _dot(xb, wk_ref[...])
    msq = _dot((k * k).astype(BF16), m_ref[...])
    k_out[...] = k * lax.rsqrt(msq + EPS) * g_ref[...]
    v_out[...] = _dot(xb, wv_ref[...])


def mem_kv(mem2d, p, tm):
    rows, d = mem2d.shape
    row = lambda n: pl.BlockSpec((tm, n), lambda i: (i, 0))
    names = ("mem_wk", "mem_wv", "m64", "mem_k_gain")
    return pl.pallas_call(
        _mem_kv_body, grid=(rows // tm,),
        in_specs=[row(d)] + [_wspec(p[n]) for n in names],
        out_specs=[row(MEM_D), row(MEM_D)],
        out_shape=[jax.ShapeDtypeStruct((rows, MEM_D), F32)] * 2,
        compiler_params=_cparams("parallel"), name="mem_kv",
    )(mem2d, *[p[n].arr for n in names])


def _mem_attend_body(x_ref, *refs, nb, kv_t, n_pre):
    a_refs, w_refs = refs[:n_pre], refs[n_pre:2 * n_pre]
    g_ref, wq_ref, m_ref, gq_ref, k_ref, v_ref, wo_ref, o_ref = refs[2 * n_pre:]
    r = x_ref.shape[0]
    x = x_ref[...]
    for a_ref, w_ref in zip(a_refs, w_refs):
        x = x + _dot(a_ref[...], w_ref[...])
    q = _dot(_rms(x, g_ref[...]).astype(BF16), wq_ref[...])
    msq = _dot((q * q).astype(BF16), m_ref[...])
    qn = (q * lax.rsqrt(msq + EPS) * gq_ref[...]).astype(BF16)
    if kv_t:
        kb = jnp.concatenate([k_ref[b] for b in range(nb)], axis=1).astype(BF16)
        vb = jnp.concatenate([v_ref[b] for b in range(nb)], axis=1).astype(BF16)
    else:
        kb = k_ref[...].astype(BF16)
        vb = v_ref[...].astype(BF16)
    lane = lax.broadcasted_iota(jnp.int32, (r, MEM_D), 1) // MEM_HEAD_DIM
    if nb > 1:
        rb = lax.broadcasted_iota(jnp.int32, (r, nb * N_MEM), 0) // (r // nb)
        cb = lax.broadcasted_iota(jnp.int32, (r, nb * N_MEM), 1) // N_MEM
        same = rb == cb
    o = jnp.zeros((r, MEM_D), F32)
    for hd in range(MEM_HEADS):
        qm = jnp.where(lane == hd, qn, jnp.zeros_like(qn))
        s = _dot(qm, kb) if kv_t else _dot_nt(qm, kb)
        if nb > 1:
            s = jnp.where(same, s, -jnp.inf)
        pr = jnp.exp(s - jnp.max(s, axis=-1, keepdims=True))
        prb = pr.astype(BF16)
        pv = (_dot_nt(prb, vb) if kv_t else _dot(prb, vb)) / jnp.sum(pr, axis=-1, keepdims=True)
        o = jnp.where(lane == hd, pv, o)
    o_ref[...] = x + _dot(o.astype(BF16), wo_ref[...])


def mem_attend(x, acts, ws, k, v, p, r, nb, rows_per_batch, kv_t, layer=0):
    rows, d = x.shape
    row = lambda n: pl.BlockSpec((r, n), lambda i: (i, 0))
    names = ("norm_mem", "mem_wq", "m64", "mem_q_gain")
    if kv_t:
        kv_spec = pl.BlockSpec((None, nb, MEM_D, N_MEM), lambda i: (layer, i, 0, 0))
    elif nb == 1:
        kv_spec = pl.BlockSpec((N_MEM, MEM_D), lambda i: (i // (rows_per_batch // r), 0))
    else:
        kv_spec = pl.BlockSpec((nb * N_MEM, MEM_D), lambda i: (i, 0))
    return pl.pallas_call(
        functools.partial(_mem_attend_body, nb=nb, kv_t=kv_t, n_pre=len(acts)), grid=(rows // r,),
        in_specs=[row(d)] + [row(a.shape[1]) for a in acts] + [_wspec(w) for w in ws]
        + [_wspec(p[n]) for n in names] + [kv_spec, kv_spec, _wspec(p["mem_wo"])],
        out_specs=row(d),
        out_shape=jax.ShapeDtypeStruct((rows, d), F32),
        compiler_params=_cparams("parallel"), name="mem_attend",
    )(x, *acts, *[w.arr for w in ws], *[p[n].arr for n in names], k, v, p["mem_wo"].arr)


def _ffn_body(x_ref, g_ref, *refs, carry, tiles_per_seq, seg, tail_rows):
    if carry:
        wup_ref, wd_ref, cw_ref, cb_ref, o_ref, tail_ref, xn_ref, carry_ref = refs
        halo_ref = None
    else:
        halo_ref, wup_ref, wd_ref, cw_ref, cb_ref, o_ref, tail_ref, xn_ref = refs
        carry_ref = None
    tm = x_ref.shape[0]
    x = x_ref[...]
    xn_ref[...] = _rms(x, g_ref[...]).astype(BF16)
    row = lax.broadcasted_iota(jnp.int32, (tm, FFN_CHUNK), 0)
    if carry:
        @pl.when(pl.program_id(0) % tiles_per_seq == 0)
        def _():
            carry_ref[...] = jnp.zeros_like(carry_ref)
    else:
        t = row % seg
    n_chunks = D_FF // FFN_CHUNK

    def up(j):
        sl = slice(j * FFN_CHUNK, (j + 1) * FFN_CHUNK)
        xn = xn_ref[...]
        return (_dot(xn, wup_ref[:, sl]),
                _dot(xn, wup_ref[:, D_FF + j * FFN_CHUNK:D_FF + (j + 1) * FFN_CHUNK]))

    acc = x
    uv = up(0)
    for j in range(n_chunks):
        sl = slice(j * FFN_CHUNK, (j + 1) * FFN_CHUNK)
        u, v = uv
        if j + 1 < n_chunks:
            uv = up(j + 1)
        um1 = pltpu.roll(u, 1, 0)
        um2 = pltpu.roll(u, 2, 0)
        if carry:
            c6 = carry_ref[6:7, sl]
            c7 = carry_ref[7:8, sl]
            um1 = jnp.where(row == 0, c7, um1)
            um2 = jnp.where(row == 0, c6, jnp.where(row == 1, c7, um2))
            carry_ref[:, sl] = u[tm - SUBLANES:, :]
        else:
            hal = halo_ref[:, sl]
            um1 = jnp.where(t >= 1, um1, pltpu.roll(hal, tm - 1, 0))
            um2 = jnp.where(t >= 2, um2, hal)
        tail_ref[:, sl] = u[tm - tail_rows:, :]
        uc = um2 * cw_ref[0:1, sl] + um1 * cw_ref[1:2, sl] + u * cw_ref[2:3, sl] + cb_ref[:, sl]
        a = (_silu(uc) * v).astype(BF16)
        acc = acc + _dot(a, wd_ref[sl, :])
    o_ref[...] = acc


def ffn(x, halo, p, tm, carry, tiles_per_seq, seg, tail_rows):
    rows, d = x.shape
    n_tiles = rows // tm
    row_spec = lambda n: pl.BlockSpec((tm, n), lambda i: (i, 0))
    in_specs = [row_spec(d), _wspec(p["norm_ffn"])]
    args = [x, p["norm_ffn"].arr]
    if not carry:
        in_specs.append(row_spec(D_FF))
        args.append(halo)
    in_specs += [_wspec(p[n], resident=True) for n in ("ffn_wup", "ffn_wd")]
    in_specs += [_wspec(p["ffn_conv_w"]), _wspec(p["ffn_conv_b"])]
    args += [p[n].arr for n in ("ffn_wup", "ffn_wd", "ffn_conv_w", "ffn_conv_b")]
    scratch = [pltpu.VMEM((tm, d), BF16)]
    if carry:
        scratch.append(pltpu.VMEM((SUBLANES, D_FF), F32))
    return pl.pallas_call(
        functools.partial(_ffn_body, carry=carry, tiles_per_seq=tiles_per_seq, seg=seg, tail_rows=tail_rows),
        grid=(n_tiles,), in_specs=in_specs,
        out_specs=[row_spec(d), pl.BlockSpec((tail_rows, D_FF), lambda i: (i, 0))],
        out_shape=[jax.ShapeDtypeStruct((rows, d), F32), jax.ShapeDtypeStruct((n_tiles * tail_rows, D_FF), F32)],
        scratch_shapes=scratch,
        compiler_params=_cparams("arbitrary"), name="ffn",
    )(*args)


def _ssd_body(*refs, ns, nb, seg, carry, has_state):
    z_ref, xbc_ref, dtr_ref = refs[0:3]
    i = 3
    halo_ref = None
    if not carry:
        halo_ref = refs[i]
        i += 1
    consts = refs[i:i + 7]
    i += 7
    h0_ref = None
    if has_state:
        h0_ref = refs[i]
        i += 1
    o_ref, h_ref = refs[i:i + 2]
    tail_ref = refs[i + 2] if carry else None

    @pl.when(pl.program_id(1) == 0)
    def _():
        h_ref[...] = h0_ref[...] if has_state else jnp.zeros_like(h_ref)
        if carry:
            tail_ref[...] = jnp.zeros_like(tail_ref)

    for s in range(ns):
        _ssd_stream(z_ref.at[s], xbc_ref.at[s], dtr_ref.at[s], None if carry else halo_ref.at[s], consts,
                    o_ref.at[s], h_ref.at[s], tail_ref.at[s] if carry else None, nb=nb, seg=seg, carry=carry)


def _ssd_stream(z_ref, xbc_ref, dtr_ref, halo_ref, consts, o_ref, h_ref, tail_ref, *, nb, seg, carry):
    cw_ref, cb_ref, dtb_ref, alog_ref, d_ref, ng_ref, rep_ref = consts
    r = z_ref.shape[0]
    hp = SSD_HEADS * SSD_HEADDIM
    gp = hp // SSD_GROUPS

    x = xbc_ref[...]
    row = lax.broadcasted_iota(jnp.int32, (r, SSD_CONV_DIM), 0)
    row8 = lax.broadcasted_iota(jnp.int32, (SUBLANES, SSD_CONV_DIM), 0)

    def prev(k):
        xs = pltpu.roll(x, k, 0)
        if carry:
            tl = pltpu.roll(tail_ref[...], k, 0)
            top = jnp.where(row8 < k, tl, xs[:SUBLANES])
            return jnp.concatenate([top, xs[SUBLANES:]], axis=0)
        sh = SSD_CONV - 1 - k
        hal = halo_ref[...]
        hs_ = hal if sh == 0 else pltpu.roll(hal, r - sh, 0)
        return jnp.where(row % seg >= k, xs, hs_)

    u = (cb_ref[...] + x * cw_ref[3:4, :] + prev(1) * cw_ref[2:3, :] + prev(2) * cw_ref[1:2, :]
         + prev(3) * cw_ref[0:1, :])
    if carry:
        tail_ref[...] = x[r - SUBLANES:, :]
    u = _silu(u)

    ri = lax.broadcasted_iota(jnp.int32, (r, r), 0)
    ci = lax.broadcasted_iota(jnp.int32, (r, r), 1)
    if nb > 1:
        tril = jnp.logical_and((ri // seg) == (ci // seg), ci <= ri)
    else:
        tril = ci <= ri
    tri = jnp.where(tril, 1.0, 0.0).astype(BF16)

    dtc = _softplus(dtr_ref[...] + dtb_ref[...])
    csc = _dot_sel(tri, dtc * (-jnp.exp(alog_ref[...])))
    dt = _dot_rep(dtc, rep_ref[...])
    cs = _dot_rep(csc, rep_ref[...])
    tot = _seg_total(cs, nb, seg) if nb > 1 else _seg_total(cs, 1, r)

    lane = lax.broadcasted_iota(jnp.int32, (r, LANES), 1)
    rb = lax.broadcasted_iota(jnp.int32, (r, LANES), 0) // seg
    lo = lane < SSD_HEADDIM

    def pair(a, b):
        return jnp.where(lo, a, b)

    def hs(a, hd):
        return a[:, hd * LANES:(hd + 1) * LANES]

    zz = z_ref[...]
    hall = h_ref[...].astype(BF16)
    for g in range(SSD_GROUPS):
        h0i, h1i = 2 * g, 2 * g + 1
        gs = slice(g * LANES, (g + 1) * LANES)
        xg = u[:, gs]
        bg = u[:, SSD_D + g * SSD_STATE:SSD_D + (g + 1) * SSD_STATE].astype(BF16)
        cg = u[:, SSD_D + SSD_GROUPS * SSD_STATE + g * SSD_STATE:
               SSD_D + SSD_GROUPS * SSD_STATE + (g + 1) * SSD_STATE].astype(BF16)
        cbm = _dot_nt(cg, bg)
        w0 = (cbm * jnp.where(tril, jnp.exp(hs(cs, h0i) - hs(cs, h0i).T), 0.0)).astype(BF16)
        w1 = (cbm * jnp.where(tril, jnp.exp(hs(cs, h1i) - hs(cs, h1i).T), 0.0)).astype(BF16)
        dtp = pair(hs(dt, h0i), hs(dt, h1i))
        csp = pair(hs(cs, h0i), hs(cs, h1i))
        totp = pair(hs(tot, h0i), hs(tot, h1i))
        xdt = (xg * dtp).astype(BF16)
        y = pair(_dot(w0, xdt), _dot(w1, xdt))
        zst = _dot_nt(cg, hall)
        if nb == 1:
            yst = zst[:, g * gp:(g + 1) * gp]
        else:
            yst = jnp.zeros((r, gp), F32)
            for b in range(nb):
                yst = jnp.where(rb == b, zst[:, b * hp + g * gp:b * hp + (g + 1) * gp], yst)
        y = y + yst * jnp.exp(csp) + d_ref[:, gs] * xg
        y = y * _silu(zz[:, gs])
        o_ref[:, gs] = _rms(y, ng_ref[:, gs]).astype(o_ref.dtype)

        xw = (xg * (jnp.exp(totp - csp) * dtp)).astype(BF16)
        if nb > 1:
            xw = jnp.concatenate([jnp.where(rb == b, xw, jnp.zeros_like(xw)) for b in range(nb)], axis=1)
        dh = _dot_tn(xw, bg)
        for b in range(nb):
            r0 = b * seg if nb > 1 else 0
            dec = jnp.concatenate(
                [jnp.broadcast_to(jnp.exp(hs(tot, h0i)[r0:r0 + 1, :]), (SSD_HEADDIM, LANES)),
                 jnp.broadcast_to(jnp.exp(hs(tot, h1i)[r0:r0 + 1, :]), (SSD_HEADDIM, LANES))], axis=0)
            sl = slice(b * hp + g * gp, b * hp + (g + 1) * gp)
            h_ref[sl, :] = h_ref[sl, :] * dec + dh[b * gp:(b + 1) * gp, :]


def ssd(z, xbc, dtr, halo, h0, p, nb, seg, ns, layer=0):
    n_groups, lg, _ = z.shape
    r = SSD_CHUNK
    carry = halo is None
    n_chunks = lg // r
    assert lg % r == 0 and n_groups % ns == 0 and (carry or n_chunks == 1)
    hp = SSD_HEADS * SSD_HEADDIM
    row_spec = lambda n: pl.BlockSpec((ns, r, n), lambda b, c: (b, c, 0))
    st_spec = pl.BlockSpec((ns, nb * hp, SSD_STATE), lambda b, c: (b, 0, 0))
    in_specs = [row_spec(SSD_D), row_spec(SSD_CONV_DIM), row_spec(LANES)]
    args = [z, xbc, dtr]
    if not carry:
        in_specs.append(row_spec(SSD_CONV_DIM))
        args.append(halo)
    names = ("ssd_conv_w", "ssd_conv_b", "ssd_dt_bias", "ssd_a_log", "ssd_d", "ssd_norm", "ssd_rep")
    in_specs += [_wspec(p[n]) for n in names]
    args += [p[n].arr for n in names]
    if h0 is not None:
        in_specs.append(pl.BlockSpec((None, ns, nb * hp, SSD_STATE), lambda b, c: (layer, b, 0, 0)))
        args.append(h0)
    scratch = [pltpu.VMEM((ns, SUBLANES, SSD_CONV_DIM), F32)] if carry else []
    return pl.pallas_call(
        functools.partial(_ssd_body, ns=ns, nb=nb, seg=seg, carry=carry, has_state=h0 is not None),
        grid=(n_groups // ns, n_chunks), in_specs=in_specs,
        out_specs=[row_spec(SSD_D), st_spec],
        out_shape=[jax.ShapeDtypeStruct((n_groups, lg, SSD_D), BF16),
                   jax.ShapeDtypeStruct((n_groups, nb * hp, SSD_STATE), F32)],
        scratch_shapes=scratch,
        compiler_params=_cparams("parallel", "arbitrary"), name="ssd",
    )(*args)


def _gla_body(*refs, ns, nb, seg, c, has_state):
    hg3_ref, wg_ref, bg_ref, m_ref, ng_ref = refs[0:5]
    i = 5
    s0_ref = None
    if has_state:
        s0_ref = refs[i]
        i += 1
    o3_ref, stc_ref, st3_ref = refs[i:i + 3]
    r = hg3_ref.shape[1]

    def head_blocks():
        for s in range(ns):
            for b in range(nb):
                for hd in range(GLA_HEADS):
                    yield s, slice(b * GLA_VD + hd * GLA_DV, b * GLA_VD + (hd + 1) * GLA_DV), \
                        slice(hd * GLA_DK, (hd + 1) * GLA_DK)

    @pl.when(pl.program_id(1) == 0)
    def _():
        st3_ref[...] = jnp.zeros_like(st3_ref)
        if has_state:
            for s, rows, lanes in head_blocks():
                st3_ref[s, rows, lanes] = s0_ref[s, rows, :]

    sl_ = seg if nb > 1 else c
    ri = lax.broadcasted_iota(jnp.int32, (r, r), 0)
    ci = lax.broadcasted_iota(jnp.int32, (r, r), 1)
    same = (ri // sl_) == (ci // sl_)
    tril = jnp.logical_and(same, ci <= ri)
    tri = jnp.where(tril, 1.0, 0.0).astype(BF16)
    klane = lax.broadcasted_iota(jnp.int32, (r, GLA_KD), 1) // GLA_DK
    vlane = lax.broadcasted_iota(jnp.int32, (r, GLA_VD), 1) // GLA_DV
    rbv = lax.broadcasted_iota(jnp.int32, (r, GLA_VD), 0) // seg
    blk = (lax.broadcasted_iota(jnp.int32, (GLA_VD, GLA_KD), 0) // GLA_DV
           == lax.broadcasted_iota(jnp.int32, (GLA_VD, GLA_KD), 1) // GLA_DK)

    for s in range(ns):
        hg_ref, o_ref, st_ref = hg3_ref.at[s], o3_ref.at[s], st3_ref.at[s]
        q = hg_ref[:, 0:GLA_KD] * (GLA_DK ** -0.5)
        k = hg_ref[:, GLA_KD:2 * GLA_KD]
        vb = hg_ref[:, 2 * GLA_KD:2 * GLA_KD + GLA_VD].astype(BF16)
        gg = hg_ref[:, 2 * GLA_KD + GLA_VD:2 * GLA_KD + 2 * GLA_VD]
        glr = hg_ref[:, 2 * GLA_KD + 2 * GLA_VD:2 * GLA_KD + 2 * GLA_VD + LANES].astype(BF16)
        gate = _log_sigmoid(_dot(glr, wg_ref[...]) + bg_ref[...]) * (1.0 / GLA_GATE_TAU)
        bc = _dot_sel(tri, gate)
        tot = _seg_total(bc, r // sl_, sl_)
        qt = (q * jnp.exp(bc)).astype(BF16)
        kt = (k * jnp.exp(-bc)).astype(BF16)
        kd = (k * jnp.exp(tot - bc)).astype(BF16)
        etot = jnp.exp(tot)
        o = jnp.zeros((r, GLA_VD), F32)
        for hd in range(GLA_HEADS):
            a = _dot_nt(jnp.where(klane == hd, qt, jnp.zeros_like(qt)), kt)
            a = jnp.where(tril, a, 0.0).astype(BF16)
            o = o + _dot(a, jnp.where(vlane == hd, vb, jnp.zeros_like(vb)))
        if nb > 1:
            zs = _dot_nt(qt, st_ref[...].astype(BF16))
            ost = jnp.zeros((r, GLA_VD), F32)
            for b in range(nb):
                ost = jnp.where(rbv == b, zs[:, b * GLA_VD:(b + 1) * GLA_VD], ost)
            o = o + ost
            vexp = jnp.concatenate([jnp.where(rbv == b, vb, jnp.zeros_like(vb)) for b in range(nb)], axis=1)
            ds = _dot_tn(vexp, kd)
            for b in range(nb):
                sl = slice(b * GLA_VD, (b + 1) * GLA_VD)
                st_ref[sl, :] = st_ref[sl, :] * etot[b * seg:b * seg + 1, :] + jnp.where(blk, ds[sl, :], 0.0)
        else:
            st = st_ref[...]
            parts = []
            for sub in range(r // c):
                rs = slice(sub * c, (sub + 1) * c)
                parts.append(_dot_nt(qt[rs, :], st.astype(BF16)))
                st = st * etot[sub * c:sub * c + 1, :] + jnp.where(blk, _dot_tn(vb[rs, :], kd[rs, :]), 0.0)
            st_ref[...] = st
            o = o + jnp.concatenate(parts, axis=0)
        msq = _dot((o * o).astype(BF16), m_ref[...])
        on = o * lax.rsqrt(msq + EPS) * ng_ref[...]
        o_ref[...] = (on * _silu(gg)).astype(o_ref.dtype)

    @pl.when(pl.program_id(1) == pl.num_programs(1) - 1)
    def _():
        for s, rows, lanes in head_blocks():
            stc_ref[s, rows, :] = st3_ref[s, rows, lanes]


def gla(hg, s0, p, nb, seg, c, r, ns):
    n_groups, lg, width = hg.shape
    n_steps = lg // r
    assert lg % r == 0 and r % c == 0 and n_groups % ns == 0
    row_spec = lambda n: pl.BlockSpec((ns, r, n), lambda b, s: (b, s, 0))
    st_spec = pl.BlockSpec((ns, nb * GLA_VD, GLA_DK), lambda b, s: (b, 0, 0))
    names = ("gla_w_gate", "gla_b_gate", "m64", "gla_norm")
    in_specs = [row_spec(width)] + [_wspec(p[n]) for n in names]
    args = [hg] + [p[n].arr for n in names]
    if s0 is not None:
        in_specs.append(st_spec)
        args.append(s0)
    return pl.pallas_call(
        functools.partial(_gla_body, ns=ns, nb=nb, seg=seg, c=c, has_state=s0 is not None),
        grid=(n_groups // ns, n_steps), in_specs=in_specs,
        out_specs=[row_spec(GLA_VD), st_spec],
        out_shape=[jax.ShapeDtypeStruct((n_groups, lg, GLA_VD), BF16),
                   jax.ShapeDtypeStruct((n_groups, nb * GLA_VD, GLA_DK), F32)],
        scratch_shapes=[pltpu.VMEM((ns, nb * GLA_VD, GLA_KD), F32)],
        compiler_params=_cparams("parallel", "arbitrary"), name="gla",
    )(*args)


def _q_absorb_body(q_ref, wt_ref, sel_ref, qa_out, qr_out):
    for hd in range(MLA_HEADS):
        qh = q_ref[:, hd * LANES:(hd + 1) * LANES].astype(F32)
        qa_out[:, hd * LANES:(hd + 1) * LANES] = _dot(qh, wt_ref[hd], HI).astype(BF16)
        qr_out[:, hd * LANES:(hd + 1) * LANES] = _dot(qh, sel_ref[...], HI).astype(BF16)


def q_absorb(q_cat, p, tm):
    rows, hw = q_cat.shape
    row_spec = pl.BlockSpec((tm, hw), lambda i: (i, 0))
    return pl.pallas_call(
        _q_absorb_body, grid=(rows // tm,),
        in_specs=[row_spec, _wspec(p["wukt_g"]), _wspec(p["rope_sel"])],
        out_specs=[row_spec, row_spec],
        out_shape=[jax.ShapeDtypeStruct((rows, hw), BF16)] * 2,
        compiler_params=_cparams("parallel"), name="q_absorb",
    )(q_cat, p["wukt_g"].arr, p["rope_sel"].arr)


PAGED_CP = 64
PAGED_SUB = 256


def _mla_paged_body(pt_ref, ptn_ref, wukt_ref, qa_ref, qr_ref, cnew_ref, krnew_ref, wuv_ref, lat_hbm, kr_hbm,
                    o_ref, lat_buf, kr_buf, sem, m_ref, l_ref, acc_ref, *, layer, n_pages, n_batch, seq, cp):
    bi = pl.program_id(0)
    nc = n_pages // cp
    nq = MLA_HEADS * seq
    nk = MLA_HEADS * MLA_NOPE

    def copies(table, chunk, slot):
        out = []
        for pg in range(cp):
            pid = table[0, 0, chunk * cp + pg]
            dst = pl.ds(pg * PAGE_SIZE, PAGE_SIZE)
            out.append(pltpu.make_async_copy(lat_hbm.at[layer, pid], lat_buf.at[slot, dst, :], sem.at[0, slot]))
            out.append(pltpu.make_async_copy(kr_hbm.at[layer, pid], kr_buf.at[slot, :, dst], sem.at[1, slot]))
        return out

    def start(table, chunk, slot):
        for cpy in copies(table, chunk, slot):
            cpy.start()

    def wait(slot):
        for cpy in copies(pt_ref, 0, slot):
            cpy.wait()

    @pl.when(bi == 0)
    def _():
        start(pt_ref, 0, 0)

    m_ref[...] = jnp.full(m_ref.shape, -jnp.inf, F32)
    l_ref[...] = jnp.zeros(l_ref.shape, F32)
    acc_ref[...] = jnp.zeros(acc_ref.shape, F32)
    lhs = jnp.concatenate([wukt_ref[...], qa_ref[0]], axis=0)
    qr = qr_ref[0]

    def scores(latb, krt):
        big = _dot_nt(lhs, latb)
        kt = big[0:nk, :]
        ss = jnp.sum((kt * kt).reshape(MLA_HEADS, MLA_NOPE, kt.shape[1]), axis=1) * (1.0 / MLA_NOPE)
        rinv = lax.rsqrt(ss + EPS)
        rexp = jnp.concatenate([jnp.broadcast_to(rinv[hd:hd + 1, :], (seq, rinv.shape[1]))
                                for hd in range(MLA_HEADS)], axis=0)
        return big[nk:, :] * rexp + _dot(qr, krt.astype(BF16))

    def attend(lat, krt, mask):
        keys = lat.shape[0]
        sub = min(PAGED_SUB, keys)
        latb = lat.astype(BF16)
        s = jnp.concatenate([scores(latb[i * sub:(i + 1) * sub, :], krt[:, i * sub:(i + 1) * sub])
                             for i in range(keys // sub)], axis=1)
        if mask is not None:
            s = jnp.where(mask, s, -jnp.inf)
        m_old = m_ref[...]
        m_new = jnp.maximum(m_old, jnp.max(s, axis=-1, keepdims=True))
        pr = jnp.exp2(s - jnp.concatenate([m_new] * (keys // LANES), axis=1))
        corr = jnp.exp2(m_old - m_new)
        l_ref[...] = l_ref[...] * corr + jnp.sum(pr, axis=-1, keepdims=True)
        acc_ref[...] = acc_ref[...] * corr + _dot(pr.astype(BF16), latb)
        m_ref[...] = m_new

    lat_new = jnp.concatenate([cnew_ref[0], jnp.zeros((LANES - seq, MLA_KV_RANK), F32)], axis=0)
    qtok = lax.broadcasted_iota(jnp.int32, (nq, LANES), 0) % seq
    key = lax.broadcasted_iota(jnp.int32, (nq, LANES), 1)
    attend(lat_new, krnew_ref[0], key <= qtok)

    def step(slot, prefetch):
        prefetch()
        wait(slot)
        attend(lat_buf[slot], kr_buf[slot], None)

    def pair_body(jj, carry_):
        step(0, lambda: start(pt_ref, 2 * jj + 1, 1))
        step(1, lambda: start(pt_ref, 2 * jj + 2, 0))
        return carry_

    lax.fori_loop(0, nc // 2 - 1, pair_body, 0)
    step(0, lambda: start(pt_ref, nc - 1, 1))

    def next_batch_prefetch():
        @pl.when(bi + 1 < n_batch)
        def _():
            start(ptn_ref, 0, 0)

    step(1, next_batch_prefetch)

    olat = (acc_ref[...] / l_ref[...]).astype(BF16)
    o = jnp.zeros((seq, MLA_HEADS * MLA_V), F32)
    for hd in range(MLA_HEADS):
        o = o + _dot(olat[hd * seq:(hd + 1) * seq, :], wuv_ref[hd * LANES:(hd + 1) * LANES, :])
    o_ref[0] = o.astype(o_ref.dtype)


def mla_paged(page_table, wukt, qa, qr, c_new, krt_new, wuv_bd, cache_lat, cache_krt, layer):
    n_batch, n_pages = page_table.shape
    seq = c_new.shape[1]
    cp = min(PAGED_CP, n_pages // 2)
    assert n_pages % (2 * cp) == 0
    nq = MLA_HEADS * seq
    pt3 = page_table.reshape(n_batch, 1, n_pages)
    smem_spec = lambda f: pl.BlockSpec((1, 1, n_pages), f, memory_space=pltpu.SMEM)
    per_b = lambda shp: pl.BlockSpec((1,) + shp, lambda b: (b, 0, 0))
    kc = cp * PAGE_SIZE
    return pl.pallas_call(
        functools.partial(_mla_paged_body, layer=layer, n_pages=n_pages, n_batch=n_batch, seq=seq, cp=cp),
        grid=(n_batch,),
        in_specs=[smem_spec(lambda b: (b, 0, 0)),
                  smem_spec(lambda b: (jnp.minimum(b + 1, n_batch - 1), 0, 0)),
                  _wspec(wukt), per_b(qa.shape[1:]), per_b(qr.shape[1:]), per_b(c_new.shape[1:]),
                  per_b(krt_new.shape[1:]),
                  _wspec(wuv_bd),
                  pl.BlockSpec(memory_space=pl.ANY), pl.BlockSpec(memory_space=pl.ANY)],
        out_specs=per_b((seq, MLA_HEADS * MLA_V)),
        out_shape=jax.ShapeDtypeStruct((n_batch, seq, MLA_HEADS * MLA_V), BF16),
        scratch_shapes=[pltpu.VMEM((2, kc, MLA_KV_RANK), F32), pltpu.VMEM((2, MLA_ROPE, kc), F32),
                        pltpu.SemaphoreType.DMA((2, 2)),
                        pltpu.VMEM((nq, LANES), F32), pltpu.VMEM((nq, LANES), F32), pltpu.VMEM((nq, LANES), F32)],
        compiler_params=_cparams("arbitrary"), name="mla_paged",
    )(pt3, pt3, wukt.arr, qa, qr, c_new, krt_new, wuv_bd.arr, cache_lat, cache_krt)


def _block_diag_mean(n, blocks):
    idx = jnp.arange(n)
    m = jnp.zeros((n, n), F32)
    for start, size in blocks:
        inb = jnp.logical_and(idx >= start, idx < start + size)
        m = m + jnp.where(jnp.logical_and(inb[:, None], inb[None, :]), 1.0 / size, 0.0)
    return m.astype(BF16)


def _rope_tables(pos):
    half = MLA_ROPE // 2
    inv = ROPE_THETA ** (-jnp.arange(half, dtype=F32) / half)
    ang = pos.astype(F32)[:, None] * inv[None, :]
    cos, sin = jnp.cos(ang), jnp.sin(ang)
    n = pos.shape[0]
    one, zero = jnp.ones((n, MLA_NOPE), F32), jnp.zeros((n, MLA_NOPE), F32)
    tail1, tail0 = jnp.ones((n, LANES - MLA_QK), F32), jnp.zeros((n, LANES - MLA_QK), F32)
    return (jnp.concatenate([one, cos, cos, tail1], axis=1), jnp.concatenate([zero, -sin, sin, tail0], axis=1))


def _pack_one(w):
    f = lambda name: w[name]
    bf = lambda a: a.astype(BF16)
    p = {}
    w_in = f("w_in")
    o = 0
    cols = {}
    for name, size in (("cq", MLA_Q_RANK), ("ckv", MLA_KV_RANK), ("krr", MLA_ROPE), ("z", SSD_D), ("xbc", SSD_CONV_DIM),
                       ("dt", SSD_HEADS), ("gq", GLA_KD), ("gk", GLA_KD), ("gv", GLA_VD), ("glr", GLA_GATE_RANK),
                       ("gg", GLA_VD)):
        cols[name] = w_in[:, o:o + size]
        o += size
    d = w_in.shape[0]

    def place(a, axis, lo, total):
        cfg = [(0, 0)] * a.ndim
        cfg[axis] = (lo, total - lo - a.shape[axis])
        return jnp.pad(a, cfg)

    krr_pad = place(cols["krr"], 1, MLA_NOPE, LANES)
    glr_pad = place(cols["glr"], 1, 0, LANES)
    p["w_mla"] = bf(jnp.concatenate([cols["cq"], cols["ckv"], krr_pad], axis=1))
    p["w_z"] = bf(cols["z"])
    p["w_xbc"] = bf(cols["xbc"])
    p["w_dt"] = bf(place(cols["dt"], 1, 0, LANES))
    p["w_gla"] = bf(jnp.concatenate([cols["gq"], cols["gk"], cols["gv"], cols["gg"], glr_pad], axis=1))
    p["norm_mix"] = f("norm_mix").reshape(1, d)

    def head_pad(wm, n_real):
        return place(wm, 2, 0, LANES).reshape(wm.shape[0], MLA_HEADS * LANES)

    p["wq_pad"] = bf(head_pad(f("mla_w_uq"), MLA_QK))
    p["wk_pad"] = bf(head_pad(f("mla_w_uk"), MLA_NOPE))
    scale = MLA_QK ** -0.5 * math.log2(math.e)
    zpad = jnp.zeros((LANES - MLA_QK,), F32)
    p["gq_vec"] = (jnp.concatenate([f("mla_qn_norm"), f("mla_qr_norm"), zpad]) * scale).reshape(1, LANES)
    p["gk_vec"] = jnp.concatenate([f("mla_kn_norm"), jnp.zeros((LANES - MLA_NOPE,), F32)]).reshape(1, LANES)
    p["gkr_vec"] = jnp.concatenate([jnp.zeros((MLA_NOPE,), F32), f("mla_kr_norm"), zpad]).reshape(1, LANES)
    p["mla_q_norm"] = f("mla_q_norm").reshape(1, MLA_Q_RANK)
    p["mla_kv_norm"] = f("mla_kv_norm").reshape(1, MLA_KV_RANK)
    p["mh"] = _block_diag_mean(LANES, [(0, MLA_NOPE), (MLA_NOPE, MLA_ROPE)])
    w_uv = f("mla_w_uv")
    p["wuv_bd"] = bf(jnp.concatenate(
        [place(w_uv[:, hd, :], 1, hd * MLA_V, MLA_HEADS * MLA_V) for hd in range(MLA_HEADS)], axis=0))
    w_uk = f("mla_w_uk")
    wt = jnp.transpose(w_uk, (1, 2, 0)) * f("mla_kn_norm")[None, :, None]
    p["wukt_g"] = place(wt, 1, 0, LANES)
    p["wukt"] = bf(jnp.transpose(w_uk, (1, 2, 0)).reshape(MLA_HEADS * MLA_NOPE, MLA_KV_RANK))
    lane_i = jnp.arange(LANES)
    p["rope_sel"] = jnp.where(jnp.logical_and(lane_i[:, None] == lane_i[None, :] + MLA_NOPE,
                                              lane_i[None, :] < MLA_ROPE), 1.0, 0.0).astype(F32)

    p["ssd_conv_w"] = place(f("ssd_conv_w"), 0, 0, SUBLANES)
    p["ssd_conv_b"] = f("ssd_conv_b").reshape(1, SSD_CONV_DIM)
    lane_pad = lambda v: place(v, 0, 0, LANES).reshape(1, LANES)
    p["ssd_dt_bias"] = lane_pad(f("ssd_dt_bias"))
    p["ssd_a_log"] = lane_pad(f("ssd_a_log"))
    p["ssd_rep"] = bf(jnp.repeat(jnp.eye(LANES, SSD_HEADS, dtype=F32), LANES, axis=1))
    p["ssd_d"] = jnp.repeat(f("ssd_d"), SSD_HEADDIM).reshape(1, SSD_D)
    p["ssd_norm"] = f("ssd_norm").reshape(1, SSD_D)
    p["gla_w_gate"] = bf(place(f("gla_w_gate"), 0, 0, LANES))
    p["gla_b_gate"] = f("gla_b_gate").reshape(1, GLA_KD)
    p["gla_norm"] = f("gla_norm").reshape(1, GLA_VD)
    p["m64"] = _block_diag_mean(MEM_D, [(i * 64, 64) for i in range(4)])
    w_o = f("w_o")
    p["wo_mla"] = bf(w_o[:MLA_HEADS * MLA_V])
    p["wo_ssd"] = bf(w_o[MLA_HEADS * MLA_V:MLA_HEADS * MLA_V + SSD_D])
    p["wo_gla"] = bf(w_o[MLA_HEADS * MLA_V + SSD_D:])
    p["norm_mem"] = f("norm_mem").reshape(1, D_MODEL)
    p["mem_wq"], p["mem_wk"], p["mem_wv"], p["mem_wo"] = bf(f("mem_wq")), bf(f("mem_wk")), bf(f("mem_wv")), bf(f("mem_wo"))
    p["mem_q_gain"] = (jnp.tile(f("mem_q_norm"), MEM_HEADS) * MEM_HEAD_DIM ** -0.5).reshape(1, MEM_D)
    p["mem_k_gain"] = jnp.tile(f("mem_k_norm"), MEM_HEADS).reshape(1, MEM_D)
    p["norm_ffn"] = f("norm_ffn").reshape(1, D_MODEL)
    w_up = f("ffn_w_up")
    p["ffn_wup"], p["ffn_wd"] = bf(w_up), bf(f("ffn_w_down"))
    p["ffn_conv_w"] = place(f("ffn_conv_w"), 0, 0, SUBLANES)
    p["ffn_conv_b"] = f("ffn_conv_b").reshape(1, D_FF)
    return p


def _pad_rows(state, seq):
    b, k, c = state.shape
    return jnp.pad(state, ((0, 0), (0, seq - k), (0, 0))).reshape(b * seq, c)


def _layer(x3, p, cos_t, sin_t, tm, sample, mem2d=None):
    b, L, d = x3.shape
    rows = b * L
    x = x3.reshape(rows, d)
    z, xbc, dtr, h_gla, q_cat, k_cat, c_kv, c_bf, kr = in_proj(x, cos_t, sin_t, p, tm)
    hw = MLA_HEADS * LANES
    if sample is None:
        o_mla = mla_prompt(q_cat.reshape(b, L, hw), k_cat.reshape(b, L, hw), c_bf.reshape(b, L, 2 * LANES),
                           p["wuv_bd"], math.gcd(L, FLASH_Q), FLASH_K)
        o_mla = o_mla.reshape(rows, MLA_HEADS * MLA_V)
        g3 = lambda a: a.reshape(b, L, a.shape[-1])
        ns = math.gcd(b, SCAN_STREAMS_PROMPT)
        o_ssd, ssd_h = ssd(g3(z), g3(xbc), g3(dtr), None, None, p, 1, L, ns)
        o_gla, gla_st = gla(g3(h_gla), None, p, 1, L, GLA_CHUNK, GLA_ROWS, ns)
    else:
        qa, qrp = q_absorb(q_cat, p, tm)
        to_hq = lambda a, n: a.reshape(b, L, MLA_HEADS, LANES)[..., :n].transpose(0, 2, 1, 3).reshape(b, MLA_HEADS * L, n)
        krt_new = jnp.pad(jnp.swapaxes(kr.reshape(b, L, MLA_ROPE), 1, 2), ((0, 0), (0, 0), (0, LANES - L)))
        o_mla = mla_paged(sample["page_table"], p["wukt"], to_hq(qa, LANES), to_hq(qrp, MLA_ROPE),
                          c_kv.reshape(b, L, LANES), krt_new, p["wuv_bd"], sample["cache_lat"], sample["cache_krt"],
                          sample["layer"])
        o_mla = o_mla.reshape(rows, MLA_HEADS * MLA_V)
        nb = SSD_CHUNK // L
        ng = b // nb
        g3 = lambda a: a.reshape(ng, nb * L, a.shape[-1])
        ns = math.gcd(ng, SCAN_STREAMS_SAMPLE)
        o_ssd, ssd_h = ssd(g3(z), g3(xbc), g3(dtr), g3(_pad_rows(sample["ssd_conv"], L)),
                           sample["ssd_all"].reshape(-1, ng, nb * SSD_D, SSD_STATE), p, nb, L, ns, sample["layer"])
        st0 = jnp.swapaxes(sample["gla"], 2, 3).reshape(ng, nb * GLA_VD, GLA_DK)
        o_gla, gla_st = gla(g3(h_gla), st0, p, nb, L, nb * L, nb * L, ns)
    o_ssd, o_gla = o_ssd.reshape(rows, SSD_D), o_gla.reshape(rows, GLA_VD)
    mix, w_mix = [o_mla, o_ssd, o_gla], [p["wo_mla"], p["wo_ssd"], p["wo_gla"]]
    if sample is None:
        mk, mv = mem_kv(mem2d, p, PROJ_ROWS)
        x = mem_attend(x, mix, w_mix, mk, mv, p, math.gcd(L, MEM_ROWS), 1, L, False)
    else:
        nbm = SSD_CHUNK // L
        x = mem_attend(x, mix, w_mix, sample["mem_kt"], sample["mem_vt"], p, nbm * L, nbm, L, True,
                       sample["layer"])

    ft = min(rows, FFN_ROWS)
    if sample is None:
        x, u_tail = ffn(x, None, p, ft, True, L // ft, L, SUBLANES)
        ffn_conv = u_tail.reshape(b, L // ft, SUBLANES, D_FF)[:, -1, SUBLANES - (FFN_CONV - 1):, :]
    else:
        x, u_full = ffn(x, _pad_rows(sample["ffn_conv"], L), p, ft, False, 1, L, ft)
        ffn_conv = u_full.reshape(b, L, D_FF)[:, L - (FFN_CONV - 1):, :]

    xbc3 = xbc.reshape(b, L, SSD_CONV_DIM)
    out = dict(
        x=x.reshape(b, L, d), lat=c_kv.reshape(b, L, MLA_KV_RANK), kr=kr.reshape(b, L, MLA_ROPE),
        ssd_conv=xbc3[:, L - (SSD_CONV - 1):, :], ssd=ssd_h.reshape(b, SSD_HEADS, SSD_HEADDIM, SSD_STATE),
        gla=jnp.swapaxes(gla_st.reshape(b, GLA_HEADS, GLA_DV, GLA_DK), 2, 3),
        ffn_conv=ffn_conv)
    if sample is None:
        out["mem_k"] = mk.reshape(b, N_MEM, MEM_HEADS, MEM_HEAD_DIM)
        out["mem_v"] = mv.reshape(b, N_MEM, MEM_HEADS, MEM_HEAD_DIM)
    return out


def kernel(x_prompt, x_sample, cache_mla_latent, cache_mla_krope, cache_mem_k, cache_mem_v, state_ssd_conv, state_ssd, state_gla, state_ffn_conv, page_table, mem_prompt, norm_mix, w_in, mla_q_norm, mla_w_uq, mla_kv_norm, mla_w_uk, mla_w_uv, mla_qn_norm, mla_qr_norm, mla_kn_norm, mla_kr_norm, ssd_conv_w, ssd_conv_b, ssd_dt_bias, ssd_a_log, ssd_d, ssd_norm, gla_w_gate, gla_b_gate, gla_norm, w_o, norm_mem, mem_wq, mem_wk, mem_wv, mem_wo, mem_q_norm, mem_k_norm, norm_ffn, ffn_w_up, ffn_conv_w, ffn_conv_b, ffn_w_down):
    weights = dict(
        norm_mix=norm_mix, w_in=w_in, mla_q_norm=mla_q_norm, mla_w_uq=mla_w_uq, mla_kv_norm=mla_kv_norm,
        mla_w_uk=mla_w_uk, mla_w_uv=mla_w_uv, mla_qn_norm=mla_qn_norm, mla_qr_norm=mla_qr_norm,
        mla_kn_norm=mla_kn_norm, mla_kr_norm=mla_kr_norm, ssd_conv_w=ssd_conv_w, ssd_conv_b=ssd_conv_b,
        ssd_dt_bias=ssd_dt_bias, ssd_a_log=ssd_a_log, ssd_d=ssd_d, ssd_norm=ssd_norm, gla_w_gate=gla_w_gate,
        gla_b_gate=gla_b_gate, gla_norm=gla_norm, w_o=w_o, norm_mem=norm_mem, mem_wq=mem_wq, mem_wk=mem_wk,
        mem_wv=mem_wv, mem_wo=mem_wo, mem_q_norm=mem_q_norm, mem_k_norm=mem_k_norm, norm_ffn=norm_ffn,
        ffn_w_up=ffn_w_up, ffn_conv_w=ffn_conv_w, ffn_conv_b=ffn_conv_b, ffn_w_down=ffn_w_down)
    depth = w_in.shape[0]
    bp, lp, _ = x_prompt.shape
    bs, ls, _ = x_sample.shape
    assert ls >= SSD_CONV - 1 and SSD_CHUNK % ls == 0
    past_len = page_table.shape[1] * PAGE_SIZE
    tm_p, tm_s = PROJ_ROWS, min(PROJ_ROWS, bs * ls)
    cos_p, sin_p = _rope_tables(jnp.arange(lp))
    cos_s, sin_s = _rope_tables(past_len + jnp.arange(ls))
    cos_s, sin_s = jnp.tile(cos_s, (tm_s // ls, 1)), jnp.tile(sin_s, (tm_s // ls, 1))
    mem2d = mem_prompt.reshape(bp * N_MEM, D_MODEL)
    cache_krt = jnp.swapaxes(cache_mla_krope, 2, 3)
    mem_kt = jnp.transpose(cache_mem_k, (0, 1, 3, 4, 2)).reshape(depth, bs, MEM_D, N_MEM)
    mem_vt = jnp.transpose(cache_mem_v, (0, 1, 3, 4, 2)).reshape(depth, bs, MEM_D, N_MEM)

    packed = jax.vmap(_pack_one)(weights)
    xp, xs = x_prompt, x_sample
    outs_p, outs_s = [], []
    for l in range(depth):
        p = {name: _Stacked(arr, l) for name, arr in packed.items()}
        op = _layer(xp, p, cos_p, sin_p, tm_p, None, mem2d)
        xp = op["x"]
        outs_p.append(op)
        sample = dict(layer=l, page_table=page_table, cache_lat=cache_mla_latent, cache_krt=cache_krt,
                      mem_kt=mem_kt, mem_vt=mem_vt, ssd_conv=state_ssd_conv[l], ssd_all=state_ssd,
                      gla=state_gla[l], ffn_conv=state_ffn_conv[l])
        os_ = _layer(xs, p, cos_s, sin_s, tm_s, sample)
        xs = os_["x"]
        outs_s.append(os_)

    st = lambda outs, key: jnp.stack([o[key] for o in outs])
    return (xp, xs,
            st(outs_p, "lat"), st(outs_p, "kr"), st(outs_p, "mem_k"), st(outs_p, "mem_v"),
            st(outs_p, "ssd_conv"), st(outs_p, "ssd"), st(outs_p, "gla"), st(outs_p, "ffn_conv"),
            st(outs_s, "lat"), st(outs_s, "kr"), st(outs_s, "ssd_conv"), st(outs_s, "ssd"), st(outs_s, "gla"),
            st(outs_s, "ffn_conv"))
```
